```python
import math
import jax, jax.numpy as jnp
from jax import lax
import numpy as np

D_MODEL = 1024
BATCH = 8
SEQ = 2048
DEPTH = 2

FFN_DIM = 2816
RG_WIDTH = D_MODEL
RG_BLOCKS = 16
RG_BLOCK_DIM = RG_WIDTH // RG_BLOCKS
RG_C = 8.0
CONV_WIDTH = 4
ATT_GROUPS = ((128, 1), (512, 4), (2048, 16))
ATT_HEADS_PER_GROUP = 4
ATT_HEADS = ATT_HEADS_PER_GROUP * len(ATT_GROUPS)
ATT_HEAD_DIM = 64
ATT_WIDTH = ATT_HEADS * ATT_HEAD_DIM
DN_HEADS = 8
DN_HEAD_DIM = 128
DN_WIDTH = DN_HEADS * DN_HEAD_DIM
DN_CHUNK = 64
N_BRANCH = 3
EPS = 1e-6
NEG_INF = -1e30

IN_SPLITS = (RG_WIDTH, RG_WIDTH, ATT_WIDTH, ATT_WIDTH, ATT_WIDTH,
             DN_WIDTH, DN_WIDTH, DN_WIDTH, DN_WIDTH, DN_HEADS, DN_HEADS, N_BRANCH * D_MODEL)
IN_DIM = sum(IN_SPLITS)
IN_OFFSETS = tuple(int(v) for v in np.cumsum(IN_SPLITS)[:-1])
BRANCH_DIM = RG_WIDTH + ATT_WIDTH + DN_WIDTH
BRANCH_OFFSETS = (RG_WIDTH, RG_WIDTH + ATT_WIDTH)

kernel_name = 'hybrid_rglru_dilated_attn_gated_deltanet_macaron'


def rms_norm(x, g):
    xf = x.astype(jnp.float32)
    y = xf * lax.rsqrt(jnp.mean(xf * xf, axis=-1, keepdims=True) + EPS)
    return (y * g.astype(jnp.float32)).astype(x.dtype)


def l2_norm(x):
    xf = x.astype(jnp.float32)
    return xf * lax.rsqrt(jnp.sum(xf * xf, axis=-1, keepdims=True) + EPS)


def swiglu(x, w_gate, w_up, w_down):
    return (jax.nn.silu(x @ w_gate) * (x @ w_up)) @ w_down


def causal_depthwise_conv(x, w):
    K = w.shape[0]
    T = x.shape[1]
    xp = jnp.pad(x, ((0, 0), (K - 1, 0), (0, 0)))
    out = xp[:, 0:T] * w[0]
    for k in range(1, K):
        out = out + xp[:, k:k + T] * w[k]
    return out


def rg_lru(x, w_r, b_r, w_i, b_i, lam):
    B, T, C = x.shape
    xb = x.reshape(B, T, RG_BLOCKS, RG_BLOCK_DIM)
    r = jax.nn.sigmoid((jnp.einsum('btnc,ncd->btnd', xb, w_r).reshape(B, T, C) + b_r).astype(jnp.float32))
    i = jax.nn.sigmoid((jnp.einsum('btnc,ncd->btnd', xb, w_i).reshape(B, T, C) + b_i).astype(jnp.float32))
    log_a = -RG_C * r * jax.nn.softplus(-lam.astype(jnp.float32))
    a = jnp.exp(log_a)
    b = jnp.sqrt(-jnp.expm1(2.0 * log_a)) * (i * x.astype(jnp.float32))

    def combine(c1, c2):
        a1, b1 = c1
        a2, b2 = c2
        return a1 * a2, a2 * b1 + b2

    _, h = lax.associative_scan(combine, (a, b), axis=1)
    return h.astype(x.dtype)


def alibi_slopes():
    h = jnp.arange(1, ATT_HEADS + 1, dtype=jnp.float32)
    return jnp.exp2(-8.0 * h / ATT_HEADS)


def dilated_window_attention(q, k, v, slopes, window, dilation):
    B, T, H, hd = q.shape
    span = window // dilation
    L = T // dilation
    nblk = -(-L // span)
    Lp = nblk * span

    def to_blocks(t):
        t = t.reshape(B, L, dilation, H, hd).transpose(0, 2, 1, 3, 4)
        t = jnp.pad(t, ((0, 0), (0, 0), (0, Lp - L), (0, 0), (0, 0)))
        return t.reshape(B, dilation, nblk, span, H, hd)

    def with_prev(t):
        prev = jnp.pad(t, ((0, 0), (0, 0), (1, 0), (0, 0), (0, 0), (0, 0)))[:, :, :-1]
        return jnp.concatenate([prev, t], axis=3)

    qb = to_blocks(q)
    kw = with_prev(to_blocks(k))
    vw = with_prev(to_blocks(v))
    s = jnp.einsum('brnqhe,brnkhe->brnhqk', qb, kw).astype(jnp.float32)
    qi = jnp.arange(span)[:, None]
    kj = jnp.arange(2 * span)[None, :]
    delta = qi + span - kj
    blk = jnp.arange(nblk)[:, None, None]
    valid = (delta >= 0) & (delta <= span) & ((blk > 0) | (kj >= span))
    bias = -slopes[:, None, None] * (delta * dilation).astype(jnp.float32)
    s = jnp.where(valid[:, None], s + bias, NEG_INF)
    m = jnp.max(s, axis=-1, keepdims=True)
    p = jnp.exp(s - m)
    den = jnp.sum(p, axis=-1, keepdims=True)
    o = jnp.einsum('brnhqk,brnkhe->brnqhe', (p / den).astype(v.dtype), vw)
    lse = (m + jnp.log(den))[..., 0]
    o = o.reshape(B, dilation, Lp, H, hd)[:, :, :L].transpose(0, 2, 1, 3, 4).reshape(B, T, H, hd)
    lse = lse.transpose(0, 1, 2, 4, 3).reshape(B, dilation, Lp, H)[:, :, :L]
    lse = lse.transpose(0, 2, 1, 3).reshape(B, T, H)
    return o, lse


def chunk_gated_delta_rule(q, k, v, g_log, beta):
    B, T, H, dk = q.shape
    dv = v.shape[-1]
    C = DN_CHUNK
    N = T // C
    f32 = jnp.float32

    def chunks(t):
        t = t.astype(f32).reshape((B, N, C, H) + t.shape[3:])
        return jnp.moveaxis(t, 3, 1)

    qc, kc, vc = chunks(q), chunks(k), chunks(v)
    gc, bc = chunks(g_log), chunks(beta)
    gam = jnp.cumsum(gc, axis=-1)
    incl = jnp.tril(jnp.ones((C, C), bool))
    strict = jnp.tril(jnp.ones((C, C), bool), -1)
    diff = gam[..., :, None] - gam[..., None, :]
    decay = jnp.where(incl, jnp.exp(jnp.where(incl, diff, 0.0)), 0.0)
    kb = kc * bc[..., None]
    vb = vc * bc[..., None]
    a = jnp.where(strict, jnp.einsum('bhnid,bhnjd->bhnij', kb, kc) * decay, 0.0)
    rhs = jnp.concatenate([vb, kb * jnp.exp(gam)[..., None]], axis=-1)
    sol = lax.linalg.triangular_solve(a + jnp.eye(C, dtype=f32), rhs, left_side=True, lower=True,
                                      unit_diagonal=True)
    u0, wk = sol[..., :dv], sol[..., dv:]
    qk = jnp.where(incl, jnp.einsum('bhnid,bhnjd->bhnij', qc, kc) * decay, 0.0)
    q_dec = qc * jnp.exp(gam)[..., None]
    k_dec = kc * jnp.exp(gam[..., -1:] - gam)[..., None]
    c_dec = jnp.exp(gam[..., -1])
    xs = tuple(jnp.moveaxis(t, 2, 0) for t in (u0, wk, qk, q_dec, k_dec, c_dec))

    def step(S, inp):
        u0_n, w_n, qk_n, qd_n, kd_n, cd_n = inp
        u = u0_n - jnp.einsum('bhck,bhkv->bhcv', w_n, S)
        o = jnp.einsum('bhck,bhkv->bhcv', qd_n, S) + jnp.einsum('bhij,bhjv->bhiv', qk_n, u)
        S = S * cd_n[..., None, None] + jnp.einsum('bhck,bhcv->bhkv', kd_n, u)
        return S, o

    S0 = jnp.zeros((B, H, dk, dv), f32)
    _, o = lax.scan(step, S0, xs)
    return jnp.transpose(o, (1, 0, 3, 2, 4)).reshape(B, T, H, dv)


def hybrid_mixer(u, w_in, rg_conv_w, rg_conv_b, rg_w_r, rg_b_r, rg_w_i, rg_b_i, rg_lambda,
                 att_q_norm, att_k_norm, dn_conv_w, dn_a_log, dn_dt_bias, dn_out_norm, w_branch, w_out):
    B, T, _ = u.shape
    proj = u @ w_in
    (rg_x, rg_gate, aq, ak, av, dq, dk_, dv_, dz, d_beta, d_alpha, merge_logits) = jnp.split(
        proj, IN_OFFSETS, axis=-1)

    xa = causal_depthwise_conv(rg_x, rg_conv_w) + rg_conv_b
    ya = rg_lru(xa, rg_w_r, rg_b_r, rg_w_i, rg_b_i, rg_lambda) * jax.nn.gelu(rg_gate)

    hshape = (B, T, ATT_HEADS, ATT_HEAD_DIM)
    q = rms_norm(aq.reshape(hshape), att_q_norm) * (ATT_HEAD_DIM ** -0.5)
    k = rms_norm(ak.reshape(hshape), att_k_norm)
    v = av.reshape(hshape)
    slopes = alibi_slopes()
    outs, lses = [], []
    for g, (window, dilation) in enumerate(ATT_GROUPS):
        hs = slice(g * ATT_HEADS_PER_GROUP, (g + 1) * ATT_HEADS_PER_GROUP)
        o_g, lse_g = dilated_window_attention(q[:, :, hs], k[:, :, hs], v[:, :, hs], slopes[hs], window, dilation)
        outs.append(o_g)
        lses.append(lse_g)
    wts = jax.nn.softmax(jnp.stack(lses, axis=0), axis=0)
    o_att = jnp.stack(outs, axis=0) * wts[..., None].astype(v.dtype)
    yb = o_att.transpose(1, 2, 0, 3, 4).reshape(B, T, ATT_WIDTH)

    qkv = jax.nn.silu(causal_depthwise_conv(jnp.concatenate([dq, dk_, dv_], axis=-1), dn_conv_w))
    cq, ck, cv = jnp.split(qkv, 3, axis=-1)
    dshape = (B, T, DN_HEADS, DN_HEAD_DIM)
    qd = l2_norm(cq.reshape(dshape)) * (DN_HEAD_DIM ** -0.5)
    kd = l2_norm(ck.reshape(dshape))
    vd = cv.reshape(dshape)
    beta = jax.nn.sigmoid(d_beta.astype(jnp.float32))
    g_log = -jnp.exp(dn_a_log.astype(jnp.float32)) * jax.nn.softplus(
        d_alpha.astype(jnp.float32) + dn_dt_bias.astype(jnp.float32))
    o_dn = chunk_gated_delta_rule(qd, kd, vd, g_log, beta)
    yc = (rms_norm(o_dn, dn_out_norm) * jax.nn.silu(dz.reshape(dshape).astype(jnp.float32)))
    yc = yc.reshape(B, T, DN_WIDTH).astype(u.dtype)

    gates = jax.nn.sigmoid(merge_logits.reshape(B, T, N_BRANCH, D_MODEL))
    wa, wb, wc = jnp.split(w_branch, BRANCH_OFFSETS, axis=0)
    y = gates[:, :, 0] * (ya @ wa) + gates[:, :, 1] * (yb @ wb) + gates[:, :, 2] * (yc @ wc)
    return y @ w_out


def setup_inputs(seed: int = 0) -> dict:
    key = jax.random.key(seed)
    ks = jax.random.split(key, 32)
    f32 = jnp.float32
    L, D, F = DEPTH, D_MODEL, FFN_DIM

    def dense(k, shape, fan_in):
        return jax.random.normal(k, shape, f32) * (fan_in ** -0.5)

    def gain(k, shape):
        return 1.0 + 0.05 * jax.random.normal(k, shape, f32)

    def bias(k, shape):
        return 0.02 * jax.random.normal(k, shape, f32)

    x = jax.random.normal(ks[0], (BATCH, SEQ, D), f32)
    a0 = jax.random.uniform(ks[13], (L, RG_WIDTH), f32, 0.9, 0.999)
    s = a0 ** (1.0 / RG_C)
    rg_lambda = jnp.log(s) - jnp.log1p(-s)
    dn_a_log = jnp.log(jax.random.uniform(ks[17], (L, DN_HEADS), f32, 1.0, 16.0))
    dt = jnp.exp(jax.random.uniform(ks[18], (L, DN_HEADS), f32, math.log(1e-3), math.log(1e-1)))
    dn_dt_bias = dt + jnp.log(-jnp.expm1(-dt))
    w_branch = jnp.concatenate([
        dense(ks[20], (L, RG_WIDTH, D), RG_WIDTH),
        dense(ks[21], (L, ATT_WIDTH, D), ATT_WIDTH),
        dense(ks[22], (L, DN_WIDTH, D), DN_WIDTH)], axis=1)
    return {
        'x': x,
        'ffn1_norm': gain(ks[1], (L, D)),
        'ffn1_w_gate': dense(ks[2], (L, D, F), D),
        'ffn1_w_up': dense(ks[3], (L, D, F), D),
        'ffn1_w_down': dense(ks[4], (L, F, D), F),
        'mix_norm': gain(ks[5], (L, D)),
        'w_in': dense(ks[6], (L, D, IN_DIM), D),
        'rg_conv_w': dense(ks[7], (L, CONV_WIDTH, RG_WIDTH), CONV_WIDTH),
        'rg_conv_b': bias(ks[8], (L, RG_WIDTH)),
        'rg_w_r': dense(ks[9], (L, RG_BLOCKS, RG_BLOCK_DIM, RG_BLOCK_DIM), RG_BLOCK_DIM),
        'rg_b_r': bias(ks[10], (L, RG_WIDTH)),
        'rg_w_i': dense(ks[11], (L, RG_BLOCKS, RG_BLOCK_DIM, RG_BLOCK_DIM), RG_BLOCK_DIM),
        'rg_b_i': bias(ks[12], (L, RG_WIDTH)),
        'rg_lambda': rg_lambda,
        'att_q_norm': gain(ks[14], (L, ATT_HEADS, ATT_HEAD_DIM)),
        'att_k_norm': gain(ks[15], (L, ATT_HEADS, ATT_HEAD_DIM)),
        'dn_conv_w': dense(ks[16], (L, CONV_WIDTH, 3 * DN_WIDTH), CONV_WIDTH),
        'dn_a_log': dn_a_log,
        'dn_dt_bias': dn_dt_bias,
        'dn_out_norm': gain(ks[19], (L, DN_HEADS, DN_HEAD_DIM)),
        'w_branch': w_branch,
        'w_out': dense(ks[23], (L, D, D), D),
        'ffn2_norm': gain(ks[24], (L, D)),
        'ffn2_w_gate': dense(ks[25], (L, D, F), D),
        'ffn2_w_up': dense(ks[26], (L, D, F), D),
        'ffn2_w_down': dense(ks[27], (L, F, D), F),
    }


def reference(x, ffn1_norm, ffn1_w_gate, ffn1_w_up, ffn1_w_down, mix_norm, w_in,
              rg_conv_w, rg_conv_b, rg_w_r, rg_b_r, rg_w_i, rg_b_i, rg_lambda,
              att_q_norm, att_k_norm, dn_conv_w, dn_a_log, dn_dt_bias, dn_out_norm,
              w_branch, w_out, ffn2_norm, ffn2_w_gate, ffn2_w_up, ffn2_w_down):
    for l in range(DEPTH):
        x = x + 0.5 * swiglu(rms_norm(x, ffn1_norm[l]), ffn1_w_gate[l], ffn1_w_up[l], ffn1_w_down[l])
        x = x + hybrid_mixer(rms_norm(x, mix_norm[l]), w_in[l], rg_conv_w[l], rg_conv_b[l],
                             rg_w_r[l], rg_b_r[l], rg_w_i[l], rg_b_i[l], rg_lambda[l],
                             att_q_norm[l], att_k_norm[l], dn_conv_w[l], dn_a_log[l], dn_dt_bias[l],
                             dn_out_norm[l], w_branch[l], w_out[l])
        x = x + 0.5 * swiglu(rms_norm(x, ffn2_norm[l]), ffn2_w_gate[l], ffn2_w_up[l], ffn2_w_down[l])
    return x
```

```python
import functools
import math

import jax
import jax.numpy as jnp
from jax import lax
from jax.experimental import pallas as pl
from jax.experimental.pallas import tpu as pltpu

F32 = jnp.float32
BF16 = jnp.bfloat16

EPS = 1e-6
NEG_INF = -1e30
RG_C = 8.0
RG_BLOCK_DIM = 64
CONV_WIDTH = 4
ATT_GROUPS = ((128, 1), (512, 4), (2048, 16))
ATT_HEADS_PER_GROUP = 4
ATT_HEAD_DIM = 64
ATT_SPAN = 128
DN_HEAD_DIM = 128
DN_CHUNK = 64

LANES = 128
SUBLANES = 8
MXU_DIM = 256
VMEM_LIMIT_BYTES = 56 * 1024 * 1024


def _rms(x, g):
    return x * lax.rsqrt(jnp.mean(x * x, axis=-1, keepdims=True) + EPS) * g


def _dot(a, b):
    return jnp.dot(a, b, preferred_element_type=F32)


def _dot_nt(a, b):
    return lax.dot_general(a, b, (((1,), (1,)), ((), ())), preferred_element_type=F32)


def _dot_tn(a, b):
    return lax.dot_general(a, b, (((0,), (0,)), ((), ())), preferred_element_type=F32)


def _softplus(z):
    return jnp.maximum(z, 0.0) + jnp.log(1.0 + jnp.exp(-jnp.abs(z)))


def _const_spec(shape):
    nd = len(shape)
    return pl.BlockSpec(shape, lambda *_: (0,) * nd)


def _params(semantics):
    return pltpu.CompilerParams(dimension_semantics=semantics, vmem_limit_bytes=VMEM_LIMIT_BYTES)


def _ffn_kernel(x_ref, g_ref, wg_ref, wu_ref, wd_ref, o_ref, *, f_chunk):
    x = x_ref[...]
    u = _rms(x, g_ref[...]).astype(BF16)
    acc = jnp.zeros_like(x)
    for c in range(wg_ref.shape[1] // f_chunk):
        sl = slice(c * f_chunk, (c + 1) * f_chunk)
        hg = _dot(u, wg_ref[:, sl])
        hu = _dot(u, wu_ref[:, sl])
        h = (hg * jax.nn.sigmoid(hg) * hu).astype(BF16)
        acc = acc + _dot(h, wd_ref[sl, :])
    o_ref[...] = x + 0.5 * acc


def _ffn(x2, g, wg, wu, wd, *, tm=512, f_chunk=256):
    n, d = x2.shape
    f = wg.shape[1]
    return pl.pallas_call(
        functools.partial(_ffn_kernel, f_chunk=f_chunk),
        grid=(n // tm,),
        in_specs=[pl.BlockSpec((tm, d), lambda i: (i, 0)), _const_spec((1, d)),
                  _const_spec((d, f)), _const_spec((d, f)), _const_spec((f, d))],
        out_specs=pl.BlockSpec((tm, d), lambda i: (i, 0)),
        out_shape=jax.ShapeDtypeStruct((n, d), F32),
        compiler_params=_params(("arbitrary",)),
        name="ffn",
    )(x2, g, wg, wu, wd)


def _shift_rows(cur, prev8, s):
    rows = lax.broadcasted_iota(jnp.int32, cur.shape, 0)
    return jnp.where(rows < s, pltpu.roll(prev8, s, 0), pltpu.roll(cur, s, 0))


def _causal_conv(p, tail8, w):
    kw = w.shape[0]
    acc = p * w[kw - 1:kw, :]
    top = p[0:SUBLANES] * w[kw - 1:kw, :]
    for s in range(1, kw):
        wk = w[kw - 1 - s:kw - s, :]
        acc = acc + pltpu.roll(p, s, 0) * wk
        top = top + _shift_rows(p[0:SUBLANES], tail8, s) * wk
    return jnp.concatenate([top, acc[SUBLANES:]], axis=0)


def _rg_kernel(x_ref, g_ref, wx_ref, wgate_ref, cw_ref, cb_ref, wr_ref, br_ref, wi_ref, bi_ref, lamc_ref,
               o_ref, tail_ref, carry_ref, a_scr, h_scr):
    tt, width = o_ref.shape

    @pl.when(pl.program_id(1) == 0)
    def _():
        tail_ref[...] = jnp.zeros_like(tail_ref)
        carry_ref[...] = jnp.zeros_like(carry_ref)

    u = _rms(x_ref[...], g_ref[...]).astype(BF16)
    px = _dot(u, wx_ref[...])
    pg = _dot(u, wgate_ref[...])
    xa = _causal_conv(px, tail_ref[...], cw_ref[...]) + cb_ref[...]
    tail_ref[...] = px[tt - SUBLANES:tt]

    xab = xa.astype(BF16)
    nt = width // MXU_DIM
    r_lin = jnp.concatenate(
        [_dot(xab[:, j * MXU_DIM:(j + 1) * MXU_DIM], wr_ref[j]) for j in range(nt)], axis=1) + br_ref[...]
    i_lin = jnp.concatenate(
        [_dot(xab[:, j * MXU_DIM:(j + 1) * MXU_DIM], wi_ref[j]) for j in range(nt)], axis=1) + bi_ref[...]
    a = jnp.exp(lamc_ref[...] * jax.nn.sigmoid(r_lin))
    b = jnp.sqrt(1.0 - a * a) * (jax.nn.sigmoid(i_lin) * xa)

    rowmod = lax.broadcasted_iota(jnp.int32, a.shape, 0) & (SUBLANES - 1)
    for s in (1, 2, 4):
        m = rowmod >= s
        b = jnp.where(m, a * pltpu.roll(b, s, 0) + b, b)
        a = jnp.where(m, a * pltpu.roll(a, s, 0), a)
    a_scr[...] = a
    h_scr[...] = b
    carry = carry_ref[...]
    for g in range(tt // SUBLANES):
        rs = slice(g * SUBLANES, (g + 1) * SUBLANES)
        hg = a_scr[rs, :] * carry + h_scr[rs, :]
        h_scr[rs, :] = hg
        carry = jnp.broadcast_to(hg[SUBLANES - 1:SUBLANES, :], hg.shape)
    carry_ref[...] = carry
    o_ref[...] = (h_scr[...] * jax.nn.gelu(pg)).astype(BF16)


def _rg_branch(x, g, wx, wgate, cw, cb, wr_t, br, wi_t, bi, lamc, *, tt=256):
    bsz, t, d = x.shape
    width = wx.shape[1]
    return pl.pallas_call(
        _rg_kernel,
        grid=(bsz, t // tt),
        in_specs=[pl.BlockSpec((None, tt, d), lambda b, i: (b, i, 0)), _const_spec((1, d)),
                  _const_spec((d, width)), _const_spec((d, width)),
                  _const_spec(cw.shape), _const_spec((1, width)),
                  _const_spec(wr_t.shape), _const_spec((1, width)),
                  _const_spec(wi_t.shape), _const_spec((1, width)), _const_spec((1, width))],
        out_specs=pl.BlockSpec((None, tt, width), lambda b, i: (b, i, 0)),
        out_shape=jax.ShapeDtypeStruct((bsz, t, width), BF16),
        scratch_shapes=[pltpu.VMEM((SUBLANES, width), F32), pltpu.VMEM((SUBLANES, width), F32),
                        pltpu.VMEM((tt, width), F32), pltpu.VMEM((tt, width), F32)],
        compiler_params=_params(("arbitrary", "arbitrary")),
        name="rg_branch",
    )(x, g, wx, wgate, cw, cb, wr_t, br, wi_t, bi, lamc)


def _att_kernel(x_ref, g_ref, w_ref, gq_ref, gk_ref, e2_ref, mb_ref, o_ref, qkv_scr, o_scr, lse_scr, *, rt):
    t_total = o_ref.shape[0]
    n_slab = o_scr.shape[0]
    i = pl.program_id(1)
    row0 = pl.multiple_of(i * rt, rt)

    u = _rms(x_ref[...], g_ref[...]).astype(BF16)
    qkv = _dot(u, w_ref[...])
    e2 = e2_ref[...]
    width = n_slab * LANES

    def head_norm(z, gain):
        sq = z * z
        hi = sq.astype(BF16)
        lo = (sq - hi.astype(F32)).astype(BF16)
        ms = _dot(hi, e2) + _dot(lo, e2)
        return z * lax.rsqrt(ms + EPS) * gain

    for s in range(n_slab):
        cs = slice(s * LANES, (s + 1) * LANES)
        q = head_norm(qkv[:, cs], gq_ref[:, cs]) * (ATT_HEAD_DIM ** -0.5)
        k = head_norm(qkv[:, width + s * LANES: width + (s + 1) * LANES], gk_ref[:, cs])
        v = qkv[:, 2 * width + s * LANES: 2 * width + (s + 1) * LANES]
        qkv_scr[s, pl.ds(row0, rt), :] = q
        qkv_scr[n_slab + s, pl.ds(row0, rt), :] = k
        qkv_scr[2 * n_slab + s, pl.ds(row0, rt), :] = v

    @pl.when(i == pl.num_programs(1) - 1)
    def _():
        lane = lax.broadcasted_iota(jnp.int32, (ATT_SPAN, LANES), 1)
        first_head = lane < ATT_HEAD_DIM
        kcol = lax.broadcasted_iota(jnp.int32, (ATT_SPAN, 2 * ATT_SPAN), 1)

        for g, (window, dil) in enumerate(ATT_GROUPS):
            assert window // dil == ATT_SPAN
            nblk = t_total // dil // ATT_SPAN
            with_prev = nblk > 1

            def rows(start, dil=dil):
                return pl.ds(start, ATT_SPAN) if dil == 1 else pl.ds(start, ATT_SPAN, stride=dil)

            def block(idx, carry, g=g, dil=dil, with_prev=with_prev, rows=rows):
                r = idx % dil
                n = idx // dil
                start = r + n * (ATT_SPAN * dil)
                prev = jnp.maximum(start - ATT_SPAN * dil, r)
                for sl in range(2):
                    slab = 2 * g + sl
                    q2 = qkv_scr[slab, rows(start), :]
                    kc = qkv_scr[n_slab + slab, rows(start), :]
                    vc = qkv_scr[2 * n_slab + slab, rows(start), :]
                    if with_prev:
                        kk = jnp.concatenate([qkv_scr[n_slab + slab, rows(prev), :], kc], axis=0).astype(BF16)
                        vv = jnp.concatenate([qkv_scr[2 * n_slab + slab, rows(prev), :], vc], axis=0).astype(BF16)
                    else:
                        kk = kc.astype(BF16)
                        vv = vc.astype(BF16)
                    outs, lses = [], []
                    for hh in range(2):
                        head = 2 * slab + hh
                        keep = first_head if hh == 0 else jnp.logical_not(first_head)
                        qm = jnp.where(keep, q2, 0.0).astype(BF16)
                        s = _dot_nt(qm, kk)
                        if with_prev:
                            s = jnp.where((kcol >= ATT_SPAN) | (n > 0), s + mb_ref[head], NEG_INF)
                        else:
                            s = s + mb_ref[head, :, ATT_SPAN:]
                        m = jnp.max(s, axis=-1, keepdims=True)
                        p = jnp.exp(s - m)
                        den = jnp.sum(p, axis=-1, keepdims=True)
                        pv = _dot(p.astype(BF16), vv)
                        outs.append(pv * (1.0 / den))
                        lses.append(jnp.broadcast_to(m + jnp.log(den), (ATT_SPAN, LANES)))
                    o_scr[slab, rows(start), :] = jnp.where(first_head, outs[0], outs[1])
                    lse_scr[slab, rows(start), :] = jnp.where(first_head, lses[0], lses[1])
                return carry

            lax.fori_loop(0, dil * nblk, block, 0)

        n_grp = len(ATT_GROUPS)
        ct = 256

        def combine(j, carry):
            rws = pl.ds(pl.multiple_of(j * ct, ct), ct)
            for sl in range(2):
                ls = [lse_scr[2 * g + sl, rws, :] for g in range(n_grp)]
                mx = functools.reduce(jnp.maximum, ls)
                es = [jnp.exp(l - mx) for l in ls]
                inv = 1.0 / functools.reduce(lambda a, b: a + b, es)
                for g in range(n_grp):
                    slab = 2 * g + sl
                    o_ref[rws, slab * LANES:(slab + 1) * LANES] = (o_scr[slab, rws, :] * (es[g] * inv)).astype(BF16)
            return carry

        lax.fori_loop(0, t_total // ct, combine, 0)


def _att_branch(x, g, w_qkv, gq, gk, e2, maskbias, *, rt=256):
    bsz, t, d = x.shape
    width = gq.shape[1]
    n_slab = width // LANES
    return pl.pallas_call(
        functools.partial(_att_kernel, rt=rt),
        grid=(bsz, t // rt),
        in_specs=[pl.BlockSpec((None, rt, d), lambda b, i: (b, i, 0)), _const_spec((1, d)),
                  _const_spec(w_qkv.shape), _const_spec((1, width)), _const_spec((1, width)),
                  _const_spec(e2.shape), _const_spec(maskbias.shape)],
        out_specs=pl.BlockSpec((None, t, width), lambda b, i: (b, 0, 0)),
        out_shape=jax.ShapeDtypeStruct((bsz, t, width), BF16),
        scratch_shapes=[pltpu.VMEM((3 * n_slab, t, LANES), F32), pltpu.VMEM((n_slab, t, LANES), F32),
                        pltpu.VMEM((n_slab, t, LANES), F32)],
        compiler_params=_params(("arbitrary", "arbitrary")),
        name="att_branch",
    )(x, g, w_qkv, gq, gk, e2, maskbias)


def _att_maskbias(n_heads):
    slopes = jnp.exp2(-8.0 * jnp.arange(1, n_heads + 1, dtype=F32) / n_heads)
    qi = jnp.arange(ATT_SPAN)[:, None]
    kj = jnp.arange(2 * ATT_SPAN)[None, :]
    delta = qi + ATT_SPAN - kj
    valid = (delta >= 0) & (delta <= ATT_SPAN)
    dil = jnp.repeat(jnp.array([d for _, d in ATT_GROUPS], F32), ATT_HEADS_PER_GROUP)
    bias = -(slopes * dil)[:, None, None] * delta.astype(F32)[None]
    return jnp.where(valid[None], bias, NEG_INF)


def _cumsum_rows(x):
    n = x.shape[0]
    rows = lax.broadcasted_iota(jnp.int32, x.shape, 0)
    s = 1
    while s < n:
        x = x + jnp.where(rows >= s, pltpu.roll(x, s, 0), 0.0)
        s *= 2
    return x


def _dn_kernel(x_ref, g_ref, w_ref, cw_ref, aexp_ref, dtb_ref, on_ref, o_ref, tail_ref, s_scr):
    tt, width = o_ref.shape
    n_heads = width // DN_HEAD_DIM
    c = DN_CHUNK

    @pl.when(pl.program_id(1) == 0)
    def _():
        tail_ref[...] = jnp.zeros_like(tail_ref)
        s_scr[...] = jnp.zeros_like(s_scr)

    u = _rms(x_ref[...], g_ref[...]).astype(BF16)
    proj = _dot(u, w_ref[...])
    pqkv = proj[:, :3 * width]
    conv = _causal_conv(pqkv, tail_ref[...], cw_ref[...])
    tail_ref[...] = pqkv[tt - SUBLANES:tt]
    qkv = conv * jax.nn.sigmoid(conv)
    z = proj[:, 3 * width:4 * width]
    ba = proj[:, 4 * width:4 * width + LANES]
    beta_all = jax.nn.sigmoid(ba)
    glog_all = -aexp_ref[...] * _softplus(ba + dtb_ref[...])

    ri = lax.broadcasted_iota(jnp.int32, (c, c), 0)
    ci = lax.broadcasted_iota(jnp.int32, (c, c), 1)
    incl = ci <= ri
    strict = ci < ri
    eye = (ci == ri).astype(F32)

    for ch in range(tt // c):
        rs = slice(ch * c, (ch + 1) * c)
        gam_all = _cumsum_rows(glog_all[rs])
        gam_t = jnp.concatenate([gam_all, jnp.zeros_like(gam_all)], axis=0).T
        for h in range(n_heads):
            hs = slice(h * DN_HEAD_DIM, (h + 1) * DN_HEAD_DIM)
            q = qkv[rs, hs]
            k = qkv[rs, width + h * DN_HEAD_DIM: width + (h + 1) * DN_HEAD_DIM]
            v = qkv[rs, 2 * width + h * DN_HEAD_DIM: 2 * width + (h + 1) * DN_HEAD_DIM]
            q = q * lax.rsqrt(jnp.sum(q * q, axis=-1, keepdims=True) + EPS) * (DN_HEAD_DIM ** -0.5)
            k = k * lax.rsqrt(jnp.sum(k * k, axis=-1, keepdims=True) + EPS)
            beta = beta_all[rs, h:h + 1]
            gcol = gam_all[:, n_heads + h:n_heads + h + 1]
            grow = gam_t[n_heads + h:n_heads + h + 1, 0:c]
            glast = gcol[c - 1:c, :]
            decay = jnp.where(incl, jnp.exp(jnp.where(incl, gcol - grow, 0.0)), 0.0)
            eg = jnp.exp(gcol)
            kb = k * beta
            vb = v * beta
            kbf = k.astype(BF16)
            kq = _dot_nt(jnp.concatenate([kb, q], axis=0).astype(BF16), kbf)
            a = jnp.where(strict, kq[:c] * decay, 0.0)
            qk = jnp.where(incl, kq[c:] * decay, 0.0)
            tinv = eye - a
            pw = a
            for _ in range(int(math.log2(c)) - 1):
                pwb = pw.astype(BF16)
                pw = _dot(pwb, pwb)
                tinv = tinv + _dot(tinv.astype(BF16), pw.astype(BF16))
            sol = _dot(tinv.astype(BF16), jnp.concatenate([vb, kb * eg], axis=1).astype(BF16))
            u0 = sol[:, :DN_HEAD_DIM]
            wk = sol[:, DN_HEAD_DIM:]
            state = s_scr[h]
            ws = _dot(jnp.concatenate([wk, q * eg], axis=0).astype(BF16), state.astype(BF16))
            un = u0 - ws[:c]
            unb = un.astype(BF16)
            o = ws[c:] + _dot(qk.astype(BF16), unb)
            kd = (k * jnp.exp(glast - gcol)).astype(BF16)
            s_scr[h] = state * jnp.exp(glast) + _dot_tn(kd, unb)
            on = _rms(o, on_ref[:, hs])
            zz = z[rs, hs]
            o_ref[rs, hs] = (on * (zz * jax.nn.sigmoid(zz))).astype(BF16)


def _dn_branch(x, g, w_dn, cw, aexp, dtb, onorm, *, tt=256):
    bsz, t, d = x.shape
    width = onorm.shape[1]
    n_heads = width // DN_HEAD_DIM
    return pl.pallas_call(
        _dn_kernel,
        grid=(bsz, t // tt),
        in_specs=[pl.BlockSpec((None, tt, d), lambda b, i: (b, i, 0)), _const_spec((1, d)),
                  _const_spec(w_dn.shape), _const_spec(cw.shape),
                  _const_spec((1, LANES)), _const_spec((1, LANES)), _const_spec((1, width))],
        out_specs=pl.BlockSpec((None, tt, width), lambda b, i: (b, i, 0)),
        out_shape=jax.ShapeDtypeStruct((bsz, t, width), BF16),
        scratch_shapes=[pltpu.VMEM((SUBLANES, 3 * width), F32),
                        pltpu.VMEM((n_heads, DN_HEAD_DIM, DN_HEAD_DIM), F32)],
        compiler_params=_params(("arbitrary", "arbitrary")),
        name="dn_branch",
    )(x, g, w_dn, cw, aexp, dtb, onorm)


def _merge_kernel(x_ref, g_ref, ya_ref, yb_ref, yc_ref, wm_ref, wa_ref, wb_ref, wc_ref, wo_ref, o_ref):
    x = x_ref[...]
    d = x.shape[1]
    u = _rms(x, g_ref[...]).astype(BF16)
    y = jnp.zeros_like(x)
    for j, (yr, wr) in enumerate(((ya_ref, wa_ref), (yb_ref, wb_ref), (yc_ref, wc_ref))):
        gate = jax.nn.sigmoid(_dot(u, wm_ref[:, j * d:(j + 1) * d]))
        y = y + gate * _dot(yr[...], wr[...])
    o_ref[...] = x + _dot(y.astype(BF16), wo_ref[...])


def _merge(x2, g, ya, yb, yc, wm, wa, wb, wc, wo, *, tm=512):
    n, d = x2.shape
    row = lambda i: (i, 0)
    return pl.pallas_call(
        _merge_kernel,
        grid=(n // tm,),
        in_specs=[pl.BlockSpec((tm, d), row), _const_spec((1, d)),
                  pl.BlockSpec((tm, ya.shape[1]), row), pl.BlockSpec((tm, yb.shape[1]), row),
                  pl.BlockSpec((tm, yc.shape[1]), row),
                  _const_spec(wm.shape), _const_spec(wa.shape), _const_spec(wb.shape), _const_spec(wc.shape),
                  _const_spec(wo.shape)],
        out_specs=pl.BlockSpec((tm, d), row),
        out_shape=jax.ShapeDtypeStruct((n, d), F32),
        compiler_params=_params(("arbitrary",)),
        name="merge",
    )(x2, g, ya, yb, yc, wm, wa, wb, wc, wo)


def _block_diag_tiles(w):
    nb, k, _ = w.shape
    per = MXU_DIM // k
    w4 = w.reshape(nb // per, per, k, k)
    eye = jnp.eye(per, dtype=w.dtype)
    return jnp.einsum('tpij,pq->tpiqj', w4, eye).reshape(nb // per, MXU_DIM, MXU_DIM)


def _row(v):
    return v.reshape(1, -1).astype(F32)


def _mixer(x, mix_norm, w_in, rg_conv_w, rg_conv_b, rg_w_r, rg_b_r, rg_w_i, rg_b_i, rg_lambda,
           att_q_norm, att_k_norm, dn_conv_w, dn_a_log, dn_dt_bias, dn_out_norm, w_branch, w_out):
    bsz, t, d = x.shape
    rg_w = rg_conv_w.shape[1]
    att_w = att_q_norm.size
    dn_w = dn_out_norm.size
    dn_h = dn_out_norm.shape[0]
    off = 0

    def take(n):
        nonlocal off
        w = w_in[:, off:off + n]
        off += n
        return w

    w_rgx, w_rgg = take(rg_w).astype(BF16), take(rg_w).astype(BF16)
    w_att = take(3 * att_w).astype(BF16)
    w_dn_main = take(4 * dn_w)
    w_dn_ba = take(2 * dn_h)
    w_merge = take(3 * d).astype(BF16)
    w_dn = jnp.concatenate([w_dn_main, w_dn_ba, jnp.zeros((d, LANES - 2 * dn_h), F32)], axis=1).astype(BF16)
    g = _row(mix_norm)

    lamc = _row(-RG_C * jax.nn.softplus(-rg_lambda.astype(F32)))
    ya = _rg_branch(x, g, w_rgx, w_rgg, rg_conv_w, _row(rg_conv_b),
                    _block_diag_tiles(rg_w_r).astype(BF16), _row(rg_b_r),
                    _block_diag_tiles(rg_w_i).astype(BF16), _row(rg_b_i), lamc)

    n_att_heads = att_q_norm.shape[0]
    lane_head = jnp.arange(LANES) // ATT_HEAD_DIM
    e2 = ((lane_head[:, None] == lane_head[None, :]).astype(F32) / ATT_HEAD_DIM).astype(BF16)
    yb = _att_branch(x, g, w_att, _row(att_q_norm), _row(att_k_norm), e2, _att_maskbias(n_att_heads))

    pad = jnp.zeros((LANES - 2 * dn_h,), F32)
    aexp = _row(jnp.concatenate([jnp.zeros((dn_h,), F32), jnp.exp(dn_a_log.astype(F32)), pad]))
    dtb = _row(jnp.concatenate([jnp.zeros((dn_h,), F32), dn_dt_bias.astype(F32), pad]))
    yc = _dn_branch(x, g, w_dn, dn_conv_w, aexp, dtb, _row(dn_out_norm))

    wa = w_branch[:rg_w].astype(BF16)
    wb = w_branch[rg_w:rg_w + att_w].astype(BF16)
    wc = w_branch[rg_w + att_w:].astype(BF16)
    n = bsz * t
    out = _merge(x.reshape(n, d), g, ya.reshape(n, rg_w), yb.reshape(n, att_w), yc.reshape(n, dn_w),
                 w_merge, wa, wb, wc, w_out.astype(BF16))
    return out.reshape(bsz, t, d)


def kernel(x, ffn1_norm, ffn1_w_gate, ffn1_w_up, ffn1_w_down, mix_norm, w_in, rg_conv_w, rg_conv_b, rg_w_r, rg_b_r, rg_w_i, rg_b_i, rg_lambda, att_q_norm, att_k_norm, dn_conv_w, dn_a_log, dn_dt_bias, dn_out_norm, w_branch, w_out, ffn2_norm, ffn2_w_gate, ffn2_w_up, ffn2_w_down):
    bsz, t, d = x.shape
    n = bsz * t
    for l in range(ffn1_norm.shape[0]):
        x = _ffn(x.reshape(n, d), _row(ffn1_norm[l]), ffn1_w_gate[l].astype(BF16), ffn1_w_up[l].astype(BF16),
                 ffn1_w_down[l].astype(BF16)).reshape(bsz, t, d)
        x = _mixer(x, mix_norm[l], w_in[l], rg_conv_w[l], rg_conv_b[l], rg_w_r[l], rg_b_r[l], rg_w_i[l],
                   rg_b_i[l], rg_lambda[l], att_q_norm[l], att_k_norm[l], dn_conv_w[l], dn_a_log[l],
                   dn_dt_bias[l], dn_out_norm[l], w_branch[l], w_out[l])
        x = _ffn(x.reshape(n, d), _row(ffn2_norm[l]), ffn2_w_gate[l].astype(BF16), ffn2_w_up[l].astype(BF16),
                 ffn2_w_down[l].astype(BF16)).reshape(bsz, t, d)
    return x
```

```python
import functools
import math

import jax
import jax.numpy as jnp
from jax import lax
from jax.experimental import pallas as pl
from jax.experimental.pallas import tpu as pltpu

F32 = jnp.float32
BF16 = jnp.bfloat16

EPS = 1e-6
NEG_INF = -1e30
RG_C = 8.0
RG_BLOCK_DIM = 64
CONV_WIDTH = 4
ATT_GROUPS = ((128, 1), (512, 4), (2048, 16))
ATT_HEADS_PER_GROUP = 4
ATT_HEAD_DIM = 64
ATT_SPAN = 128
DN_HEAD_DIM = 128
DN_CHUNK = 64

LANES = 128
SUBLANES = 8
MXU_DIM = 256
VMEM_LIMIT_BYTES = 56 * 1024 * 1024


def _rms(x, g):
    return x * lax.rsqrt(jnp.mean(x * x, axis=-1, keepdims=True) + EPS) * g


def _dot(a, b):
    return jnp.dot(a, b, preferred_element_type=F32)


def _dot_nt(a, b):
    return lax.dot_general(a, b, (((1,), (1,)), ((), ())), preferred_element_type=F32)


def _dot_tn(a, b):
    return lax.dot_general(a, b, (((0,), (0,)), ((), ())), preferred_element_type=F32)


def _softplus(z):
    return jnp.maximum(z, 0.0) + jnp.log(1.0 + jnp.exp(-jnp.abs(z)))


def _const_spec(shape):
    nd = len(shape)
    return pl.BlockSpec(shape, lambda *_: (0,) * nd)


def _params(semantics):
    return pltpu.CompilerParams(dimension_semantics=semantics, vmem_limit_bytes=VMEM_LIMIT_BYTES)


def _ffn_kernel(x_ref, g_ref, wg_ref, wu_ref, wd_ref, o_ref, *, f_chunk):
    x = x_ref[...]
    u = _rms(x, g_ref[...]).astype(BF16)
    acc = jnp.zeros_like(x)
    for c in range(wg_ref.shape[1] // f_chunk):
        sl = slice(c * f_chunk, (c + 1) * f_chunk)
        hg = _dot(u, wg_ref[:, sl])
        hu = _dot(u, wu_ref[:, sl])
        h = (hg * jax.nn.sigmoid(hg) * hu).astype(BF16)
        acc = acc + _dot(h, wd_ref[sl, :])
    o_ref[...] = x + 0.5 * acc


def _ffn(x2, g, wg, wu, wd, *, tm=512, f_chunk=256):
    n, d = x2.shape
    f = wg.shape[1]
    return pl.pallas_call(
        functools.partial(_ffn_kernel, f_chunk=f_chunk),
        grid=(n // tm,),
        in_specs=[pl.BlockSpec((tm, d), lambda i: (i, 0)), _const_spec((1, d)),
                  _const_spec((d, f)), _const_spec((d, f)), _const_spec((f, d))],
        out_specs=pl.BlockSpec((tm, d), lambda i: (i, 0)),
        out_shape=jax.ShapeDtypeStruct((n, d), F32),
        compiler_params=_params(("arbitrary",)),
        name="ffn",
    )(x2, g, wg, wu, wd)


def _shift_rows(cur, prev8, s):
    rows = lax.broadcasted_iota(jnp.int32, cur.shape, 0)
    return jnp.where(rows < s, pltpu.roll(prev8, s, 0), pltpu.roll(cur, s, 0))


def _causal_conv(p, tail8, w):
    kw = w.shape[0]
    acc = p * w[kw - 1:kw, :]
    top = p[0:SUBLANES] * w[kw - 1:kw, :]
    for s in range(1, kw):
        wk = w[kw - 1 - s:kw - s, :]
        acc = acc + pltpu.roll(p, s, 0) * wk
        top = top + _shift_rows(p[0:SUBLANES], tail8, s) * wk
    return jnp.concatenate([top, acc[SUBLANES:]], axis=0)


def _rg_kernel(x_ref, g_ref, wx_ref, wgate_ref, cw_ref, cb_ref, wr_ref, br_ref, wi_ref, bi_ref, lamc_ref,
               o_ref, tail_ref, carry_ref, a_scr, h_scr):
    tt, width = o_ref.shape

    @pl.when(pl.program_id(1) == 0)
    def _():
        tail_ref[...] = jnp.zeros_like(tail_ref)
        carry_ref[...] = jnp.zeros_like(carry_ref)

    u = _rms(x_ref[...], g_ref[...]).astype(BF16)
    px = _dot(u, wx_ref[...])
    pg = _dot(u, wgate_ref[...])
    xa = _causal_conv(px, tail_ref[...], cw_ref[...]) + cb_ref[...]
    tail_ref[...] = px[tt - SUBLANES:tt]

    xab = xa.astype(BF16)
    nt = width // MXU_DIM
    r_lin = jnp.concatenate(
        [_dot(xab[:, j * MXU_DIM:(j + 1) * MXU_DIM], wr_ref[j]) for j in range(nt)], axis=1) + br_ref[...]
    i_lin = jnp.concatenate(
        [_dot(xab[:, j * MXU_DIM:(j + 1) * MXU_DIM], wi_ref[j]) for j in range(nt)], axis=1) + bi_ref[...]
    a = jnp.exp(lamc_ref[...] * jax.nn.sigmoid(r_lin))
    b = jnp.sqrt(1.0 - a * a) * (jax.nn.sigmoid(i_lin) * xa)

    rowmod = lax.broadcasted_iota(jnp.int32, a.shape, 0) & (SUBLANES - 1)
    for s in (1, 2, 4):
        m = rowmod >= s
        b = jnp.where(m, a * pltpu.roll(b, s, 0) + b, b)
        a = jnp.where(m, a * pltpu.roll(a, s, 0), a)
    a_scr[...] = a
    h_scr[...] = b
    carry = carry_ref[...]
    for g in range(tt // SUBLANES):
        rs = slice(g * SUBLANES, (g + 1) * SUBLANES)
        hg = a_scr[rs, :] * carry + h_scr[rs, :]
        h_scr[rs, :] = hg
        carry = jnp.broadcast_to(hg[SUBLANES - 1:SUBLANES, :], hg.shape)
    carry_ref[...] = carry
    o_ref[...] = (h_scr[...] * jax.nn.gelu(pg)).astype(BF16)


def _rg_branch(x, g, wx, wgate, cw, cb, wr_t, br, wi_t, bi, lamc, *, tt=256):
    bsz, t, d = x.shape
    width = wx.shape[1]
    return pl.pallas_call(
        _rg_kernel,
        grid=(bsz, t // tt),
        in_specs=[pl.BlockSpec((None, tt, d), lambda b, i: (b, i, 0)), _const_spec((1, d)),
                  _const_spec((d, width)), _const_spec((d, width)),
                  _const_spec(cw.shape), _const_spec((1, width)),
                  _const_spec(wr_t.shape), _const_spec((1, width)),
                  _const_spec(wi_t.shape), _const_spec((1, width)), _const_spec((1, width))],
        out_specs=pl.BlockSpec((None, tt, width), lambda b, i: (b, i, 0)),
        out_shape=jax.ShapeDtypeStruct((bsz, t, width), BF16),
        scratch_shapes=[pltpu.VMEM((SUBLANES, width), F32), pltpu.VMEM((SUBLANES, width), F32),
                        pltpu.VMEM((tt, width), F32), pltpu.VMEM((tt, width), F32)],
        compiler_params=_params(("arbitrary", "arbitrary")),
        name="rg_branch",
    )(x, g, wx, wgate, cw, cb, wr_t, br, wi_t, bi, lamc)


ATT_BLOCK_BATCH = 4


def _att_kernel(x_ref, g_ref, w_ref, gq_ref, gk_ref, e4_ref, mb_ref, o_ref, qkv_scr, o_scr, lse_scr, *, rt):
    t_total = o_ref.shape[0]
    n_slab = o_scr.shape[0]
    i = pl.program_id(1)
    row0 = pl.multiple_of(i * rt, rt)
    span = ATT_SPAN

    u = _rms(x_ref[...], g_ref[...]).astype(BF16)
    qkv = _dot(u, w_ref[...])
    width = n_slab * LANES
    e4 = e4_ref[...]
    for s in range(n_slab):
        cs = slice(s * LANES, (s + 1) * LANES)
        qk = jnp.concatenate([qkv[:, cs], qkv[:, width + s * LANES: width + (s + 1) * LANES]], axis=1)
        gain = jnp.concatenate([gq_ref[:, cs] * (ATT_HEAD_DIM ** -0.5), gk_ref[:, cs]], axis=1)
        ms = _dot((qk * qk).astype(BF16), e4)
        qk = qk * lax.rsqrt(ms + EPS) * gain
        qkv_scr[s, pl.ds(row0, rt), :] = qk[:, :LANES]
        qkv_scr[n_slab + s, pl.ds(row0, rt), :] = qk[:, LANES:]
        qkv_scr[2 * n_slab + s, pl.ds(row0, rt), :] = qkv[:, 2 * width + s * LANES: 2 * width + (s + 1) * LANES]

    @pl.when(i == pl.num_programs(1) - 1)
    def _():
        lane = lax.broadcasted_iota(jnp.int32, (span, LANES), 1)
        first_head = lane < ATT_HEAD_DIM

        for g, (window, dil) in enumerate(ATT_GROUPS):
            assert window // dil == span
            nblk = t_total // dil // span
            blocks = [(r, n) for n in range(nblk) for r in range(dil)]

            def rows(start, dil=dil):
                return pl.ds(start, span) if dil == 1 else pl.ds(start, span, stride=dil)

            for b0 in range(0, len(blocks), ATT_BLOCK_BATCH):
                probs = [(r + n * span * dil, n > 0, 2 * g + sl)
                         for r, n in blocks[b0:b0 + ATT_BLOCK_BATCH] for sl in range(2)]
                qs, kk, vv = [], [], []
                for start, has_prev, slab in probs:
                    q2 = qkv_scr[slab, rows(start), :]
                    qs.append(jnp.concatenate([jnp.where(first_head, q2, 0.0), jnp.where(first_head, 0.0, q2)],
                                              axis=0).astype(BF16))
                    kc = qkv_scr[n_slab + slab, rows(start), :]
                    vc = qkv_scr[2 * n_slab + slab, rows(start), :]
                    if has_prev:
                        prev = start - span * dil
                        kc = jnp.concatenate([qkv_scr[n_slab + slab, rows(prev), :], kc], axis=0)
                        vc = jnp.concatenate([qkv_scr[2 * n_slab + slab, rows(prev), :], vc], axis=0)
                    kk.append(kc.astype(BF16))
                    vv.append(vc.astype(BF16))
                sc = [_dot_nt(q_, k_) for q_, k_ in zip(qs, kk)]
                sc = [s_ + (mb_ref[slab] if has_prev else mb_ref[slab, :, span:])
                      for s_, (_, has_prev, slab) in zip(sc, probs)]
                mx = [jnp.max(s_, axis=-1, keepdims=True) for s_ in sc]
                pr = [jnp.exp(s_ - m_) for s_, m_ in zip(sc, mx)]
                den = [jnp.sum(p_, axis=-1, keepdims=True) for p_ in pr]
                pv = [_dot(p_.astype(BF16), v_) for p_, v_ in zip(pr, vv)]
                for (start, _, slab), pv_, m_, d_ in zip(probs, pv, mx, den):
                    on = pv_ * (1.0 / d_)
                    lse = jnp.broadcast_to(m_ + jnp.log(d_), on.shape)
                    o_scr[slab, rows(start), :] = jnp.where(first_head, on[:span], on[span:])
                    lse_scr[slab, rows(start), :] = jnp.where(first_head, lse[:span], lse[span:])

        n_grp = len(ATT_GROUPS)
        ct = 256

        def combine(j, carry):
            rws = pl.ds(pl.multiple_of(j * ct, ct), ct)
            for sl in range(2):
                ls = [lse_scr[2 * g + sl, rws, :] for g in range(n_grp)]
                mx = functools.reduce(jnp.maximum, ls)
                es = [jnp.exp(l - mx) for l in ls]
                inv = 1.0 / functools.reduce(lambda a, b: a + b, es)
                for g in range(n_grp):
                    slab = 2 * g + sl
                    o_ref[rws, slab * LANES:(slab + 1) * LANES] = (o_scr[slab, rws, :] * (es[g] * inv)).astype(BF16)
            return carry

        lax.fori_loop(0, t_total // ct, combine, 0)


def _att_branch(x, g, w_qkv, gq, gk, e4, maskbias, *, rt=256):
    bsz, t, d = x.shape
    width = gq.shape[1]
    n_slab = width // LANES
    return pl.pallas_call(
        functools.partial(_att_kernel, rt=rt),
        grid=(bsz, t // rt),
        in_specs=[pl.BlockSpec((None, rt, d), lambda b, i: (b, i, 0)), _const_spec((1, d)),
                  _const_spec(w_qkv.shape), _const_spec((1, width)), _const_spec((1, width)),
                  _const_spec(e4.shape), _const_spec(maskbias.shape)],
        out_specs=pl.BlockSpec((None, t, width), lambda b, i: (b, 0, 0)),
        out_shape=jax.ShapeDtypeStruct((bsz, t, width), BF16),
        scratch_shapes=[pltpu.VMEM((3 * n_slab, t, LANES), F32), pltpu.VMEM((n_slab, t, LANES), F32),
                        pltpu.VMEM((n_slab, t, LANES), F32)],
        compiler_params=_params(("arbitrary", "arbitrary")),
        name="att_branch",
    )(x, g, w_qkv, gq, gk, e4, maskbias)


def _att_maskbias(n_heads):
    slopes = jnp.exp2(-8.0 * jnp.arange(1, n_heads + 1, dtype=F32) / n_heads)
    qi = jnp.arange(ATT_SPAN)[:, None]
    kj = jnp.arange(2 * ATT_SPAN)[None, :]
    delta = qi + ATT_SPAN - kj
    valid = (delta >= 0) & (delta <= ATT_SPAN)
    dil = jnp.repeat(jnp.array([d for _, d in ATT_GROUPS], F32), ATT_HEADS_PER_GROUP)
    bias = -(slopes * dil)[:, None, None] * delta.astype(F32)[None]
    return jnp.where(valid[None], bias, NEG_INF).reshape(n_heads // 2, 2 * ATT_SPAN, 2 * ATT_SPAN)


def _head_mean_matrix(n_lanes, head_dim):
    lane_head = jnp.arange(n_lanes) // head_dim
    return ((lane_head[:, None] == lane_head[None, :]).astype(F32) / head_dim).astype(BF16)


def _cumsum_rows(x):
    n = x.shape[0]
    rows = lax.broadcasted_iota(jnp.int32, x.shape, 0)
    s = 1
    while s < n:
        x = x + jnp.where(rows >= s, pltpu.roll(x, s, 0), 0.0)
        s *= 2
    return x


def _dn_kernel(x_ref, g_ref, w_ref, cw_ref, aexp_ref, dtb_ref, on_ref, o_ref, tail_ref, s_scr):
    tt, width = o_ref.shape
    n_heads = width // DN_HEAD_DIM
    c = DN_CHUNK

    @pl.when(pl.program_id(1) == 0)
    def _():
        tail_ref[...] = jnp.zeros_like(tail_ref)
        s_scr[...] = jnp.zeros_like(s_scr)

    u = _rms(x_ref[...], g_ref[...]).astype(BF16)
    proj = _dot(u, w_ref[...])
    pqkv = proj[:, :3 * width]
    conv = _causal_conv(pqkv, tail_ref[...], cw_ref[...])
    tail_ref[...] = pqkv[tt - SUBLANES:tt]
    qkv = conv * jax.nn.sigmoid(conv)
    z = proj[:, 3 * width:4 * width]
    ba = proj[:, 4 * width:4 * width + LANES]
    beta_all = jax.nn.sigmoid(ba)
    glog_all = -aexp_ref[...] * _softplus(ba + dtb_ref[...])

    ri = lax.broadcasted_iota(jnp.int32, (c, c), 0)
    ci = lax.broadcasted_iota(jnp.int32, (c, c), 1)
    incl = ci <= ri
    strict = ci < ri
    eye = (ci == ri).astype(F32)

    n_ch = tt // c
    probs = [(ch, h) for ch in range(n_ch) for h in range(n_heads)]
    gam_all = [_cumsum_rows(glog_all[ch * c:(ch + 1) * c]) for ch in range(n_ch)]
    gam_t = [jnp.concatenate([g_, jnp.zeros_like(g_)], axis=0).T for g_ in gam_all]

    def cols(base, h):
        return slice(base * width + h * DN_HEAD_DIM, base * width + (h + 1) * DN_HEAD_DIM)

    def l2n(m):
        return m * lax.rsqrt(jnp.sum(m * m, axis=-1, keepdims=True) + EPS)

    q = [l2n(qkv[ch * c:(ch + 1) * c, cols(0, h)]) * (DN_HEAD_DIM ** -0.5) for ch, h in probs]
    k = [l2n(qkv[ch * c:(ch + 1) * c, cols(1, h)]) for ch, h in probs]
    gcol = [gam_all[ch][:, n_heads + h:n_heads + h + 1] for ch, h in probs]
    grow = [gam_t[ch][n_heads + h:n_heads + h + 1, 0:c] for ch, h in probs]
    decay = [jnp.where(incl, jnp.exp(jnp.where(incl, gc - gr, 0.0)), 0.0) for gc, gr in zip(gcol, grow)]
    eg = [jnp.exp(gc) for gc in gcol]
    kb = [k_ * beta_all[ch * c:(ch + 1) * c, h:h + 1] for k_, (ch, h) in zip(k, probs)]
    rhs = [jnp.concatenate([qkv[ch * c:(ch + 1) * c, cols(2, h)] * beta_all[ch * c:(ch + 1) * c, h:h + 1],
                            kb_ * eg_], axis=1).astype(BF16) for (ch, h), kb_, eg_ in zip(probs, kb, eg)]
    kq = [_dot_nt(jnp.concatenate([kb_, q_], axis=0).astype(BF16), k_.astype(BF16))
          for kb_, q_, k_ in zip(kb, q, k)]
    a = [jnp.where(strict, kq_[:c] * d_, 0.0) for kq_, d_ in zip(kq, decay)]
    qk = [jnp.where(incl, kq_[c:] * d_, 0.0).astype(BF16) for kq_, d_ in zip(kq, decay)]
    tinv = [eye - a_ for a_ in a]
    ab = [a_.astype(BF16) for a_ in a]
    pw = [_dot(a_, a_) for a_ in ab]
    for _ in range(int(math.log2(c)) - 2):
        r = [_dot(jnp.concatenate([t_, p_], axis=0).astype(BF16), p_.astype(BF16)) for t_, p_ in zip(tinv, pw)]
        tinv = [t_ + r_[:c] for t_, r_ in zip(tinv, r)]
        pw = [r_[c:] for r_ in r]
    tinv = [t_ + _dot(t_.astype(BF16), p_.astype(BF16)) for t_, p_ in zip(tinv, pw)]
    sol = [_dot(t_.astype(BF16), r_) for t_, r_ in zip(tinv, rhs)]
    wq = [jnp.concatenate([s_[:, DN_HEAD_DIM:], q_ * eg_], axis=0).astype(BF16) for s_, q_, eg_ in zip(sol, q, eg)]
    glast = [gc[c - 1:c, :] for gc in gcol]
    qkd = [jnp.concatenate([qk_, (k_ * jnp.exp(gl - gc)).T.astype(BF16)], axis=0)
           for qk_, k_, gl, gc in zip(qk, k, glast, gcol)]

    for ch in range(n_ch):
        rs = slice(ch * c, (ch + 1) * c)
        ix = [ch * n_heads + h for h in range(n_heads)]
        state = [s_scr[h] for h in range(n_heads)]
        ws = [_dot(wq[i_], st.astype(BF16)) for i_, st in zip(ix, state)]
        un = [(sol[i_][:, :DN_HEAD_DIM] - ws_[:c]).astype(BF16) for i_, ws_ in zip(ix, ws)]
        ou = [_dot(qkd[i_], un_) for i_, un_ in zip(ix, un)]
        for h in range(n_heads):
            hs = slice(h * DN_HEAD_DIM, (h + 1) * DN_HEAD_DIM)
            s_scr[h] = state[h] * jnp.exp(glast[ix[h]]) + ou[h][c:]
            on = _rms(ws[h][c:] + ou[h][:c], on_ref[:, hs])
            zz = z[rs, hs]
            o_ref[rs, hs] = (on * (zz * jax.nn.sigmoid(zz))).astype(BF16)


def _dn_branch(x, g, w_dn, cw, aexp, dtb, onorm, *, tt=256):
    bsz, t, d = x.shape
    width = onorm.shape[1]
    n_heads = width // DN_HEAD_DIM
    return pl.pallas_call(
        _dn_kernel,
        grid=(bsz, t // tt),
        in_specs=[pl.BlockSpec((None, tt, d), lambda b, i: (b, i, 0)), _const_spec((1, d)),
                  _const_spec(w_dn.shape), _const_spec(cw.shape),
                  _const_spec((1, LANES)), _const_spec((1, LANES)), _const_spec((1, width))],
        out_specs=pl.BlockSpec((None, tt, width), lambda b, i: (b, i, 0)),
        out_shape=jax.ShapeDtypeStruct((bsz, t, width), BF16),
        scratch_shapes=[pltpu.VMEM((SUBLANES, 3 * width), F32),
                        pltpu.VMEM((n_heads, DN_HEAD_DIM, DN_HEAD_DIM), F32)],
        compiler_params=_params(("arbitrary", "arbitrary")),
        name="dn_branch",
    )(x, g, w_dn, cw, aexp, dtb, onorm)


def _merge_kernel(x_ref, g_ref, ya_ref, yb_ref, yc_ref, wm_ref, wa_ref, wb_ref, wc_ref, wo_ref, o_ref):
    x = x_ref[...]
    d = x.shape[1]
    u = _rms(x, g_ref[...]).astype(BF16)
    y = jnp.zeros_like(x)
    for j, (yr, wr) in enumerate(((ya_ref, wa_ref), (yb_ref, wb_ref), (yc_ref, wc_ref))):
        gate = jax.nn.sigmoid(_dot(u, wm_ref[:, j * d:(j + 1) * d]))
        y = y + gate * _dot(yr[...], wr[...])
    o_ref[...] = x + _dot(y.astype(BF16), wo_ref[...])


def _merge(x2, g, ya, yb, yc, wm, wa, wb, wc, wo, *, tm=512):
    n, d = x2.shape
    row = lambda i: (i, 0)
    return pl.pallas_call(
        _merge_kernel,
        grid=(n // tm,),
        in_specs=[pl.BlockSpec((tm, d), row), _const_spec((1, d)),
                  pl.BlockSpec((tm, ya.shape[1]), row), pl.BlockSpec((tm, yb.shape[1]), row),
                  pl.BlockSpec((tm, yc.shape[1]), row),
                  _const_spec(wm.shape), _const_spec(wa.shape), _const_spec(wb.shape), _const_spec(wc.shape),
                  _const_spec(wo.shape)],
        out_specs=pl.BlockSpec((tm, d), row),
        out_shape=jax.ShapeDtypeStruct((n, d), F32),
        compiler_params=_params(("arbitrary",)),
        name="merge",
    )(x2, g, ya, yb, yc, wm, wa, wb, wc, wo)


def _block_diag_tiles(w):
    nb, k, _ = w.shape
    per = MXU_DIM // k
    w4 = w.reshape(nb // per, per, k, k)
    eye = jnp.eye(per, dtype=w.dtype)
    return jnp.einsum('tpij,pq->tpiqj', w4, eye).reshape(nb // per, MXU_DIM, MXU_DIM)


def _row(v):
    return v.reshape(1, -1).astype(F32)


def _mixer(x, mix_norm, w_in, rg_conv_w, rg_conv_b, rg_w_r, rg_b_r, rg_w_i, rg_b_i, rg_lambda,
           att_q_norm, att_k_norm, dn_conv_w, dn_a_log, dn_dt_bias, dn_out_norm, w_branch, w_out):
    bsz, t, d = x.shape
    rg_w = rg_conv_w.shape[1]
    att_w = att_q_norm.size
    dn_w = dn_out_norm.size
    dn_h = dn_out_norm.shape[0]
    off = 0

    def take(n):
        nonlocal off
        w = w_in[:, off:off + n]
        off += n
        return w

    w_rgx, w_rgg = take(rg_w).astype(BF16), take(rg_w).astype(BF16)
    w_att = take(3 * att_w).astype(BF16)
    w_dn_main = take(4 * dn_w)
    w_dn_ba = take(2 * dn_h)
    w_merge = take(3 * d).astype(BF16)
    w_dn = jnp.concatenate([w_dn_main, w_dn_ba, jnp.zeros((d, LANES - 2 * dn_h), F32)], axis=1).astype(BF16)
    g = _row(mix_norm)

    lamc = _row(-RG_C * jax.nn.softplus(-rg_lambda.astype(F32)))
    ya = _rg_branch(x, g, w_rgx, w_rgg, rg_conv_w, _row(rg_conv_b),
                    _block_diag_tiles(rg_w_r).astype(BF16), _row(rg_b_r),
                    _block_diag_tiles(rg_w_i).astype(BF16), _row(rg_b_i), lamc)

    n_att_heads = att_q_norm.shape[0]
    yb = _att_branch(x, g, w_att, _row(att_q_norm), _row(att_k_norm),
                     _head_mean_matrix(2 * LANES, ATT_HEAD_DIM), _att_maskbias(n_att_heads))

    pad = jnp.zeros((LANES - 2 * dn_h,), F32)
    aexp = _row(jnp.concatenate([jnp.zeros((dn_h,), F32), jnp.exp(dn_a_log.astype(F32)), pad]))
    dtb = _row(jnp.concatenate([jnp.zeros((dn_h,), F32), dn_dt_bias.astype(F32), pad]))
    yc = _dn_branch(x, g, w_dn, dn_conv_w, aexp, dtb, _row(dn_out_norm))

    wa = w_branch[:rg_w].astype(BF16)
    wb = w_branch[rg_w:rg_w + att_w].astype(BF16)
    wc = w_branch[rg_w + att_w:].astype(BF16)
    n = bsz * t
    out = _merge(x.reshape(n, d), g, ya.reshape(n, rg_w), yb.reshape(n, att_w), yc.reshape(n, dn_w),
                 w_merge, wa, wb, wc, w_out.astype(BF16))
    return out.reshape(bsz, t, d)


def kernel(x, ffn1_norm, ffn1_w_gate, ffn1_w_up, ffn1_w_down, mix_norm, w_in, rg_conv_w, rg_conv_b, rg_w_r, rg_b_r, rg_w_i, rg_b_i, rg_lambda, att_q_norm, att_k_norm, dn_conv_w, dn_a_log, dn_dt_bias, dn_out_norm, w_branch, w_out, ffn2_norm, ffn2_w_gate, ffn2_w_up, ffn2_w_down):
    bsz, t, d = x.shape
    n = bsz * t
    for l in range(ffn1_norm.shape[0]):
        x = _ffn(x.reshape(n, d), _row(ffn1_norm[l]), ffn1_w_gate[l].astype(BF16), ffn1_w_up[l].astype(BF16),
                 ffn1_w_down[l].astype(BF16)).reshape(bsz, t, d)
        x = _mixer(x, mix_norm[l], w_in[l], rg_conv_w[l], rg_conv_b[l], rg_w_r[l], rg_b_r[l], rg_w_i[l],
                   rg_b_i[l], rg_lambda[l], att_q_norm[l], att_k_norm[l], dn_conv_w[l], dn_a_log[l],
                   dn_dt_bias[l], dn_out_norm[l], w_branch[l], w_out[l])
        x = _ffn(x.reshape(n, d), _row(ffn2_norm[l]), ffn2_w_gate[l].astype(BF16), ffn2_w_up[l].astype(BF16),
                 ffn2_w_down[l].astype(BF16)).reshape(bsz, t, d)
    return x
```

```python
import functools
import math

import jax
import jax.numpy as jnp
from jax import lax
from jax.experimental import pallas as pl
from jax.experimental.pallas import tpu as pltpu

F32 = jnp.float32
BF16 = jnp.bfloat16

EPS = 1e-6
NEG_INF = -1e30
RG_C = 8.0
RG_BLOCK_DIM = 64
CONV_WIDTH = 4
ATT_GROUPS = ((128, 1), (512, 4), (2048, 16))
ATT_HEADS_PER_GROUP = 4
ATT_HEAD_DIM = 64
ATT_SPAN = 128
DN_HEAD_DIM = 128
DN_CHUNK = 64

LANES = 128
SUBLANES = 8
MXU_DIM = 256
VMEM_LIMIT_BYTES = 56 * 1024 * 1024


def _rms(x, g):
    return x * lax.rsqrt(jnp.mean(x * x, axis=-1, keepdims=True) + EPS) * g


def _dot(a, b):
    return jnp.dot(a, b, preferred_element_type=F32)


def _dot_nt(a, b):
    return lax.dot_general(a, b, (((1,), (1,)), ((), ())), preferred_element_type=F32)


def _dot_tn(a, b):
    return lax.dot_general(a, b, (((0,), (0,)), ((), ())), preferred_element_type=F32)


def _softplus(z):
    return jnp.maximum(z, 0.0) + jnp.log(1.0 + jnp.exp(-jnp.abs(z)))


def _const_spec(shape):
    nd = len(shape)
    return pl.BlockSpec(shape, lambda *_: (0,) * nd)


def _layer_spec(rows, cols, layer, col_block=0):
    return pl.BlockSpec((None, rows, cols), lambda *_: (layer, 0, col_block))


def _params(semantics):
    return pltpu.CompilerParams(dimension_semantics=semantics, vmem_limit_bytes=VMEM_LIMIT_BYTES)


def _ffn_kernel(x_ref, g_ref, wg_ref, wu_ref, wd_ref, o_ref, *, f_chunk):
    x = x_ref[...]
    u = _rms(x, g_ref[...]).astype(BF16)
    acc = jnp.zeros_like(x)
    for c in range(wg_ref.shape[1] // f_chunk):
        sl = slice(c * f_chunk, (c + 1) * f_chunk)
        hg = _dot(u, wg_ref[:, sl])
        hu = _dot(u, wu_ref[:, sl])
        h = (hg * jax.nn.sigmoid(hg) * hu).astype(BF16)
        acc = acc + _dot(h, wd_ref[sl, :])
    o_ref[...] = x + 0.5 * acc


def _ffn(x2, g, wg, wu, wd, layer, *, tm=512, f_chunk=256):
    n, d = x2.shape
    f = wg.shape[2]
    return pl.pallas_call(
        functools.partial(_ffn_kernel, f_chunk=f_chunk),
        grid=(n // tm,),
        in_specs=[pl.BlockSpec((tm, d), lambda i: (i, 0)), _const_spec((1, d)),
                  _layer_spec(d, f, layer), _layer_spec(d, f, layer), _layer_spec(f, d, layer)],
        out_specs=pl.BlockSpec((tm, d), lambda i: (i, 0)),
        out_shape=jax.ShapeDtypeStruct((n, d), F32),
        compiler_params=_params(("arbitrary",)),
        name="ffn",
    )(x2, g, wg, wu, wd)


def _shift_rows(cur, prev8, s):
    rows = lax.broadcasted_iota(jnp.int32, cur.shape, 0)
    return jnp.where(rows < s, pltpu.roll(prev8, s, 0), pltpu.roll(cur, s, 0))


def _causal_conv(p, tail8, w):
    kw = w.shape[0]
    acc = p * w[kw - 1:kw, :]
    top = p[0:SUBLANES] * w[kw - 1:kw, :]
    for s in range(1, kw):
        wk = w[kw - 1 - s:kw - s, :]
        acc = acc + pltpu.roll(p, s, 0) * wk
        top = top + _shift_rows(p[0:SUBLANES], tail8, s) * wk
    return jnp.concatenate([top, acc[SUBLANES:]], axis=0)


def _segment_perm(tt):
    r = jnp.arange(tt)
    src = (r % SUBLANES) * (tt // SUBLANES) + r // SUBLANES
    return (src[:, None] == jnp.arange(tt)[None, :]).astype(BF16)


def _rg_kernel(x_ref, g_ref, perm_ref, permt_ref, wx_ref, wgate_ref, cw_ref, cb_ref, wr_ref, br_ref, wi_ref,
               bi_ref, lamc_ref, o_ref, tail_ref, carry_ref, a_scr, h_scr):
    tt, width = o_ref.shape
    n_grp = tt // SUBLANES
    kw = cw_ref.shape[0]

    @pl.when(pl.program_id(1) == 0)
    def _():
        tail_ref[...] = jnp.zeros_like(tail_ref)
        carry_ref[...] = jnp.zeros_like(carry_ref)

    u = _rms(x_ref[...], g_ref[...]).astype(BF16)
    up = _dot(perm_ref[...], u).astype(BF16)
    nc = width // MXU_DIM
    rows8 = lax.broadcasted_iota(jnp.int32, (SUBLANES, MXU_DIM), 0)

    def project(c):
        cs = slice(c * MXU_DIM, (c + 1) * MXU_DIM)
        return _dot(up, wx_ref[:, cs]), _dot(up, wgate_ref[:, cs])

    def conv(c, px):
        cs = slice(c * MXU_DIM, (c + 1) * MXU_DIM)
        tail = tail_ref[:, cs]
        head = [_shift_rows(px[(n_grp - k) * SUBLANES:(n_grp - k + 1) * SUBLANES],
                            tail[(kw - 1 - k) * SUBLANES:(kw - k) * SUBLANES], 1) for k in range(kw - 1, 0, -1)]
        ext = jnp.concatenate(head + [px], axis=0)
        tail_ref[:, cs] = px[tt - (kw - 1) * SUBLANES:tt]
        xa = cb_ref[:, cs] + ext[0:tt] * cw_ref[0:1, cs]
        for k in range(1, kw):
            xa = xa + ext[k * SUBLANES:k * SUBLANES + tt] * cw_ref[k:k + 1, cs]
        return xa

    def gates(c, xa):
        xab = xa.astype(BF16)
        return _dot(xab, wr_ref[c]), _dot(xab, wi_ref[c])

    def scan(c, xa, r_lin, i_lin, pg):
        cs = slice(c * MXU_DIM, (c + 1) * MXU_DIM)
        a = jnp.exp(lamc_ref[:, cs] * jax.nn.sigmoid(r_lin + br_ref[:, cs]))
        a_scr[:, cs] = a
        h_scr[:, cs] = jnp.sqrt(1.0 - a * a) * (jax.nn.sigmoid(i_lin + bi_ref[:, cs]) * xa)
        hl = jnp.zeros((SUBLANES, MXU_DIM), F32)
        ap = jnp.ones((SUBLANES, MXU_DIM), F32)
        for j in range(n_grp):
            rs = slice(j * SUBLANES, (j + 1) * SUBLANES)
            aj = a_scr[rs, cs]
            hl = aj * hl + h_scr[rs, cs]
            ap = aj * ap
            h_scr[rs, cs] = hl
            a_scr[rs, cs] = ap
        for s in (1, 2, 4):
            m = rows8 >= s
            hl = jnp.where(m, ap * pltpu.roll(hl, s, 0) + hl, hl)
            ap = jnp.where(m, ap * pltpu.roll(ap, s, 0), ap)
        cin = carry_ref[:, cs]
        seg_end = hl + ap * cin
        carry_ref[:, cs] = jnp.broadcast_to(seg_end[SUBLANES - 1:SUBLANES, :], seg_end.shape)
        seg_in = jnp.where(rows8 < 1, cin, pltpu.roll(seg_end, 1, 0))
        h = (h_scr[:, cs].reshape(n_grp, SUBLANES, MXU_DIM)
             + a_scr[:, cs].reshape(n_grp, SUBLANES, MXU_DIM) * seg_in[None])
        return (h.reshape(tt, MXU_DIM) * jax.nn.gelu(pg)).astype(BF16)

    def emit(c, y):
        o_ref[:, c * MXU_DIM:(c + 1) * MXU_DIM] = _dot(permt_ref[...], y).astype(BF16)

    proj, xas, gts, ys = {}, {}, {}, {}
    for step in range(nc + 3):
        if step < nc:
            proj[step] = project(step)
        c = step - 1
        if 0 <= c < nc:
            xas[c] = conv(c, proj[c][0])
            gts[c] = gates(c, xas[c])
        c = step - 2
        if 0 <= c < nc:
            ys[c] = scan(c, xas[c], gts[c][0], gts[c][1], proj[c][1])
        c = step - 3
        if 0 <= c < nc:
            emit(c, ys[c])


def _rg_branch(x, g, w_in, layer, cw, cb, wr_t, br, wi_t, bi, lamc, *, tt=256):
    bsz, t, d = x.shape
    width = cw.shape[1]
    perm = _segment_perm(tt)
    return pl.pallas_call(
        _rg_kernel,
        grid=(bsz, t // tt),
        in_specs=[pl.BlockSpec((None, tt, d), lambda b, i: (b, i, 0)), _const_spec((1, d)),
                  _const_spec((tt, tt)), _const_spec((tt, tt)),
                  _layer_spec(d, width, layer, 0), _layer_spec(d, width, layer, 1),
                  _const_spec(cw.shape), _const_spec((1, width)),
                  _const_spec(wr_t.shape), _const_spec((1, width)),
                  _const_spec(wi_t.shape), _const_spec((1, width)), _const_spec((1, width))],
        out_specs=pl.BlockSpec((None, tt, width), lambda b, i: (b, i, 0)),
        out_shape=jax.ShapeDtypeStruct((bsz, t, width), BF16),
        scratch_shapes=[pltpu.VMEM(((cw.shape[0] - 1) * SUBLANES, width), F32), pltpu.VMEM((SUBLANES, width), F32),
                        pltpu.VMEM((tt, width), F32), pltpu.VMEM((tt, width), F32)],
        compiler_params=_params(("arbitrary", "arbitrary")),
        name="rg_branch",
    )(x, g, perm, perm.T, w_in, w_in, cw, cb, wr_t, br, wi_t, bi, lamc)


ATT_BLOCK_BATCH = 4


def _att_kernel(x_ref, g_ref, w_ref, gq_ref, gk_ref, e4_ref, mb_ref, o_ref, qkv_scr, o_scr, lse_scr, *, rt):
    t_total = o_ref.shape[0]
    n_slab = o_scr.shape[0]
    i = pl.program_id(1)
    row0 = pl.multiple_of(i * rt, rt)
    span = ATT_SPAN

    u = _rms(x_ref[...], g_ref[...]).astype(BF16)
    qkv = _dot(u, w_ref[...])
    width = n_slab * LANES
    e4 = e4_ref[...]
    for s in range(n_slab):
        cs = slice(s * LANES, (s + 1) * LANES)
        qk = jnp.concatenate([qkv[:, cs], qkv[:, width + s * LANES: width + (s + 1) * LANES]], axis=1)
        gain = jnp.concatenate([gq_ref[:, cs] * (ATT_HEAD_DIM ** -0.5), gk_ref[:, cs]], axis=1)
        ms = _dot((qk * qk).astype(BF16), e4)
        qk = qk * lax.rsqrt(ms + EPS) * gain
        qkv_scr[s, pl.ds(row0, rt), :] = qk[:, :LANES]
        qkv_scr[n_slab + s, pl.ds(row0, rt), :] = qk[:, LANES:]
        qkv_scr[2 * n_slab + s, pl.ds(row0, rt), :] = qkv[:, 2 * width + s * LANES: 2 * width + (s + 1) * LANES]

    @pl.when(i == pl.num_programs(1) - 1)
    def _():
        lane = lax.broadcasted_iota(jnp.int32, (span, LANES), 1)
        first_head = lane < ATT_HEAD_DIM

        for g, (window, dil) in enumerate(ATT_GROUPS):
            assert window // dil == span
            nblk = t_total // dil // span
            blocks = [(r, n) for n in range(nblk) for r in range(dil)]

            def rows(start, dil=dil):
                return pl.ds(start, span) if dil == 1 else pl.ds(start, span, stride=dil)

            for b0 in range(0, len(blocks), ATT_BLOCK_BATCH):
                probs = [(r + n * span * dil, n > 0, 2 * g + sl)
                         for r, n in blocks[b0:b0 + ATT_BLOCK_BATCH] for sl in range(2)]
                qs, kk, vv = [], [], []
                for start, has_prev, slab in probs:
                    q2 = qkv_scr[slab, rows(start), :]
                    qs.append(jnp.concatenate([jnp.where(first_head, q2, 0.0), jnp.where(first_head, 0.0, q2)],
                                              axis=0).astype(BF16))
                    kc = qkv_scr[n_slab + slab, rows(start), :]
                    vc = qkv_scr[2 * n_slab + slab, rows(start), :]
                    if has_prev:
                        prev = start - span * dil
                        kc = jnp.concatenate([qkv_scr[n_slab + slab, rows(prev), :], kc], axis=0)
                        vc = jnp.concatenate([qkv_scr[2 * n_slab + slab, rows(prev), :], vc], axis=0)
                    kk.append(kc.astype(BF16))
                    vv.append(vc.astype(BF16))
                sc = [_dot_nt(q_, k_) for q_, k_ in zip(qs, kk)]
                sc = [s_ + (mb_ref[slab] if has_prev else mb_ref[slab, :, span:])
                      for s_, (_, has_prev, slab) in zip(sc, probs)]
                mx = [jnp.max(s_, axis=-1, keepdims=True) for s_ in sc]
                pr = [jnp.exp(s_ - m_) for s_, m_ in zip(sc, mx)]
                den = [jnp.sum(p_, axis=-1, keepdims=True) for p_ in pr]
                pv = [_dot(p_.astype(BF16), v_) for p_, v_ in zip(pr, vv)]
                for (start, _, slab), pv_, m_, d_ in zip(probs, pv, mx, den):
                    on = pv_ * (1.0 / d_)
                    lse = jnp.broadcast_to(m_ + jnp.log(d_), on.shape)
                    o_scr[slab, rows(start), :] = jnp.where(first_head, on[:span], on[span:])
                    lse_scr[slab, rows(start), :] = jnp.where(first_head, lse[:span], lse[span:])

        n_grp = len(ATT_GROUPS)
        ct = 256

        def combine(j, carry):
            rws = pl.ds(pl.multiple_of(j * ct, ct), ct)
            for sl in range(2):
                ls = [lse_scr[2 * g + sl, rws, :] for g in range(n_grp)]
                mx = functools.reduce(jnp.maximum, ls)
                es = [jnp.exp(l - mx) for l in ls]
                inv = 1.0 / functools.reduce(lambda a, b: a + b, es)
                for g in range(n_grp):
                    slab = 2 * g + sl
                    o_ref[rws, slab * LANES:(slab + 1) * LANES] = (o_scr[slab, rws, :] * (es[g] * inv)).astype(BF16)
            return carry

        lax.fori_loop(0, t_total // ct, combine, 0)


def _att_branch(x, g, w_qkv, gq, gk, e4, maskbias, *, rt=256):
    bsz, t, d = x.shape
    width = gq.shape[1]
    n_slab = width // LANES
    return pl.pallas_call(
        functools.partial(_att_kernel, rt=rt),
        grid=(bsz, t // rt),
        in_specs=[pl.BlockSpec((None, rt, d), lambda b, i: (b, i, 0)), _const_spec((1, d)),
                  _const_spec(w_qkv.shape), _const_spec((1, width)), _const_spec((1, width)),
                  _const_spec(e4.shape), _const_spec(maskbias.shape)],
        out_specs=pl.BlockSpec((None, t, width), lambda b, i: (b, 0, 0)),
        out_shape=jax.ShapeDtypeStruct((bsz, t, width), BF16),
        scratch_shapes=[pltpu.VMEM((3 * n_slab, t, LANES), F32), pltpu.VMEM((n_slab, t, LANES), F32),
                        pltpu.VMEM((n_slab, t, LANES), F32)],
        compiler_params=_params(("arbitrary", "arbitrary")),
        name="att_branch",
    )(x, g, w_qkv, gq, gk, e4, maskbias)


def _att_maskbias(n_heads):
    slopes = jnp.exp2(-8.0 * jnp.arange(1, n_heads + 1, dtype=F32) / n_heads)
    qi = jnp.arange(ATT_SPAN)[:, None]
    kj = jnp.arange(2 * ATT_SPAN)[None, :]
    delta = qi + ATT_SPAN - kj
    valid = (delta >= 0) & (delta <= ATT_SPAN)
    dil = jnp.repeat(jnp.array([d for _, d in ATT_GROUPS], F32), ATT_HEADS_PER_GROUP)
    bias = -(slopes * dil)[:, None, None] * delta.astype(F32)[None]
    return jnp.where(valid[None], bias, NEG_INF).reshape(n_heads // 2, 2 * ATT_SPAN, 2 * ATT_SPAN)


def _head_mean_matrix(n_lanes, head_dim):
    lane_head = jnp.arange(n_lanes) // head_dim
    return ((lane_head[:, None] == lane_head[None, :]).astype(F32) / head_dim).astype(BF16)


def _cumsum_rows(x):
    n = x.shape[0]
    rows = lax.broadcasted_iota(jnp.int32, x.shape, 0)
    s = 1
    while s < n:
        x = x + jnp.where(rows >= s, pltpu.roll(x, s, 0), 0.0)
        s *= 2
    return x


def _dn_kernel(x_ref, g_ref, w_ref, cw_ref, aexp_ref, dtb_ref, on_ref, o_ref, tail_ref, s_scr):
    tt, width = o_ref.shape
    n_heads = width // DN_HEAD_DIM
    c = DN_CHUNK

    @pl.when(pl.program_id(1) == 0)
    def _():
        tail_ref[...] = jnp.zeros_like(tail_ref)
        s_scr[...] = jnp.zeros_like(s_scr)

    u = _rms(x_ref[...], g_ref[...]).astype(BF16)
    proj = _dot(u, w_ref[...])
    pqkv = proj[:, :3 * width]
    conv = _causal_conv(pqkv, tail_ref[...], cw_ref[...])
    tail_ref[...] = pqkv[tt - SUBLANES:tt]
    qkv = conv * jax.nn.sigmoid(conv)
    z = proj[:, 3 * width:4 * width]
    ba = proj[:, 4 * width:4 * width + LANES]
    beta_all = jax.nn.sigmoid(ba)
    glog_all = -aexp_ref[...] * _softplus(ba + dtb_ref[...])

    assert 2 * c == LANES and n_heads % 4 == 0
    n_ch = tt // c
    n_pair = n_heads // 2
    row2 = lax.broadcasted_iota(jnp.int32, (c, LANES), 0)
    lane2 = lax.broadcasted_iota(jnp.int32, (c, LANES), 1)
    col2 = lane2 & (c - 1)
    incl = col2 <= row2
    strict = col2 < row2
    lo2 = lane2 < c
    lane2x = lax.broadcasted_iota(jnp.int32, (LANES, LANES), 1)
    lo2x = lane2x < c
    row4 = lax.broadcasted_iota(jnp.int32, (c, 2 * LANES), 0)
    lane4 = lax.broadcasted_iota(jnp.int32, (c, 2 * LANES), 1)
    eye4 = ((lane4 & (c - 1)) == row4).astype(F32)
    blk4 = lane4 // c
    zero_k = jnp.zeros((c, DN_HEAD_DIM), BF16)

    def block_diag(m):
        return jnp.concatenate([jnp.where(blk4 == i, m, 0.0) for i in range(4)], axis=0).astype(BF16)

    def cols(base, h):
        return slice(base * width + h * DN_HEAD_DIM, base * width + (h + 1) * DN_HEAD_DIM)

    def l2n(m):
        return m * lax.rsqrt(jnp.sum(m * m, axis=-1, keepdims=True) + EPS)

    probs = [(ch, h) for ch in range(n_ch) for h in range(n_heads)]
    pairs = [(ch, p) for ch in range(n_ch) for p in range(n_pair)]
    quads = [(ch, qd) for ch in range(n_ch) for qd in range(n_pair // 2)]
    pix = {ph: i for i, ph in enumerate(probs)}
    gam_all = [_cumsum_rows(glog_all[ch * c:(ch + 1) * c]) for ch in range(n_ch)]
    gam_t = [jnp.concatenate([g_, g_], axis=0).T for g_ in gam_all]

    q = [l2n(qkv[ch * c:(ch + 1) * c, cols(0, h)]) * (DN_HEAD_DIM ** -0.5) for ch, h in probs]
    k = [l2n(qkv[ch * c:(ch + 1) * c, cols(1, h)]) for ch, h in probs]
    gcol = [gam_all[ch][:, n_heads + h:n_heads + h + 1] for ch, h in probs]
    glast = [gc[c - 1:c, :] for gc in gcol]
    eg = [jnp.exp(gc) for gc in gcol]
    kb = [k_ * beta_all[ch * c:(ch + 1) * c, h:h + 1] for k_, (ch, h) in zip(k, probs)]
    rhs = [jnp.concatenate([qkv[ch * c:(ch + 1) * c, cols(2, h)] * beta_all[ch * c:(ch + 1) * c, h:h + 1],
                            kb_ * eg_], axis=1).astype(BF16) for (ch, h), kb_, eg_ in zip(probs, kb, eg)]
    kbq = [jnp.concatenate([kb_, q_], axis=0).astype(BF16) for kb_, q_ in zip(kb, q)]
    kd = [k_ * jnp.exp(gl - gc) for k_, gl, gc in zip(k, glast, gcol)]

    def both(lst, ch, p):
        return lst[pix[(ch, 2 * p)]], lst[pix[(ch, 2 * p + 1)]]

    kq = []
    for ch, p in pairs:
        (x0, x1), (k0, k1) = both(kbq, ch, p), both(k, ch, p)
        kq.append(_dot_nt(x0, jnp.concatenate([k0.astype(BF16), zero_k], axis=0))
                  + _dot_nt(x1, jnp.concatenate([zero_k, k1.astype(BF16)], axis=0)))
    decay = []
    for ch, p in pairs:
        g0, g1 = both(gcol, ch, p)
        gr = jnp.where(lo2[0:1], gam_t[ch][n_heads + 2 * p:n_heads + 2 * p + 1, :],
                       gam_t[ch][n_heads + 2 * p + 1:n_heads + 2 * p + 2, :])
        diff = jnp.where(lo2, g0, g1) - gr
        decay.append(jnp.where(incl, jnp.exp(jnp.where(incl, diff, 0.0)), 0.0))
    a2 = [jnp.where(strict, kq_[:c] * d_, 0.0) for kq_, d_ in zip(kq, decay)]
    qk2 = [jnp.where(incl, kq_[c:] * d_, 0.0) for kq_, d_ in zip(kq, decay)]

    a4 = [jnp.concatenate([a2[ch * n_pair + 2 * qd], a2[ch * n_pair + 2 * qd + 1]], axis=1) for ch, qd in quads]
    tinv = [eye4 - a_ for a_ in a4]
    pw = [_dot(a_.astype(BF16), block_diag(a_)) for a_ in a4]
    for _ in range(int(math.log2(c)) - 2):
        r = [_dot(jnp.concatenate([t_, p_], axis=0).astype(BF16), block_diag(p_)) for t_, p_ in zip(tinv, pw)]
        tinv = [t_ + r_[:c] for t_, r_ in zip(tinv, r)]
        pw = [r_[c:] for r_ in r]
    tinv = [t_ + _dot(t_.astype(BF16), block_diag(p_)) for t_, p_ in zip(tinv, pw)]

    def split_rows(m):
        return jnp.concatenate([jnp.where(lo2, m, 0.0), jnp.where(lo2, 0.0, m)], axis=0).astype(BF16)

    sol, lhs_o = [], []
    for i, (ch, p) in enumerate(pairs):
        t2 = tinv[i // 2][:, (i % 2) * LANES:(i % 2 + 1) * LANES]
        r0, r1 = both(rhs, ch, p)
        sol.append(_dot(split_rows(t2), jnp.concatenate([r0, r1], axis=0)))
        kd0, kd1 = both(kd, ch, p)
        kdt = jnp.concatenate([kd0, kd1], axis=0).T
        lhs_o.append(jnp.concatenate([split_rows(qk2[i]), jnp.where(lo2x, kdt, 0.0).astype(BF16),
                                      jnp.where(lo2x, 0.0, kdt).astype(BF16)], axis=0))
    wq = []
    for ch, h in probs:
        s_ = sol[ch * n_pair + h // 2][(h % 2) * c:(h % 2 + 1) * c]
        i = pix[(ch, h)]
        wq.append(jnp.concatenate([s_[:, DN_HEAD_DIM:], q[i] * eg[i]], axis=0).astype(BF16))

    for ch in range(n_ch):
        rs = slice(ch * c, (ch + 1) * c)
        state = [s_scr[h] for h in range(n_heads)]
        ws = [_dot(wq[pix[(ch, h)]], state[h].astype(BF16)) for h in range(n_heads)]
        un = [sol[ch * n_pair + h // 2][(h % 2) * c:(h % 2 + 1) * c, :DN_HEAD_DIM] - ws[h][:c] for h in range(n_heads)]
        ou = [_dot(lhs_o[ch * n_pair + p], jnp.concatenate([un[2 * p], un[2 * p + 1]], axis=0).astype(BF16))
              for p in range(n_pair)]
        for h in range(n_heads):
            hs = slice(h * DN_HEAD_DIM, (h + 1) * DN_HEAD_DIM)
            o2 = ou[h // 2]
            sub = h % 2
            s_scr[h] = (state[h] * jnp.exp(glast[pix[(ch, h)]])
                        + o2[2 * c + sub * DN_HEAD_DIM:2 * c + (sub + 1) * DN_HEAD_DIM])
            on = _rms(ws[h][c:] + o2[sub * c:(sub + 1) * c], on_ref[:, hs])
            zz = z[rs, hs]
            o_ref[rs, hs] = (on * (zz * jax.nn.sigmoid(zz))).astype(BF16)


def _dn_branch(x, g, w_dn, cw, aexp, dtb, onorm, *, tt=256):
    bsz, t, d = x.shape
    width = onorm.shape[1]
    n_heads = width // DN_HEAD_DIM
    return pl.pallas_call(
        _dn_kernel,
        grid=(bsz, t // tt),
        in_specs=[pl.BlockSpec((None, tt, d), lambda b, i: (b, i, 0)), _const_spec((1, d)),
                  _const_spec(w_dn.shape), _const_spec(cw.shape),
                  _const_spec((1, LANES)), _const_spec((1, LANES)), _const_spec((1, width))],
        out_specs=pl.BlockSpec((None, tt, width), lambda b, i: (b, i, 0)),
        out_shape=jax.ShapeDtypeStruct((bsz, t, width), BF16),
        scratch_shapes=[pltpu.VMEM((SUBLANES, 3 * width), F32),
                        pltpu.VMEM((n_heads, DN_HEAD_DIM, DN_HEAD_DIM), F32)],
        compiler_params=_params(("arbitrary", "arbitrary")),
        name="dn_branch",
    )(x, g, w_dn, cw, aexp, dtb, onorm)


def _merge_kernel(x_ref, g_ref, ya_ref, yb_ref, yc_ref, wm_ref, wa_ref, wb_ref, wc_ref, wo_ref, o_ref):
    x = x_ref[...]
    d = x.shape[1]
    u = _rms(x, g_ref[...]).astype(BF16)
    y = jnp.zeros_like(x)
    for j, (yr, wr) in enumerate(((ya_ref, wa_ref), (yb_ref, wb_ref), (yc_ref, wc_ref))):
        gate = jax.nn.sigmoid(_dot(u, wm_ref[:, j * d:(j + 1) * d]))
        y = y + gate * _dot(yr[...], wr[...])
    o_ref[...] = x + _dot(y.astype(BF16), wo_ref[...])


def _merge(x2, g, ya, yb, yc, wm, wa, wb, wc, wo, *, tm=512):
    n, d = x2.shape
    row = lambda i: (i, 0)
    return pl.pallas_call(
        _merge_kernel,
        grid=(n // tm,),
        in_specs=[pl.BlockSpec((tm, d), row), _const_spec((1, d)),
                  pl.BlockSpec((tm, ya.shape[1]), row), pl.BlockSpec((tm, yb.shape[1]), row),
                  pl.BlockSpec((tm, yc.shape[1]), row),
                  _const_spec(wm.shape), _const_spec(wa.shape), _const_spec(wb.shape), _const_spec(wc.shape),
                  _const_spec(wo.shape)],
        out_specs=pl.BlockSpec((tm, d), row),
        out_shape=jax.ShapeDtypeStruct((n, d), F32),
        compiler_params=_params(("arbitrary",)),
        name="merge",
    )(x2, g, ya, yb, yc, wm, wa, wb, wc, wo)


def _block_diag_tiles(w):
    nb, k, _ = w.shape
    per = MXU_DIM // k
    w4 = w.reshape(nb // per, per, k, k)
    eye = jnp.eye(per, dtype=w.dtype)
    return jnp.einsum('tpij,pq->tpiqj', w4, eye).reshape(nb // per, MXU_DIM, MXU_DIM)


def _row(v):
    return v.reshape(1, -1).astype(F32)


def _mixer(x, layer, mix_norm, w_in, rg_conv_w, rg_conv_b, rg_w_r, rg_b_r, rg_w_i, rg_b_i, rg_lambda,
           att_q_norm, att_k_norm, dn_conv_w, dn_a_log, dn_dt_bias, dn_out_norm, w_branch, w_out):
    bsz, t, d = x.shape
    rg_w = rg_conv_w.shape[1]
    att_w = att_q_norm.size
    dn_w = dn_out_norm.size
    dn_h = dn_out_norm.shape[0]
    off = 2 * rg_w

    def take(n):
        nonlocal off
        w = w_in[layer, :, off:off + n]
        off += n
        return w

    w_att = take(3 * att_w)
    w_dn_main = take(4 * dn_w)
    w_dn_ba = take(2 * dn_h)
    w_merge = take(3 * d)
    w_dn = jnp.concatenate([w_dn_main, w_dn_ba, jnp.zeros((d, LANES - 2 * dn_h), BF16)], axis=1)
    g = _row(mix_norm)

    lamc = _row(-RG_C * jax.nn.softplus(-rg_lambda.astype(F32)))
    ya = _rg_branch(x, g, w_in, layer, rg_conv_w, _row(rg_conv_b),
                    _block_diag_tiles(rg_w_r).astype(BF16), _row(rg_b_r),
                    _block_diag_tiles(rg_w_i).astype(BF16), _row(rg_b_i), lamc)

    n_att_heads = att_q_norm.shape[0]
    yb = _att_branch(x, g, w_att, _row(att_q_norm), _row(att_k_norm),
                     _head_mean_matrix(2 * LANES, ATT_HEAD_DIM), _att_maskbias(n_att_heads))

    pad = jnp.zeros((LANES - 2 * dn_h,), F32)
    aexp = _row(jnp.concatenate([jnp.zeros((dn_h,), F32), jnp.exp(dn_a_log.astype(F32)), pad]))
    dtb = _row(jnp.concatenate([jnp.zeros((dn_h,), F32), dn_dt_bias.astype(F32), pad]))
    yc = _dn_branch(x, g, w_dn, dn_conv_w, aexp, dtb, _row(dn_out_norm))

    wa = w_branch[layer, :rg_w]
    wb = w_branch[layer, rg_w:rg_w + att_w]
    wc = w_branch[layer, rg_w + att_w:]
    n = bsz * t
    out = _merge(x.reshape(n, d), g, ya.reshape(n, rg_w), yb.reshape(n, att_w), yc.reshape(n, dn_w),
                 w_merge, wa, wb, wc, w_out[layer])
    return out.reshape(bsz, t, d)


def kernel(x, ffn1_norm, ffn1_w_gate, ffn1_w_up, ffn1_w_down, mix_norm, w_in, rg_conv_w, rg_conv_b, rg_w_r, rg_b_r, rg_w_i, rg_b_i, rg_lambda, att_q_norm, att_k_norm, dn_conv_w, dn_a_log, dn_dt_bias, dn_out_norm, w_branch, w_out, ffn2_norm, ffn2_w_gate, ffn2_w_up, ffn2_w_down):
    bsz, t, d = x.shape
    n = bsz * t
    f1g, f1u, f1d = ffn1_w_gate.astype(BF16), ffn1_w_up.astype(BF16), ffn1_w_down.astype(BF16)
    f2g, f2u, f2d = ffn2_w_gate.astype(BF16), ffn2_w_up.astype(BF16), ffn2_w_down.astype(BF16)
    w_in_b, w_branch_b, w_out_b = w_in.astype(BF16), w_branch.astype(BF16), w_out.astype(BF16)
    for l in range(ffn1_norm.shape[0]):
        x = _ffn(x.reshape(n, d), _row(ffn1_norm[l]), f1g, f1u, f1d, l).reshape(bsz, t, d)
        x = _mixer(x, l, mix_norm[l], w_in_b, rg_conv_w[l], rg_conv_b[l], rg_w_r[l], rg_b_r[l], rg_w_i[l],
                   rg_b_i[l], rg_lambda[l], att_q_norm[l], att_k_norm[l], dn_conv_w[l], dn_a_log[l],
                   dn_dt_bias[l], dn_out_norm[l], w_branch_b, w_out_b)
        x = _ffn(x.reshape(n, d), _row(ffn2_norm[l]), f2g, f2u, f2d, l).reshape(bsz, t, d)
    return x
```

```python
import functools
import math

import jax
import jax.numpy as jnp
from jax import lax
from jax.experimental import pallas as pl
from jax.experimental.pallas import tpu as pltpu

F32 = jnp.float32
BF16 = jnp.bfloat16

EPS = 1e-6
NEG_INF = -1e30
RG_C = 8.0
RG_BLOCK_DIM = 64
CONV_WIDTH = 4
ATT_GROUPS = ((128, 1), (512, 4), (2048, 16))
ATT_HEADS_PER_GROUP = 4
ATT_HEAD_DIM = 64
ATT_SPAN = 128
DN_HEAD_DIM = 128
DN_CHUNK = 64

LANES = 128
SUBLANES = 8
MXU_DIM = 256
VMEM_LIMIT_BYTES = 56 * 1024 * 1024


def _rms(x, g):
    return x * lax.rsqrt(jnp.mean(x * x, axis=-1, keepdims=True) + EPS) * g


def _dot(a, b):
    return jnp.dot(a, b, preferred_element_type=F32)


def _dot_nt(a, b):
    return lax.dot_general(a, b, (((1,), (1,)), ((), ())), preferred_element_type=F32)


def _dot_tn(a, b):
    return lax.dot_general(a, b, (((0,), (0,)), ((), ())), preferred_element_type=F32)


def _softplus(z):
    return jnp.maximum(z, 0.0) + jnp.log(1.0 + jnp.exp(-jnp.abs(z)))


def _const_spec(shape):
    nd = len(shape)
    return pl.BlockSpec(shape, lambda *_: (0,) * nd)


def _layer_spec(rows, cols, layer, col_block=0):
    return pl.BlockSpec((None, rows, cols), lambda *_: (layer, 0, col_block))


def _params(semantics):
    return pltpu.CompilerParams(dimension_semantics=semantics, vmem_limit_bytes=VMEM_LIMIT_BYTES)


def _ffn_kernel(x_ref, g_ref, wg_ref, wu_ref, wd_ref, o_ref, *, f_chunk):
    x = x_ref[...]
    u = _rms(x, g_ref[...]).astype(BF16)
    acc = jnp.zeros_like(x)
    for c in range(wg_ref.shape[1] // f_chunk):
        sl = slice(c * f_chunk, (c + 1) * f_chunk)
        hg = _dot(u, wg_ref[:, sl])
        hu = _dot(u, wu_ref[:, sl])
        h = (hg * jax.nn.sigmoid(hg) * hu).astype(BF16)
        acc = acc + _dot(h, wd_ref[sl, :])
    o_ref[...] = x + 0.5 * acc


def _ffn(x2, g, wg, wu, wd, layer, *, tm=512, f_chunk=256):
    n, d = x2.shape
    f = wg.shape[2]
    return pl.pallas_call(
        functools.partial(_ffn_kernel, f_chunk=f_chunk),
        grid=(n // tm,),
        in_specs=[pl.BlockSpec((tm, d), lambda i: (i, 0)), _const_spec((1, d)),
                  _layer_spec(d, f, layer), _layer_spec(d, f, layer), _layer_spec(f, d, layer)],
        out_specs=pl.BlockSpec((tm, d), lambda i: (i, 0)),
        out_shape=jax.ShapeDtypeStruct((n, d), F32),
        compiler_params=_params(("arbitrary",)),
        name="ffn",
    )(x2, g, wg, wu, wd)


def _shift_rows(cur, prev8, s):
    rows = lax.broadcasted_iota(jnp.int32, cur.shape, 0)
    return jnp.where(rows < s, pltpu.roll(prev8, s, 0), pltpu.roll(cur, s, 0))


def _causal_conv(p, tail8, w):
    kw = w.shape[0]
    acc = p * w[kw - 1:kw, :]
    top = p[0:SUBLANES] * w[kw - 1:kw, :]
    for s in range(1, kw):
        wk = w[kw - 1 - s:kw - s, :]
        acc = acc + pltpu.roll(p, s, 0) * wk
        top = top + _shift_rows(p[0:SUBLANES], tail8, s) * wk
    return jnp.concatenate([top, acc[SUBLANES:]], axis=0)


def _segment_perm(tt):
    r = jnp.arange(tt)
    src = (r % SUBLANES) * (tt // SUBLANES) + r // SUBLANES
    return (src[:, None] == jnp.arange(tt)[None, :]).astype(BF16)


def _rg_kernel(x_ref, g_ref, perm_ref, permt_ref, wx_ref, wgate_ref, cw_ref, cb_ref, wr_ref, br_ref, wi_ref,
               bi_ref, lamc_ref, o_ref, tail_ref, carry_ref, a_scr, h_scr):
    tt, width = o_ref.shape
    n_grp = tt // SUBLANES
    kw = cw_ref.shape[0]

    @pl.when(pl.program_id(1) == 0)
    def _():
        tail_ref[...] = jnp.zeros_like(tail_ref)
        carry_ref[...] = jnp.zeros_like(carry_ref)

    u = _rms(x_ref[...], g_ref[...]).astype(BF16)
    up = _dot(perm_ref[...], u).astype(BF16)
    nc = width // MXU_DIM
    rows8 = lax.broadcasted_iota(jnp.int32, (SUBLANES, MXU_DIM), 0)

    def project(c):
        cs = slice(c * MXU_DIM, (c + 1) * MXU_DIM)
        return _dot(up, wx_ref[:, cs]), _dot(up, wgate_ref[:, cs])

    def conv(c, px):
        cs = slice(c * MXU_DIM, (c + 1) * MXU_DIM)
        tail = tail_ref[:, cs]
        head = [_shift_rows(px[(n_grp - k) * SUBLANES:(n_grp - k + 1) * SUBLANES],
                            tail[(kw - 1 - k) * SUBLANES:(kw - k) * SUBLANES], 1) for k in range(kw - 1, 0, -1)]
        ext = jnp.concatenate(head + [px], axis=0)
        tail_ref[:, cs] = px[tt - (kw - 1) * SUBLANES:tt]
        xa = cb_ref[:, cs] + ext[0:tt] * cw_ref[0:1, cs]
        for k in range(1, kw):
            xa = xa + ext[k * SUBLANES:k * SUBLANES + tt] * cw_ref[k:k + 1, cs]
        return xa

    def gates(c, xa):
        xab = xa.astype(BF16)
        return _dot(xab, wr_ref[c]), _dot(xab, wi_ref[c])

    def scan(c, xa, r_lin, i_lin, pg):
        cs = slice(c * MXU_DIM, (c + 1) * MXU_DIM)
        a = jnp.exp(lamc_ref[:, cs] * jax.nn.sigmoid(r_lin + br_ref[:, cs]))
        a_scr[:, cs] = a
        h_scr[:, cs] = jnp.sqrt(1.0 - a * a) * (jax.nn.sigmoid(i_lin + bi_ref[:, cs]) * xa)
        hl = jnp.zeros((SUBLANES, MXU_DIM), F32)
        ap = jnp.ones((SUBLANES, MXU_DIM), F32)
        for j in range(n_grp):
            rs = slice(j * SUBLANES, (j + 1) * SUBLANES)
            aj = a_scr[rs, cs]
            hl = aj * hl + h_scr[rs, cs]
            ap = aj * ap
            h_scr[rs, cs] = hl
            a_scr[rs, cs] = ap
        for s in (1, 2, 4):
            m = rows8 >= s
            hl = jnp.where(m, ap * pltpu.roll(hl, s, 0) + hl, hl)
            ap = jnp.where(m, ap * pltpu.roll(ap, s, 0), ap)
        cin = carry_ref[:, cs]
        seg_end = hl + ap * cin
        carry_ref[:, cs] = jnp.broadcast_to(seg_end[SUBLANES - 1:SUBLANES, :], seg_end.shape)
        seg_in = jnp.where(rows8 < 1, cin, pltpu.roll(seg_end, 1, 0))
        h = (h_scr[:, cs].reshape(n_grp, SUBLANES, MXU_DIM)
             + a_scr[:, cs].reshape(n_grp, SUBLANES, MXU_DIM) * seg_in[None])
        return (h.reshape(tt, MXU_DIM) * jax.nn.gelu(pg)).astype(BF16)

    def emit(c, y):
        o_ref[:, c * MXU_DIM:(c + 1) * MXU_DIM] = _dot(permt_ref[...], y).astype(BF16)

    proj, xas, gts, ys = {}, {}, {}, {}
    for step in range(nc + 3):
        if step < nc:
            proj[step] = project(step)
        c = step - 1
        if 0 <= c < nc:
            xas[c] = conv(c, proj[c][0])
            gts[c] = gates(c, xas[c])
        c = step - 2
        if 0 <= c < nc:
            ys[c] = scan(c, xas[c], gts[c][0], gts[c][1], proj[c][1])
        c = step - 3
        if 0 <= c < nc:
            emit(c, ys[c])


def _rg_branch(x, g, w_in, layer, cw, cb, wr_t, br, wi_t, bi, lamc, *, tt=256):
    bsz, t, d = x.shape
    width = cw.shape[1]
    perm = _segment_perm(tt)
    return pl.pallas_call(
        _rg_kernel,
        grid=(bsz, t // tt),
        in_specs=[pl.BlockSpec((None, tt, d), lambda b, i: (b, i, 0)), _const_spec((1, d)),
                  _const_spec((tt, tt)), _const_spec((tt, tt)),
                  _layer_spec(d, width, layer, 0), _layer_spec(d, width, layer, 1),
                  _const_spec(cw.shape), _const_spec((1, width)),
                  _const_spec(wr_t.shape), _const_spec((1, width)),
                  _const_spec(wi_t.shape), _const_spec((1, width)), _const_spec((1, width))],
        out_specs=pl.BlockSpec((None, tt, width), lambda b, i: (b, i, 0)),
        out_shape=jax.ShapeDtypeStruct((bsz, t, width), BF16),
        scratch_shapes=[pltpu.VMEM(((cw.shape[0] - 1) * SUBLANES, width), F32), pltpu.VMEM((SUBLANES, width), F32),
                        pltpu.VMEM((tt, width), F32), pltpu.VMEM((tt, width), F32)],
        compiler_params=_params(("arbitrary", "arbitrary")),
        name="rg_branch",
    )(x, g, perm, perm.T, w_in, w_in, cw, cb, wr_t, br, wi_t, bi, lamc)


ATT_BLOCK_BATCH = 4


def _att_kernel(x_ref, g_ref, w_ref, gq_ref, gk_ref, e4_ref, mb_ref, o_ref, qkv_scr, o_scr, lse_scr, *, rt):
    t_total = o_ref.shape[0]
    n_slab = o_scr.shape[0]
    i = pl.program_id(1)
    row0 = pl.multiple_of(i * rt, rt)
    span = ATT_SPAN

    u = _rms(x_ref[...], g_ref[...]).astype(BF16)
    qkv = _dot(u, w_ref[...])
    width = n_slab * LANES
    e4 = e4_ref[...]
    for s in range(n_slab):
        cs = slice(s * LANES, (s + 1) * LANES)
        qk = jnp.concatenate([qkv[:, cs], qkv[:, width + s * LANES: width + (s + 1) * LANES]], axis=1)
        gain = jnp.concatenate([gq_ref[:, cs] * (ATT_HEAD_DIM ** -0.5), gk_ref[:, cs]], axis=1)
        ms = _dot((qk * qk).astype(BF16), e4)
        qk = qk * lax.rsqrt(ms + EPS) * gain
        qkv_scr[s, pl.ds(row0, rt), :] = qk[:, :LANES]
        qkv_scr[n_slab + s, pl.ds(row0, rt), :] = qk[:, LANES:]
        qkv_scr[2 * n_slab + s, pl.ds(row0, rt), :] = qkv[:, 2 * width + s * LANES: 2 * width + (s + 1) * LANES]

    @pl.when(i == pl.num_programs(1) - 1)
    def _():
        lane = lax.broadcasted_iota(jnp.int32, (span, LANES), 1)
        first_head = lane < ATT_HEAD_DIM

        for g, (window, dil) in enumerate(ATT_GROUPS):
            assert window // dil == span
            nblk = t_total // dil // span
            blocks = [(r, n) for n in range(nblk) for r in range(dil)]

            def rows(start, dil=dil):
                return pl.ds(start, span) if dil == 1 else pl.ds(start, span, stride=dil)

            for b0 in range(0, len(blocks), ATT_BLOCK_BATCH):
                probs = [(r + n * span * dil, n > 0, 2 * g + sl)
                         for r, n in blocks[b0:b0 + ATT_BLOCK_BATCH] for sl in range(2)]
                qs, kk, vv = [], [], []
                for start, has_prev, slab in probs:
                    q2 = qkv_scr[slab, rows(start), :]
                    qs.append(jnp.concatenate([jnp.where(first_head, q2, 0.0), jnp.where(first_head, 0.0, q2)],
                                              axis=0).astype(BF16))
                    kc = qkv_scr[n_slab + slab, rows(start), :]
                    vc = qkv_scr[2 * n_slab + slab, rows(start), :]
                    if has_prev:
                        prev = start - span * dil
                        kc = jnp.concatenate([qkv_scr[n_slab + slab, rows(prev), :], kc], axis=0)
                        vc = jnp.concatenate([qkv_scr[2 * n_slab + slab, rows(prev), :], vc], axis=0)
                    kk.append(kc.astype(BF16))
                    vv.append(vc.astype(BF16))
                sc = [_dot_nt(q_, k_) for q_, k_ in zip(qs, kk)]
                sc = [s_ + (mb_ref[slab] if has_prev else mb_ref[slab, :, span:])
                      for s_, (_, has_prev, slab) in zip(sc, probs)]
                mx = [jnp.max(s_, axis=-1, keepdims=True) for s_ in sc]
                pr = [jnp.exp(s_ - m_) for s_, m_ in zip(sc, mx)]
                den = [jnp.sum(p_, axis=-1, keepdims=True) for p_ in pr]
                pv = [_dot(p_.astype(BF16), v_) for p_, v_ in zip(pr, vv)]
                for (start, _, slab), pv_, m_, d_ in zip(probs, pv, mx, den):
                    on = pv_ * (1.0 / d_)
                    lse = jnp.broadcast_to(m_ + jnp.log(d_), on.shape)
                    o_scr[slab, rows(start), :] = jnp.where(first_head, on[:span], on[span:])
                    lse_scr[slab, rows(start), :] = jnp.where(first_head, lse[:span], lse[span:])

        n_grp = len(ATT_GROUPS)
        ct = 256

        def combine(j, carry):
            rws = pl.ds(pl.multiple_of(j * ct, ct), ct)
            for sl in range(2):
                ls = [lse_scr[2 * g + sl, rws, :] for g in range(n_grp)]
                mx = functools.reduce(jnp.maximum, ls)
                es = [jnp.exp(l - mx) for l in ls]
                inv = 1.0 / functools.reduce(lambda a, b: a + b, es)
                for g in range(n_grp):
                    slab = 2 * g + sl
                    o_ref[rws, slab * LANES:(slab + 1) * LANES] = (o_scr[slab, rws, :] * (es[g] * inv)).astype(BF16)
            return carry

        lax.fori_loop(0, t_total // ct, combine, 0)


def _att_branch(x, g, w_qkv, gq, gk, e4, maskbias, *, rt=256):
    bsz, t, d = x.shape
    width = gq.shape[1]
    n_slab = width // LANES
    return pl.pallas_call(
        functools.partial(_att_kernel, rt=rt),
        grid=(bsz, t // rt),
        in_specs=[pl.BlockSpec((None, rt, d), lambda b, i: (b, i, 0)), _const_spec((1, d)),
                  _const_spec(w_qkv.shape), _const_spec((1, width)), _const_spec((1, width)),
                  _const_spec(e4.shape), _const_spec(maskbias.shape)],
        out_specs=pl.BlockSpec((None, t, width), lambda b, i: (b, 0, 0)),
        out_shape=jax.ShapeDtypeStruct((bsz, t, width), BF16),
        scratch_shapes=[pltpu.VMEM((3 * n_slab, t, LANES), F32), pltpu.VMEM((n_slab, t, LANES), F32),
                        pltpu.VMEM((n_slab, t, LANES), F32)],
        compiler_params=_params(("arbitrary", "arbitrary")),
        name="att_branch",
    )(x, g, w_qkv, gq, gk, e4, maskbias)


def _att_maskbias(n_heads):
    slopes = jnp.exp2(-8.0 * jnp.arange(1, n_heads + 1, dtype=F32) / n_heads)
    qi = jnp.arange(ATT_SPAN)[:, None]
    kj = jnp.arange(2 * ATT_SPAN)[None, :]
    delta = qi + ATT_SPAN - kj
    valid = (delta >= 0) & (delta <= ATT_SPAN)
    dil = jnp.repeat(jnp.array([d for _, d in ATT_GROUPS], F32), ATT_HEADS_PER_GROUP)
    bias = -(slopes * dil)[:, None, None] * delta.astype(F32)[None]
    return jnp.where(valid[None], bias, NEG_INF).reshape(n_heads // 2, 2 * ATT_SPAN, 2 * ATT_SPAN)


def _head_mean_matrix(n_lanes, head_dim):
    lane_head = jnp.arange(n_lanes) // head_dim
    return ((lane_head[:, None] == lane_head[None, :]).astype(F32) / head_dim).astype(BF16)


def _cumsum_rows(x):
    n = x.shape[0]
    rows = lax.broadcasted_iota(jnp.int32, x.shape, 0)
    s = 1
    while s < n:
        x = x + jnp.where(rows >= s, pltpu.roll(x, s, 0), 0.0)
        s *= 2
    return x


def _dn_kernel(x_ref, g_ref, w_ref, cw_ref, aexp_ref, dtb_ref, on_ref, o_ref, tail_ref, s_scr,
               kbq_s, k_s, rhs_s, kd_s, qeg_s, gam_s, gamt_s, z_s, *, tiles_per_seq):
    tt, width = o_ref.shape
    n_heads = width // DN_HEAD_DIM
    c = DN_CHUNK
    step = pl.program_id(0)
    stage_scratch = (kbq_s, k_s, rhs_s, kd_s, qeg_s, gam_s, gamt_s, z_s)

    @pl.when(step == 0)
    def _():
        for ref in stage_scratch:
            ref[...] = jnp.zeros_like(ref)

    @pl.when(step % tiles_per_seq == 0)
    def _():
        tail_ref[...] = jnp.zeros_like(tail_ref)

    @pl.when((step % tiles_per_seq == 1) | (step == 0))
    def _():
        s_scr[...] = jnp.zeros_like(s_scr)

    u = _rms(x_ref[...], g_ref[...]).astype(BF16)
    parts = {}

    def project(*idx):
        for j in idx:
            parts[j] = _dot(u, w_ref[:, j * MXU_DIM:(j + 1) * MXU_DIM])

    assert 2 * c == LANES and n_heads % 4 == 0
    n_ch = tt // c
    n_pair = n_heads // 2
    row2 = lax.broadcasted_iota(jnp.int32, (c, LANES), 0)
    lane2 = lax.broadcasted_iota(jnp.int32, (c, LANES), 1)
    col2 = lane2 & (c - 1)
    incl = col2 <= row2
    strict = col2 < row2
    lo2 = lane2 < c
    lane2x = lax.broadcasted_iota(jnp.int32, (LANES, LANES), 1)
    lo2x = lane2x < c
    row4 = lax.broadcasted_iota(jnp.int32, (c, 2 * LANES), 0)
    lane4 = lax.broadcasted_iota(jnp.int32, (c, 2 * LANES), 1)
    eye4 = ((lane4 & (c - 1)) == row4).astype(F32)
    blk4 = lane4 // c
    zero_k = jnp.zeros((c, DN_HEAD_DIM), BF16)

    def block_diag(m):
        return jnp.concatenate([jnp.where(blk4 == i, m, 0.0) for i in range(4)], axis=0).astype(BF16)

    def cols(base, h):
        return slice(base * width + h * DN_HEAD_DIM, base * width + (h + 1) * DN_HEAD_DIM)

    def l2n(m):
        return m * lax.rsqrt(jnp.sum(m * m, axis=-1, keepdims=True) + EPS)

    probs = [(ch, h) for ch in range(n_ch) for h in range(n_heads)]
    pairs = [(ch, p) for ch in range(n_ch) for p in range(n_pair)]
    quads = [(ch, qd) for ch in range(n_ch) for qd in range(n_pair // 2)]
    pix = {ph: i for i, ph in enumerate(probs)}

    n_prob = len(probs)
    staged = {name: [None] * n_prob for name in ("kbq", "k", "rhs", "kd", "qeg")}
    common = {}
    ba_piece = 4 * n_pair

    def prep_common():
        ba = parts[ba_piece][:, :LANES]
        common["beta"] = jax.nn.sigmoid(ba)
        glog_all = -aexp_ref[...] * _softplus(ba + dtb_ref[...])
        common["gam"] = [_cumsum_rows(glog_all[ch * c:(ch + 1) * c]) for ch in range(n_ch)]
        common["gam_t"] = [jnp.concatenate([g_, g_], axis=0).T for g_ in common["gam"]]

    def prep_pair(p):
        def conv_silu(section):
            blk = parts[section * n_pair + p]
            cs = slice(section * width + p * MXU_DIM, section * width + (p + 1) * MXU_DIM)
            y = _causal_conv(blk, tail_ref[:, cs], cw_ref[:, cs])
            tail_ref[:, cs] = blk[tt - SUBLANES:tt]
            return y * jax.nn.sigmoid(y)

        qs, ks, vs = conv_silu(0), conv_silu(1), conv_silu(2)
        for ch in range(n_ch):
            rs = slice(ch * c, (ch + 1) * c)
            for sub in range(2):
                h = 2 * p + sub
                ls = slice(sub * DN_HEAD_DIM, (sub + 1) * DN_HEAD_DIM)
                q_ = l2n(qs[rs, ls]) * (DN_HEAD_DIM ** -0.5)
                k_ = l2n(ks[rs, ls])
                beta = common["beta"][rs, h:h + 1]
                gc = common["gam"][ch][:, n_heads + h:n_heads + h + 1]
                eg_ = jnp.exp(gc)
                kb_ = k_ * beta
                i = pix[(ch, h)]
                staged["rhs"][i] = jnp.concatenate([vs[rs, ls] * beta, kb_ * eg_], axis=1).astype(BF16)
                staged["kbq"][i] = jnp.concatenate([kb_, q_], axis=0).astype(BF16)
                staged["kd"][i] = k_ * jnp.exp(gc[c - 1:c, :] - gc)
                staged["qeg"][i] = (q_ * eg_).astype(BF16)
                staged["k"][i] = k_.astype(BF16)

    fill = iter([
        lambda: project(ba_piece, 0, n_pair, 2 * n_pair),
        lambda: (prep_common(), project(1, n_pair + 1, 2 * n_pair + 1)),
        lambda: prep_pair(0),
        lambda: project(2, n_pair + 2, 2 * n_pair + 2),
        lambda: prep_pair(1),
        lambda: project(3, n_pair + 3, 2 * n_pair + 3),
        lambda: prep_pair(2),
        lambda: project(3 * n_pair, 3 * n_pair + 1),
        lambda: prep_pair(3),
        lambda: project(3 * n_pair + 2),
        lambda: project(3 * n_pair + 3),
    ])
    assert n_pair == 4

    def interleave():
        action = next(fill, None)
        if action is not None:
            action()

    rd = (step + 1) % 2
    wr = step % 2
    kbq = [kbq_s[rd, i] for i in range(n_prob)]
    k = [k_s[rd, i] for i in range(n_prob)]
    rhs = [rhs_s[rd, i] for i in range(n_prob)]
    kd = [kd_s[rd, i] for i in range(n_prob)]
    qeg = [qeg_s[rd, i] for i in range(n_prob)]
    gam_all = [gam_s[rd, ch] for ch in range(n_ch)]
    gam_t = [gamt_s[rd, ch] for ch in range(n_ch)]
    z = z_s[rd]
    gcol = [gam_all[ch][:, n_heads + h:n_heads + h + 1] for ch, h in probs]
    glast = [gc[c - 1:c, :] for gc in gcol]

    def both(lst, ch, p):
        return lst[pix[(ch, 2 * p)]], lst[pix[(ch, 2 * p + 1)]]

    interleave()
    kq = []
    for ch, p in pairs:
        (x0, x1), (k0, k1) = both(kbq, ch, p), both(k, ch, p)
        kq.append(_dot_nt(x0, jnp.concatenate([k0.astype(BF16), zero_k], axis=0))
                  + _dot_nt(x1, jnp.concatenate([zero_k, k1.astype(BF16)], axis=0)))
    interleave()
    decay = []
    for ch, p in pairs:
        g0, g1 = both(gcol, ch, p)
        gr = jnp.where(lo2[0:1], gam_t[ch][n_heads + 2 * p:n_heads + 2 * p + 1, :],
                       gam_t[ch][n_heads + 2 * p + 1:n_heads + 2 * p + 2, :])
        diff = jnp.where(lo2, g0, g1) - gr
        decay.append(jnp.where(incl, jnp.exp(jnp.where(incl, diff, 0.0)), 0.0))
    a2 = [jnp.where(strict, kq_[:c] * d_, 0.0) for kq_, d_ in zip(kq, decay)]
    qk2 = [jnp.where(incl, kq_[c:] * d_, 0.0) for kq_, d_ in zip(kq, decay)]

    a4 = [jnp.concatenate([a2[ch * n_pair + 2 * qd], a2[ch * n_pair + 2 * qd + 1]], axis=1) for ch, qd in quads]
    tinv = [eye4 - a_ for a_ in a4]
    pw = [_dot(a_.astype(BF16), block_diag(a_)) for a_ in a4]
    interleave()
    for _ in range(int(math.log2(c)) - 2):
        r = [_dot(jnp.concatenate([t_, p_], axis=0).astype(BF16), block_diag(p_)) for t_, p_ in zip(tinv, pw)]
        tinv = [t_ + r_[:c] for t_, r_ in zip(tinv, r)]
        pw = [r_[c:] for r_ in r]
        interleave()
    tinv = [t_ + _dot(t_.astype(BF16), block_diag(p_)) for t_, p_ in zip(tinv, pw)]
    interleave()

    def split_rows(m):
        return jnp.concatenate([jnp.where(lo2, m, 0.0), jnp.where(lo2, 0.0, m)], axis=0).astype(BF16)

    sol, lhs_o = [], []
    for i, (ch, p) in enumerate(pairs):
        t2 = tinv[i // 2][:, (i % 2) * LANES:(i % 2 + 1) * LANES]
        r0, r1 = both(rhs, ch, p)
        sol.append(_dot(split_rows(t2), jnp.concatenate([r0, r1], axis=0)))
        kd0, kd1 = both(kd, ch, p)
        kdt = jnp.concatenate([kd0, kd1], axis=0).T
        lhs_o.append(jnp.concatenate([split_rows(qk2[i]), jnp.where(lo2x, kdt, 0.0).astype(BF16),
                                      jnp.where(lo2x, 0.0, kdt).astype(BF16)], axis=0))
    wq = []
    for ch, h in probs:
        s_ = sol[ch * n_pair + h // 2][(h % 2) * c:(h % 2 + 1) * c]
        i = pix[(ch, h)]
        wq.append(jnp.concatenate([s_[:, DN_HEAD_DIM:].astype(BF16), qeg[i]], axis=0))
    interleave()

    for ch in range(n_ch):
        rs = slice(ch * c, (ch + 1) * c)
        state = [s_scr[h] for h in range(n_heads)]
        ws = [_dot(wq[pix[(ch, h)]], state[h].astype(BF16)) for h in range(n_heads)]
        interleave()
        un = [sol[ch * n_pair + h // 2][(h % 2) * c:(h % 2 + 1) * c, :DN_HEAD_DIM] - ws[h][:c] for h in range(n_heads)]
        ou = [_dot(lhs_o[ch * n_pair + p], jnp.concatenate([un[2 * p], un[2 * p + 1]], axis=0).astype(BF16))
              for p in range(n_pair)]
        interleave()
        for h in range(n_heads):
            hs = slice(h * DN_HEAD_DIM, (h + 1) * DN_HEAD_DIM)
            o2 = ou[h // 2]
            sub = h % 2
            s_scr[h] = (state[h] * jnp.exp(glast[pix[(ch, h)]])
                        + o2[2 * c + sub * DN_HEAD_DIM:2 * c + (sub + 1) * DN_HEAD_DIM])
            on = _rms(ws[h][c:] + o2[sub * c:(sub + 1) * c], on_ref[:, hs])
            zz = z[rs, hs]
            o_ref[rs, hs] = (on * (zz * jax.nn.sigmoid(zz))).astype(BF16)

    for action in fill:
        action()
    for ref, vals in ((kbq_s, staged["kbq"]), (k_s, staged["k"]), (rhs_s, staged["rhs"]), (kd_s, staged["kd"]),
                      (qeg_s, staged["qeg"]), (gam_s, common["gam"]), (gamt_s, common["gam_t"])):
        for i, v in enumerate(vals):
            ref[wr, i] = v
    z_s[wr] = jnp.concatenate([parts[3 * n_pair + j] for j in range(n_pair)], axis=1)


def _dn_branch(x, g, w_in, layer, w_block, cw, aexp, dtb, onorm, *, tt=256):
    bsz, t, d = x.shape
    width = onorm.shape[1]
    n_heads = width // DN_HEAD_DIM
    w_cols = 4 * width + 2 * LANES
    tiles_per_seq = t // tt
    n_tiles = bsz * tiles_per_seq
    n_prob = (tt // DN_CHUNK) * n_heads
    c = DN_CHUNK

    def x_map(s):
        j = jnp.minimum(s, n_tiles - 1)
        return j // tiles_per_seq, j % tiles_per_seq, 0

    def o_map(s):
        j = jnp.maximum(s - 1, 0)
        return j // tiles_per_seq, j % tiles_per_seq, 0

    assert tiles_per_seq > 1
    return pl.pallas_call(
        functools.partial(_dn_kernel, tiles_per_seq=tiles_per_seq),
        grid=(n_tiles + 1,),
        in_specs=[pl.BlockSpec((None, tt, d), x_map), _const_spec((1, d)),
                  _layer_spec(d, w_cols, layer, w_block), _const_spec(cw.shape),
                  _const_spec((1, LANES)), _const_spec((1, LANES)), _const_spec((1, width))],
        out_specs=pl.BlockSpec((None, tt, width), o_map),
        out_shape=jax.ShapeDtypeStruct((bsz, t, width), BF16),
        scratch_shapes=[pltpu.VMEM((SUBLANES, 3 * width), F32),
                        pltpu.VMEM((n_heads, DN_HEAD_DIM, DN_HEAD_DIM), F32),
                        pltpu.VMEM((2, n_prob, 2 * c, DN_HEAD_DIM), BF16),
                        pltpu.VMEM((2, n_prob, c, DN_HEAD_DIM), BF16),
                        pltpu.VMEM((2, n_prob, c, 2 * DN_HEAD_DIM), BF16),
                        pltpu.VMEM((2, n_prob, c, DN_HEAD_DIM), F32),
                        pltpu.VMEM((2, n_prob, c, DN_HEAD_DIM), BF16),
                        pltpu.VMEM((2, tt // c, c, LANES), F32),
                        pltpu.VMEM((2, tt // c, LANES, LANES), F32),
                        pltpu.VMEM((2, tt, width), F32)],
        compiler_params=_params(("arbitrary",)),
        name="dn_branch",
    )(x, g, w_in, cw, aexp, dtb, onorm)


def _merge_kernel(x_ref, g_ref, ya_ref, yb_ref, yc_ref, wm_ref, wa_ref, wb_ref, wc_ref, wo_ref, o_ref):
    x = x_ref[...]
    d = x.shape[1]
    u = _rms(x, g_ref[...]).astype(BF16)
    y = jnp.zeros_like(x)
    for j, (yr, wr) in enumerate(((ya_ref, wa_ref), (yb_ref, wb_ref), (yc_ref, wc_ref))):
        gate = jax.nn.sigmoid(_dot(u, wm_ref[:, j * d:(j + 1) * d]))
        y = y + gate * _dot(yr[...], wr[...])
    o_ref[...] = x + _dot(y.astype(BF16), wo_ref[...])


def _merge(x2, g, ya, yb, yc, wm, wa, wb, wc, wo, *, tm=512):
    n, d = x2.shape
    row = lambda i: (i, 0)
    return pl.pallas_call(
        _merge_kernel,
        grid=(n // tm,),
        in_specs=[pl.BlockSpec((tm, d), row), _const_spec((1, d)),
                  pl.BlockSpec((tm, ya.shape[1]), row), pl.BlockSpec((tm, yb.shape[1]), row),
                  pl.BlockSpec((tm, yc.shape[1]), row),
                  _const_spec(wm.shape), _const_spec(wa.shape), _const_spec(wb.shape), _const_spec(wc.shape),
                  _const_spec(wo.shape)],
        out_specs=pl.BlockSpec((tm, d), row),
        out_shape=jax.ShapeDtypeStruct((n, d), F32),
        compiler_params=_params(("arbitrary",)),
        name="merge",
    )(x2, g, ya, yb, yc, wm, wa, wb, wc, wo)


def _block_diag_tiles(w):
    nb, k, _ = w.shape
    per = MXU_DIM // k
    w4 = w.reshape(nb // per, per, k, k)
    eye = jnp.eye(per, dtype=w.dtype)
    return jnp.einsum('tpij,pq->tpiqj', w4, eye).reshape(nb // per, MXU_DIM, MXU_DIM)


def _row(v):
    return v.reshape(1, -1).astype(F32)


def _mixer(x, layer, mix_norm, w_in, rg_conv_w, rg_conv_b, rg_w_r, rg_b_r, rg_w_i, rg_b_i, rg_lambda,
           att_q_norm, att_k_norm, dn_conv_w, dn_a_log, dn_dt_bias, dn_out_norm, w_branch, w_out):
    bsz, t, d = x.shape
    rg_w = rg_conv_w.shape[1]
    att_w = att_q_norm.size
    dn_w = dn_out_norm.size
    dn_h = dn_out_norm.shape[0]
    att_off = 2 * rg_w
    dn_off = att_off + 3 * att_w
    merge_off = dn_off + 4 * dn_w + 2 * dn_h
    dn_cols = 4 * dn_w + 2 * LANES
    assert dn_off % dn_cols == 0 and dn_off + dn_cols <= w_in.shape[2] and 2 * dn_h <= LANES
    w_att = w_in[layer, :, att_off:dn_off]
    w_merge = w_in[layer, :, merge_off:merge_off + 3 * d]
    g = _row(mix_norm)

    lamc = _row(-RG_C * jax.nn.softplus(-rg_lambda.astype(F32)))
    ya = _rg_branch(x, g, w_in, layer, rg_conv_w, _row(rg_conv_b),
                    _block_diag_tiles(rg_w_r).astype(BF16), _row(rg_b_r),
                    _block_diag_tiles(rg_w_i).astype(BF16), _row(rg_b_i), lamc)

    n_att_heads = att_q_norm.shape[0]
    yb = _att_branch(x, g, w_att, _row(att_q_norm), _row(att_k_norm),
                     _head_mean_matrix(2 * LANES, ATT_HEAD_DIM), _att_maskbias(n_att_heads))

    pad = jnp.zeros((LANES - 2 * dn_h,), F32)
    aexp = _row(jnp.concatenate([jnp.zeros((dn_h,), F32), jnp.exp(dn_a_log.astype(F32)), pad]))
    dtb = _row(jnp.concatenate([jnp.zeros((dn_h,), F32), dn_dt_bias.astype(F32), pad]))
    yc = _dn_branch(x, g, w_in, layer, dn_off // dn_cols, dn_conv_w, aexp, dtb, _row(dn_out_norm))

    wa = w_branch[layer, :rg_w]
    wb = w_branch[layer, rg_w:rg_w + att_w]
    wc = w_branch[layer, rg_w + att_w:]
    n = bsz * t
    out = _merge(x.reshape(n, d), g, ya.reshape(n, rg_w), yb.reshape(n, att_w), yc.reshape(n, dn_w),
                 w_merge, wa, wb, wc, w_out[layer])
    return out.reshape(bsz, t, d)


def kernel(x, ffn1_norm, ffn1_w_gate, ffn1_w_up, ffn1_w_down, mix_norm, w_in, rg_conv_w, rg_conv_b, rg_w_r, rg_b_r, rg_w_i, rg_b_i, rg_lambda, att_q_norm, att_k_norm, dn_conv_w, dn_a_log, dn_dt_bias, dn_out_norm, w_branch, w_out, ffn2_norm, ffn2_w_gate, ffn2_w_up, ffn2_w_down):
    bsz, t, d = x.shape
    n = bsz * t
    f1g, f1u, f1d = ffn1_w_gate.astype(BF16), ffn1_w_up.astype(BF16), ffn1_w_down.astype(BF16)
    f2g, f2u, f2d = ffn2_w_gate.astype(BF16), ffn2_w_up.astype(BF16), ffn2_w_down.astype(BF16)
    w_in_b, w_branch_b, w_out_b = w_in.astype(BF16), w_branch.astype(BF16), w_out.astype(BF16)
    for l in range(ffn1_norm.shape[0]):
        x = _ffn(x.reshape(n, d), _row(ffn1_norm[l]), f1g, f1u, f1d, l).reshape(bsz, t, d)
        x = _mixer(x, l, mix_norm[l], w_in_b, rg_conv_w[l], rg_conv_b[l], rg_w_r[l], rg_b_r[l], rg_w_i[l],
                   rg_b_i[l], rg_lambda[l], att_q_norm[l], att_k_norm[l], dn_conv_w[l], dn_a_log[l],
                   dn_dt_bias[l], dn_out_norm[l], w_branch_b, w_out_b)
        x = _ffn(x.reshape(n, d), _row(ffn2_norm[l]), f2g, f2u, f2d, l).reshape(bsz, t, d)
    return x
```

```python
import functools
import math

import jax
import jax.numpy as jnp
from jax import lax
from jax.experimental import pallas as pl
from jax.experimental.pallas import tpu as pltpu

F32 = jnp.float32
BF16 = jnp.bfloat16

EPS = 1e-6
NEG_INF = -1e30
RG_C = 8.0
RG_BLOCK_DIM = 64
CONV_WIDTH = 4
ATT_GROUPS = ((128, 1), (512, 4), (2048, 16))
ATT_HEADS_PER_GROUP = 4
ATT_HEAD_DIM = 64
ATT_SPAN = 128
DN_HEAD_DIM = 128
DN_CHUNK = 64

LANES = 128
SUBLANES = 8
MXU_DIM = 256
VMEM_LIMIT_BYTES = 56 * 1024 * 1024


def _rms(x, g):
    return x * lax.rsqrt(jnp.mean(x * x, axis=-1, keepdims=True) + EPS) * g


def _dot(a, b):
    return jnp.dot(a, b, preferred_element_type=F32)


def _dot_nt(a, b):
    return lax.dot_general(a, b, (((1,), (1,)), ((), ())), preferred_element_type=F32)


def _dot_tn(a, b):
    return lax.dot_general(a, b, (((0,), (0,)), ((), ())), preferred_element_type=F32)


def _softplus(z):
    return jnp.maximum(z, 0.0) + jnp.log(1.0 + jnp.exp(-jnp.abs(z)))


def _const_spec(shape):
    nd = len(shape)
    return pl.BlockSpec(shape, lambda *_: (0,) * nd)


def _col_block_spec(rows, cols, col_block):
    return pl.BlockSpec((rows, cols), lambda *_: (0, col_block))


def _params(semantics):
    return pltpu.CompilerParams(dimension_semantics=semantics, vmem_limit_bytes=VMEM_LIMIT_BYTES)


BF16_SUBLANES = 16


def _cast_blocking(rows, n_steps):
    rpb = BF16_SUBLANES
    while rows % rpb or rows // rpb > n_steps:
        rpb += BF16_SUBLANES
        assert rpb <= rows
    return rpb, rows // rpb


def _ffn_kernel(x_ref, g_ref, wg_ref, wu_ref, wd_ref, *rest, f_chunk):
    n_cast = (len(rest) - 1) // 2
    o_ref = rest[n_cast]
    for src, dst in zip(rest[:n_cast], rest[n_cast + 1:]):
        dst[...] = src[...].astype(BF16)
    x = x_ref[...]
    u = _rms(x, g_ref[...]).astype(BF16)
    acc = jnp.zeros_like(x)
    for c in range(wg_ref.shape[1] // f_chunk):
        sl = slice(c * f_chunk, (c + 1) * f_chunk)
        hg = _dot(u, wg_ref[:, sl])
        hu = _dot(u, wu_ref[:, sl])
        h = (hg * jax.nn.sigmoid(hg) * hu).astype(BF16)
        acc = acc + _dot(h, wd_ref[sl, :])
    o_ref[...] = x + 0.5 * acc


def _ffn(x2, g, wg, wu, wd, casts=(), *, tm=512, f_chunk=256):
    n, d = x2.shape
    f = wg.shape[1]
    n_steps = n // tm
    cast_in, cast_out, cast_shapes = [], [], []
    for stack, layer in casts:
        _, rows, cols = stack.shape
        rpb, nb = _cast_blocking(rows, n_steps)
        cast_in.append(pl.BlockSpec((None, rpb, cols), lambda i, layer=layer, nb=nb: (layer, jnp.minimum(i, nb - 1), 0)))
        cast_out.append(pl.BlockSpec((rpb, cols), lambda i, nb=nb: (jnp.minimum(i, nb - 1), 0)))
        cast_shapes.append(jax.ShapeDtypeStruct((rows, cols), BF16))
    outs = pl.pallas_call(
        functools.partial(_ffn_kernel, f_chunk=f_chunk),
        grid=(n_steps,),
        in_specs=[pl.BlockSpec((tm, d), lambda i: (i, 0)), _const_spec((1, d)),
                  _const_spec((d, f)), _const_spec((d, f)), _const_spec((f, d))] + cast_in,
        out_specs=[pl.BlockSpec((tm, d), lambda i: (i, 0))] + cast_out,
        out_shape=[jax.ShapeDtypeStruct((n, d), F32)] + cast_shapes,
        compiler_params=_params(("arbitrary",)),
        name="ffn",
    )(x2, g, wg, wu, wd, *[stack for stack, _ in casts])
    return outs[0], list(outs[1:])


def _shift_rows(cur, prev8, s):
    rows = lax.broadcasted_iota(jnp.int32, cur.shape, 0)
    return jnp.where(rows < s, pltpu.roll(prev8, s, 0), pltpu.roll(cur, s, 0))


def _causal_conv(p, tail8, w):
    kw = w.shape[0]
    acc = p * w[kw - 1:kw, :]
    top = p[0:SUBLANES] * w[kw - 1:kw, :]
    for s in range(1, kw):
        wk = w[kw - 1 - s:kw - s, :]
        acc = acc + pltpu.roll(p, s, 0) * wk
        top = top + _shift_rows(p[0:SUBLANES], tail8, s) * wk
    return jnp.concatenate([top, acc[SUBLANES:]], axis=0)


def _segment_perm(tt):
    r = jnp.arange(tt)
    src = (r % SUBLANES) * (tt // SUBLANES) + r // SUBLANES
    return (src[:, None] == jnp.arange(tt)[None, :]).astype(BF16)


def _rg_kernel(x_ref, g_ref, perm_ref, permt_ref, wx_ref, wgate_ref, cw_ref, cb_ref, wr_ref, br_ref, wi_ref,
               bi_ref, lamc_ref, o_ref, tail_ref, carry_ref, a_scr, h_scr):
    tt, width = o_ref.shape
    n_grp = tt // SUBLANES
    kw = cw_ref.shape[0]

    @pl.when(pl.program_id(1) == 0)
    def _():
        tail_ref[...] = jnp.zeros_like(tail_ref)
        carry_ref[...] = jnp.zeros_like(carry_ref)

    u = _rms(x_ref[...], g_ref[...]).astype(BF16)
    up = _dot(perm_ref[...], u).astype(BF16)
    nc = width // MXU_DIM
    rows8 = lax.broadcasted_iota(jnp.int32, (SUBLANES, MXU_DIM), 0)

    def project(c):
        cs = slice(c * MXU_DIM, (c + 1) * MXU_DIM)
        return _dot(up, wx_ref[:, cs]), _dot(up, wgate_ref[:, cs])

    def conv(c, px):
        cs = slice(c * MXU_DIM, (c + 1) * MXU_DIM)
        tail = tail_ref[:, cs]
        head = [_shift_rows(px[(n_grp - k) * SUBLANES:(n_grp - k + 1) * SUBLANES],
                            tail[(kw - 1 - k) * SUBLANES:(kw - k) * SUBLANES], 1) for k in range(kw - 1, 0, -1)]
        ext = jnp.concatenate(head + [px], axis=0)
        tail_ref[:, cs] = px[tt - (kw - 1) * SUBLANES:tt]
        xa = cb_ref[:, cs] + ext[0:tt] * cw_ref[0:1, cs]
        for k in range(1, kw):
            xa = xa + ext[k * SUBLANES:k * SUBLANES + tt] * cw_ref[k:k + 1, cs]
        return xa

    def gates(c, xa):
        xab = xa.astype(BF16)
        return _dot(xab, wr_ref[c]), _dot(xab, wi_ref[c])

    def scan(c, xa, r_lin, i_lin, pg):
        cs = slice(c * MXU_DIM, (c + 1) * MXU_DIM)
        a = jnp.exp(lamc_ref[:, cs] * jax.nn.sigmoid(r_lin + br_ref[:, cs]))
        a_scr[:, cs] = a
        h_scr[:, cs] = jnp.sqrt(1.0 - a * a) * (jax.nn.sigmoid(i_lin + bi_ref[:, cs]) * xa)
        hl = jnp.zeros((SUBLANES, MXU_DIM), F32)
        ap = jnp.ones((SUBLANES, MXU_DIM), F32)
        for j in range(n_grp):
            rs = slice(j * SUBLANES, (j + 1) * SUBLANES)
            aj = a_scr[rs, cs]
            hl = aj * hl + h_scr[rs, cs]
            ap = aj * ap
            h_scr[rs, cs] = hl
            a_scr[rs, cs] = ap
        for s in (1, 2, 4):
            m = rows8 >= s
            hl = jnp.where(m, ap * pltpu.roll(hl, s, 0) + hl, hl)
            ap = jnp.where(m, ap * pltpu.roll(ap, s, 0), ap)
        cin = carry_ref[:, cs]
        seg_end = hl + ap * cin
        carry_ref[:, cs] = jnp.broadcast_to(seg_end[SUBLANES - 1:SUBLANES, :], seg_end.shape)
        seg_in = jnp.where(rows8 < 1, cin, pltpu.roll(seg_end, 1, 0))
        h = (h_scr[:, cs].reshape(n_grp, SUBLANES, MXU_DIM)
             + a_scr[:, cs].reshape(n_grp, SUBLANES, MXU_DIM) * seg_in[None])
        return (h.reshape(tt, MXU_DIM) * jax.nn.gelu(pg)).astype(BF16)

    def emit(c, y):
        o_ref[:, c * MXU_DIM:(c + 1) * MXU_DIM] = _dot(permt_ref[...], y).astype(BF16)

    proj, xas, gts, ys = {}, {}, {}, {}
    for step in range(nc + 3):
        if step < nc:
            proj[step] = project(step)
        c = step - 1
        if 0 <= c < nc:
            xas[c] = conv(c, proj[c][0])
            gts[c] = gates(c, xas[c])
        c = step - 2
        if 0 <= c < nc:
            ys[c] = scan(c, xas[c], gts[c][0], gts[c][1], proj[c][1])
        c = step - 3
        if 0 <= c < nc:
            emit(c, ys[c])


def _rg_branch(x, g, w_in, cw, cb, wr_t, br, wi_t, bi, lamc, *, tt=256):
    bsz, t, d = x.shape
    width = cw.shape[1]
    perm = _segment_perm(tt)
    return pl.pallas_call(
        _rg_kernel,
        grid=(bsz, t // tt),
        in_specs=[pl.BlockSpec((None, tt, d), lambda b, i: (b, i, 0)), _const_spec((1, d)),
                  _const_spec((tt, tt)), _const_spec((tt, tt)),
                  _col_block_spec(d, width, 0), _col_block_spec(d, width, 1),
                  _const_spec(cw.shape), _const_spec((1, width)),
                  _const_spec(wr_t.shape), _const_spec((1, width)),
                  _const_spec(wi_t.shape), _const_spec((1, width)), _const_spec((1, width))],
        out_specs=pl.BlockSpec((None, tt, width), lambda b, i: (b, i, 0)),
        out_shape=jax.ShapeDtypeStruct((bsz, t, width), BF16),
        scratch_shapes=[pltpu.VMEM(((cw.shape[0] - 1) * SUBLANES, width), F32), pltpu.VMEM((SUBLANES, width), F32),
                        pltpu.VMEM((tt, width), F32), pltpu.VMEM((tt, width), F32)],
        compiler_params=_params(("arbitrary", "arbitrary")),
        name="rg_branch",
    )(x, g, perm, perm.T, w_in, w_in, cw, cb, wr_t, br, wi_t, bi, lamc)


ATT_BLOCK_BATCH = 4


def _att_kernel(x_ref, g_ref, w_ref, gq_ref, gk_ref, e4_ref, mb_ref, o_ref, qkv_scr, o_scr, lse_scr, *, rt):
    t_total = o_ref.shape[0]
    n_slab = o_scr.shape[0]
    i = pl.program_id(1)
    row0 = pl.multiple_of(i * rt, rt)
    span = ATT_SPAN

    u = _rms(x_ref[...], g_ref[...]).astype(BF16)
    qkv = _dot(u, w_ref[...])
    width = n_slab * LANES
    e4 = e4_ref[...]
    for s in range(n_slab):
        cs = slice(s * LANES, (s + 1) * LANES)
        qk = jnp.concatenate([qkv[:, cs], qkv[:, width + s * LANES: width + (s + 1) * LANES]], axis=1)
        gain = jnp.concatenate([gq_ref[:, cs] * (ATT_HEAD_DIM ** -0.5), gk_ref[:, cs]], axis=1)
        ms = _dot((qk * qk).astype(BF16), e4)
        qk = qk * lax.rsqrt(ms + EPS) * gain
        qkv_scr[s, pl.ds(row0, rt), :] = qk[:, :LANES]
        qkv_scr[n_slab + s, pl.ds(row0, rt), :] = qk[:, LANES:]
        qkv_scr[2 * n_slab + s, pl.ds(row0, rt), :] = qkv[:, 2 * width + s * LANES: 2 * width + (s + 1) * LANES]

    @pl.when(i == pl.num_programs(1) - 1)
    def _():
        lane = lax.broadcasted_iota(jnp.int32, (span, LANES), 1)
        first_head = lane < ATT_HEAD_DIM

        for g, (window, dil) in enumerate(ATT_GROUPS):
            assert window // dil == span
            nblk = t_total // dil // span
            blocks = [(r, n) for n in range(nblk) for r in range(dil)]

            def rows(start, dil=dil):
                return pl.ds(start, span) if dil == 1 else pl.ds(start, span, stride=dil)

            for b0 in range(0, len(blocks), ATT_BLOCK_BATCH):
                probs = [(r + n * span * dil, n > 0, 2 * g + sl)
                         for r, n in blocks[b0:b0 + ATT_BLOCK_BATCH] for sl in range(2)]
                qs, kk, vv = [], [], []
                for start, has_prev, slab in probs:
                    q2 = qkv_scr[slab, rows(start), :]
                    qs.append(jnp.concatenate([jnp.where(first_head, q2, 0.0), jnp.where(first_head, 0.0, q2)],
                                              axis=0).astype(BF16))
                    kc = qkv_scr[n_slab + slab, rows(start), :]
                    vc = qkv_scr[2 * n_slab + slab, rows(start), :]
                    if has_prev:
                        prev = start - span * dil
                        kc = jnp.concatenate([qkv_scr[n_slab + slab, rows(prev), :], kc], axis=0)
                        vc = jnp.concatenate([qkv_scr[2 * n_slab + slab, rows(prev), :], vc], axis=0)
                    kk.append(kc.astype(BF16))
                    vv.append(vc.astype(BF16))
                sc = [_dot_nt(q_, k_) for q_, k_ in zip(qs, kk)]
                sc = [s_ + (mb_ref[slab] if has_prev else mb_ref[slab, :, span:])
                      for s_, (_, has_prev, slab) in zip(sc, probs)]
                mx = [jnp.max(s_, axis=-1, keepdims=True) for s_ in sc]
                pr = [jnp.exp(s_ - m_) for s_, m_ in zip(sc, mx)]
                den = [jnp.sum(p_, axis=-1, keepdims=True) for p_ in pr]
                pv = [_dot(p_.astype(BF16), v_) for p_, v_ in zip(pr, vv)]
                for (start, _, slab), pv_, m_, d_ in zip(probs, pv, mx, den):
                    on = pv_ * (1.0 / d_)
                    lse = jnp.broadcast_to(m_ + jnp.log(d_), on.shape)
                    o_scr[slab, rows(start), :] = jnp.where(first_head, on[:span], on[span:])
                    lse_scr[slab, rows(start), :] = jnp.where(first_head, lse[:span], lse[span:])

        n_grp = len(ATT_GROUPS)
        ct = 256

        def combine(j, carry):
            rws = pl.ds(pl.multiple_of(j * ct, ct), ct)
            for sl in range(2):
                ls = [lse_scr[2 * g + sl, rws, :] for g in range(n_grp)]
                mx = functools.reduce(jnp.maximum, ls)
                es = [jnp.exp(l - mx) for l in ls]
                inv = 1.0 / functools.reduce(lambda a, b: a + b, es)
                for g in range(n_grp):
                    slab = 2 * g + sl
                    o_ref[rws, slab * LANES:(slab + 1) * LANES] = (o_scr[slab, rws, :] * (es[g] * inv)).astype(BF16)
            return carry

        lax.fori_loop(0, t_total // ct, combine, 0)


def _att_branch(x, g, w_qkv, gq, gk, e4, maskbias, *, rt=256):
    bsz, t, d = x.shape
    width = gq.shape[1]
    n_slab = width // LANES
    return pl.pallas_call(
        functools.partial(_att_kernel, rt=rt),
        grid=(bsz, t // rt),
        in_specs=[pl.BlockSpec((None, rt, d), lambda b, i: (b, i, 0)), _const_spec((1, d)),
                  _const_spec(w_qkv.shape), _const_spec((1, width)), _const_spec((1, width)),
                  _const_spec(e4.shape), _const_spec(maskbias.shape)],
        out_specs=pl.BlockSpec((None, t, width), lambda b, i: (b, 0, 0)),
        out_shape=jax.ShapeDtypeStruct((bsz, t, width), BF16),
        scratch_shapes=[pltpu.VMEM((3 * n_slab, t, LANES), F32), pltpu.VMEM((n_slab, t, LANES), F32),
                        pltpu.VMEM((n_slab, t, LANES), F32)],
        compiler_params=_params(("arbitrary", "arbitrary")),
        name="att_branch",
    )(x, g, w_qkv, gq, gk, e4, maskbias)


def _att_maskbias(n_heads):
    slopes = jnp.exp2(-8.0 * jnp.arange(1, n_heads + 1, dtype=F32) / n_heads)
    qi = jnp.arange(ATT_SPAN)[:, None]
    kj = jnp.arange(2 * ATT_SPAN)[None, :]
    delta = qi + ATT_SPAN - kj
    valid = (delta >= 0) & (delta <= ATT_SPAN)
    dil = jnp.repeat(jnp.array([d for _, d in ATT_GROUPS], F32), ATT_HEADS_PER_GROUP)
    bias = -(slopes * dil)[:, None, None] * delta.astype(F32)[None]
    return jnp.where(valid[None], bias, NEG_INF).reshape(n_heads // 2, 2 * ATT_SPAN, 2 * ATT_SPAN)


def _head_mean_matrix(n_lanes, head_dim):
    lane_head = jnp.arange(n_lanes) // head_dim
    return ((lane_head[:, None] == lane_head[None, :]).astype(F32) / head_dim).astype(BF16)


def _cumsum_rows(x):
    n = x.shape[0]
    rows = lax.broadcasted_iota(jnp.int32, x.shape, 0)
    s = 1
    while s < n:
        x = x + jnp.where(rows >= s, pltpu.roll(x, s, 0), 0.0)
        s *= 2
    return x


def _dn_kernel(x_ref, g_ref, w_ref, cw_ref, aexp_ref, dtb_ref, on_ref, o_ref, tail_ref, s_scr,
               kbq_s, k_s, rhs_s, kd_s, qeg_s, gam_s, gamt_s, z_s, *, tiles_per_seq):
    tt, width = o_ref.shape
    n_heads = width // DN_HEAD_DIM
    c = DN_CHUNK
    step = pl.program_id(0)
    stage_scratch = (kbq_s, k_s, rhs_s, kd_s, qeg_s, gam_s, gamt_s, z_s)

    @pl.when(step == 0)
    def _():
        for ref in stage_scratch:
            ref[...] = jnp.zeros_like(ref)

    @pl.when(step % tiles_per_seq == 0)
    def _():
        tail_ref[...] = jnp.zeros_like(tail_ref)

    @pl.when((step % tiles_per_seq == 1) | (step == 0))
    def _():
        s_scr[...] = jnp.zeros_like(s_scr)

    u = _rms(x_ref[...], g_ref[...]).astype(BF16)
    parts = {}

    def project(*idx):
        for j in idx:
            parts[j] = _dot(u, w_ref[:, j * MXU_DIM:(j + 1) * MXU_DIM])

    assert 2 * c == LANES and n_heads % 4 == 0
    n_ch = tt // c
    n_pair = n_heads // 2
    row2 = lax.broadcasted_iota(jnp.int32, (c, LANES), 0)
    lane2 = lax.broadcasted_iota(jnp.int32, (c, LANES), 1)
    col2 = lane2 & (c - 1)
    incl = col2 <= row2
    strict = col2 < row2
    lo2 = lane2 < c
    lane2x = lax.broadcasted_iota(jnp.int32, (LANES, LANES), 1)
    lo2x = lane2x < c
    row4 = lax.broadcasted_iota(jnp.int32, (c, 2 * LANES), 0)
    lane4 = lax.broadcasted_iota(jnp.int32, (c, 2 * LANES), 1)
    eye4 = ((lane4 & (c - 1)) == row4).astype(F32)
    blk4 = lane4 // c
    zero_k = jnp.zeros((c, DN_HEAD_DIM), BF16)

    def block_diag(m):
        return jnp.concatenate([jnp.where(blk4 == i, m, 0.0) for i in range(4)], axis=0).astype(BF16)

    def cols(base, h):
        return slice(base * width + h * DN_HEAD_DIM, base * width + (h + 1) * DN_HEAD_DIM)

    def l2n(m):
        return m * lax.rsqrt(jnp.sum(m * m, axis=-1, keepdims=True) + EPS)

    probs = [(ch, h) for ch in range(n_ch) for h in range(n_heads)]
    pairs = [(ch, p) for ch in range(n_ch) for p in range(n_pair)]
    quads = [(ch, qd) for ch in range(n_ch) for qd in range(n_pair // 2)]
    pix = {ph: i for i, ph in enumerate(probs)}

    n_prob = len(probs)
    staged = {name: [None] * n_prob for name in ("kbq", "k", "rhs", "kd", "qeg")}
    common = {}
    ba_piece = 4 * n_pair

    def prep_common():
        ba = parts[ba_piece][:, :LANES]
        common["beta"] = jax.nn.sigmoid(ba)
        glog_all = -aexp_ref[...] * _softplus(ba + dtb_ref[...])
        common["gam"] = [_cumsum_rows(glog_all[ch * c:(ch + 1) * c]) for ch in range(n_ch)]
        common["gam_t"] = [jnp.concatenate([g_, g_], axis=0).T for g_ in common["gam"]]

    def prep_pair(p):
        def conv_silu(section):
            blk = parts[section * n_pair + p]
            cs = slice(section * width + p * MXU_DIM, section * width + (p + 1) * MXU_DIM)
            y = _causal_conv(blk, tail_ref[:, cs], cw_ref[:, cs])
            tail_ref[:, cs] = blk[tt - SUBLANES:tt]
            return y * jax.nn.sigmoid(y)

        qs, ks, vs = conv_silu(0), conv_silu(1), conv_silu(2)
        for ch in range(n_ch):
            rs = slice(ch * c, (ch + 1) * c)
            for sub in range(2):
                h = 2 * p + sub
                ls = slice(sub * DN_HEAD_DIM, (sub + 1) * DN_HEAD_DIM)
                q_ = l2n(qs[rs, ls]) * (DN_HEAD_DIM ** -0.5)
                k_ = l2n(ks[rs, ls])
                beta = common["beta"][rs, h:h + 1]
                gc = common["gam"][ch][:, n_heads + h:n_heads + h + 1]
                eg_ = jnp.exp(gc)
                kb_ = k_ * beta
                i = pix[(ch, h)]
                staged["rhs"][i] = jnp.concatenate([vs[rs, ls] * beta, kb_ * eg_], axis=1).astype(BF16)
                staged["kbq"][i] = jnp.concatenate([kb_, q_], axis=0).astype(BF16)
                staged["kd"][i] = k_ * jnp.exp(gc[c - 1:c, :] - gc)
                staged["qeg"][i] = (q_ * eg_).astype(BF16)
                staged["k"][i] = k_.astype(BF16)

    fill = iter([
        lambda: project(ba_piece, 0, n_pair, 2 * n_pair),
        lambda: (prep_common(), project(1, n_pair + 1, 2 * n_pair + 1)),
        lambda: prep_pair(0),
        lambda: project(2, n_pair + 2, 2 * n_pair + 2),
        lambda: prep_pair(1),
        lambda: project(3, n_pair + 3, 2 * n_pair + 3),
        lambda: prep_pair(2),
        lambda: project(3 * n_pair, 3 * n_pair + 1),
        lambda: prep_pair(3),
        lambda: project(3 * n_pair + 2),
        lambda: project(3 * n_pair + 3),
    ])
    assert n_pair == 4

    def interleave():
        action = next(fill, None)
        if action is not None:
            action()

    rd = (step + 1) % 2
    wr = step % 2
    kbq = [kbq_s[rd, i] for i in range(n_prob)]
    k = [k_s[rd, i] for i in range(n_prob)]
    rhs = [rhs_s[rd, i] for i in range(n_prob)]
    kd = [kd_s[rd, i] for i in range(n_prob)]
    qeg = [qeg_s[rd, i] for i in range(n_prob)]
    gam_all = [gam_s[rd, ch] for ch in range(n_ch)]
    gam_t = [gamt_s[rd, ch] for ch in range(n_ch)]
    z = z_s[rd]
    gcol = [gam_all[ch][:, n_heads + h:n_heads + h + 1] for ch, h in probs]
    glast = [gc[c - 1:c, :] for gc in gcol]

    def both(lst, ch, p):
        return lst[pix[(ch, 2 * p)]], lst[pix[(ch, 2 * p + 1)]]

    interleave()
    kq = []
    for ch, p in pairs:
        (x0, x1), (k0, k1) = both(kbq, ch, p), both(k, ch, p)
        kq.append(_dot_nt(x0, jnp.concatenate([k0.astype(BF16), zero_k], axis=0))
                  + _dot_nt(x1, jnp.concatenate([zero_k, k1.astype(BF16)], axis=0)))
    interleave()
    decay = []
    for ch, p in pairs:
        g0, g1 = both(gcol, ch, p)
        gr = jnp.where(lo2[0:1], gam_t[ch][n_heads + 2 * p:n_heads + 2 * p + 1, :],
                       gam_t[ch][n_heads + 2 * p + 1:n_heads + 2 * p + 2, :])
        diff = jnp.where(lo2, g0, g1) - gr
        decay.append(jnp.where(incl, jnp.exp(jnp.where(incl, diff, 0.0)), 0.0))
    a2 = [jnp.where(strict, kq_[:c] * d_, 0.0) for kq_, d_ in zip(kq, decay)]
    qk2 = [jnp.where(incl, kq_[c:] * d_, 0.0) for kq_, d_ in zip(kq, decay)]

    a4 = [jnp.concatenate([a2[ch * n_pair + 2 * qd], a2[ch * n_pair + 2 * qd + 1]], axis=1) for ch, qd in quads]
    tinv = [eye4 - a_ for a_ in a4]
    pw = [_dot(a_.astype(BF16), block_diag(a_)) for a_ in a4]
    interleave()
    for _ in range(int(math.log2(c)) - 2):
        r = [_dot(jnp.concatenate([t_, p_], axis=0).astype(BF16), block_diag(p_)) for t_, p_ in zip(tinv, pw)]
        tinv = [t_ + r_[:c] for t_, r_ in zip(tinv, r)]
        pw = [r_[c:] for r_ in r]
        interleave()
    tinv = [t_ + _dot(t_.astype(BF16), block_diag(p_)) for t_, p_ in zip(tinv, pw)]
    interleave()

    def split_rows(m):
        return jnp.concatenate([jnp.where(lo2, m, 0.0), jnp.where(lo2, 0.0, m)], axis=0).astype(BF16)

    sol, lhs_o = [], []
    for i, (ch, p) in enumerate(pairs):
        t2 = tinv[i // 2][:, (i % 2) * LANES:(i % 2 + 1) * LANES]
        r0, r1 = both(rhs, ch, p)
        sol.append(_dot(split_rows(t2), jnp.concatenate([r0, r1], axis=0)))
        kd0, kd1 = both(kd, ch, p)
        kdt = jnp.concatenate([kd0, kd1], axis=0).T
        lhs_o.append(jnp.concatenate([split_rows(qk2[i]), jnp.where(lo2x, kdt, 0.0).astype(BF16),
                                      jnp.where(lo2x, 0.0, kdt).astype(BF16)], axis=0))
    wq = []
    for ch, h in probs:
        s_ = sol[ch * n_pair + h // 2][(h % 2) * c:(h % 2 + 1) * c]
        i = pix[(ch, h)]
        wq.append(jnp.concatenate([s_[:, DN_HEAD_DIM:].astype(BF16), qeg[i]], axis=0))
    interleave()

    for ch in range(n_ch):
        rs = slice(ch * c, (ch + 1) * c)
        state = [s_scr[h] for h in range(n_heads)]
        ws = [_dot(wq[pix[(ch, h)]], state[h].astype(BF16)) for h in range(n_heads)]
        interleave()
        un = [sol[ch * n_pair + h // 2][(h % 2) * c:(h % 2 + 1) * c, :DN_HEAD_DIM] - ws[h][:c] for h in range(n_heads)]
        ou = [_dot(lhs_o[ch * n_pair + p], jnp.concatenate([un[2 * p], un[2 * p + 1]], axis=0).astype(BF16))
              for p in range(n_pair)]
        interleave()
        for h in range(n_heads):
            hs = slice(h * DN_HEAD_DIM, (h + 1) * DN_HEAD_DIM)
            o2 = ou[h // 2]
            sub = h % 2
            s_scr[h] = (state[h] * jnp.exp(glast[pix[(ch, h)]])
                        + o2[2 * c + sub * DN_HEAD_DIM:2 * c + (sub + 1) * DN_HEAD_DIM])
            on = _rms(ws[h][c:] + o2[sub * c:(sub + 1) * c], on_ref[:, hs])
            zz = z[rs, hs]
            o_ref[rs, hs] = (on * (zz * jax.nn.sigmoid(zz))).astype(BF16)

    for action in fill:
        action()
    for ref, vals in ((kbq_s, staged["kbq"]), (k_s, staged["k"]), (rhs_s, staged["rhs"]), (kd_s, staged["kd"]),
                      (qeg_s, staged["qeg"]), (gam_s, common["gam"]), (gamt_s, common["gam_t"])):
        for i, v in enumerate(vals):
            ref[wr, i] = v
    z_s[wr] = jnp.concatenate([parts[3 * n_pair + j] for j in range(n_pair)], axis=1)


def _dn_branch(x, g, w_in, w_block, cw, aexp, dtb, onorm, *, tt=256):
    bsz, t, d = x.shape
    width = onorm.shape[1]
    n_heads = width // DN_HEAD_DIM
    w_cols = 4 * width + 2 * LANES
    tiles_per_seq = t // tt
    n_tiles = bsz * tiles_per_seq
    n_prob = (tt // DN_CHUNK) * n_heads
    c = DN_CHUNK

    def x_map(s):
        j = jnp.minimum(s, n_tiles - 1)
        return j // tiles_per_seq, j % tiles_per_seq, 0

    def o_map(s):
        j = jnp.maximum(s - 1, 0)
        return j // tiles_per_seq, j % tiles_per_seq, 0

    assert tiles_per_seq > 1
    return pl.pallas_call(
        functools.partial(_dn_kernel, tiles_per_seq=tiles_per_seq),
        grid=(n_tiles + 1,),
        in_specs=[pl.BlockSpec((None, tt, d), x_map), _const_spec((1, d)),
                  _col_block_spec(d, w_cols, w_block), _const_spec(cw.shape),
                  _const_spec((1, LANES)), _const_spec((1, LANES)), _const_spec((1, width))],
        out_specs=pl.BlockSpec((None, tt, width), o_map),
        out_shape=jax.ShapeDtypeStruct((bsz, t, width), BF16),
        scratch_shapes=[pltpu.VMEM((SUBLANES, 3 * width), F32),
                        pltpu.VMEM((n_heads, DN_HEAD_DIM, DN_HEAD_DIM), F32),
                        pltpu.VMEM((2, n_prob, 2 * c, DN_HEAD_DIM), BF16),
                        pltpu.VMEM((2, n_prob, c, DN_HEAD_DIM), BF16),
                        pltpu.VMEM((2, n_prob, c, 2 * DN_HEAD_DIM), BF16),
                        pltpu.VMEM((2, n_prob, c, DN_HEAD_DIM), F32),
                        pltpu.VMEM((2, n_prob, c, DN_HEAD_DIM), BF16),
                        pltpu.VMEM((2, tt // c, c, LANES), F32),
                        pltpu.VMEM((2, tt // c, LANES, LANES), F32),
                        pltpu.VMEM((2, tt, width), F32)],
        compiler_params=_params(("arbitrary",)),
        name="dn_branch",
    )(x, g, w_in, cw, aexp, dtb, onorm)


def _merge_kernel(x_ref, g_ref, ya_ref, yb_ref, yc_ref, wm_ref, wa_ref, wb_ref, wc_ref, wo_ref, o_ref):
    x = x_ref[...]
    d = x.shape[1]
    u = _rms(x, g_ref[...]).astype(BF16)
    y = jnp.zeros_like(x)
    for j, (yr, wr) in enumerate(((ya_ref, wa_ref), (yb_ref, wb_ref), (yc_ref, wc_ref))):
        gate = jax.nn.sigmoid(_dot(u, wm_ref[:, j * d:(j + 1) * d]))
        y = y + gate * _dot(yr[...], wr[...])
    o_ref[...] = x + _dot(y.astype(BF16), wo_ref[...])


def _merge(x2, g, ya, yb, yc, wm, wa, wb, wc, wo, *, tm=512):
    n, d = x2.shape
    row = lambda i: (i, 0)
    return pl.pallas_call(
        _merge_kernel,
        grid=(n // tm,),
        in_specs=[pl.BlockSpec((tm, d), row), _const_spec((1, d)),
                  pl.BlockSpec((tm, ya.shape[1]), row), pl.BlockSpec((tm, yb.shape[1]), row),
                  pl.BlockSpec((tm, yc.shape[1]), row),
                  _const_spec(wm.shape), _const_spec(wa.shape), _const_spec(wb.shape), _const_spec(wc.shape),
                  _const_spec(wo.shape)],
        out_specs=pl.BlockSpec((tm, d), row),
        out_shape=jax.ShapeDtypeStruct((n, d), F32),
        compiler_params=_params(("arbitrary",)),
        name="merge",
    )(x2, g, ya, yb, yc, wm, wa, wb, wc, wo)


def _block_diag_tiles(w):
    nb, k, _ = w.shape
    per = MXU_DIM // k
    w4 = w.reshape(nb // per, per, k, k)
    eye = jnp.eye(per, dtype=w.dtype)
    return jnp.einsum('tpij,pq->tpiqj', w4, eye).reshape(nb // per, MXU_DIM, MXU_DIM)


def _row(v):
    return v.reshape(1, -1).astype(F32)


def _mixer(x, mix_norm, w_in, rg_conv_w, rg_conv_b, rg_w_r, rg_b_r, rg_w_i, rg_b_i, rg_lambda,
           att_q_norm, att_k_norm, dn_conv_w, dn_a_log, dn_dt_bias, dn_out_norm, w_branch, w_out):
    bsz, t, d = x.shape
    rg_w = rg_conv_w.shape[1]
    att_w = att_q_norm.size
    dn_w = dn_out_norm.size
    dn_h = dn_out_norm.shape[0]
    att_off = 2 * rg_w
    dn_off = att_off + 3 * att_w
    merge_off = dn_off + 4 * dn_w + 2 * dn_h
    dn_cols = 4 * dn_w + 2 * LANES
    assert dn_off % dn_cols == 0 and dn_off + dn_cols <= w_in.shape[1] and 2 * dn_h <= LANES
    w_att = w_in[:, att_off:dn_off]
    w_merge = w_in[:, merge_off:merge_off + 3 * d]
    g = _row(mix_norm)

    lamc = _row(-RG_C * jax.nn.softplus(-rg_lambda.astype(F32)))
    ya = _rg_branch(x, g, w_in, rg_conv_w, _row(rg_conv_b),
                    _block_diag_tiles(rg_w_r).astype(BF16), _row(rg_b_r),
                    _block_diag_tiles(rg_w_i).astype(BF16), _row(rg_b_i), lamc)

    n_att_heads = att_q_norm.shape[0]
    yb = _att_branch(x, g, w_att, _row(att_q_norm), _row(att_k_norm),
                     _head_mean_matrix(2 * LANES, ATT_HEAD_DIM), _att_maskbias(n_att_heads))

    pad = jnp.zeros((LANES - 2 * dn_h,), F32)
    aexp = _row(jnp.concatenate([jnp.zeros((dn_h,), F32), jnp.exp(dn_a_log.astype(F32)), pad]))
    dtb = _row(jnp.concatenate([jnp.zeros((dn_h,), F32), dn_dt_bias.astype(F32), pad]))
    yc = _dn_branch(x, g, w_in, dn_off // dn_cols, dn_conv_w, aexp, dtb, _row(dn_out_norm))

    wa = w_branch[:rg_w]
    wb = w_branch[rg_w:rg_w + att_w]
    wc = w_branch[rg_w + att_w:]
    n = bsz * t
    out = _merge(x.reshape(n, d), g, ya.reshape(n, rg_w), yb.reshape(n, att_w), yc.reshape(n, dn_w),
                 w_merge, wa, wb, wc, w_out)
    return out.reshape(bsz, t, d)


def kernel(x, ffn1_norm, ffn1_w_gate, ffn1_w_up, ffn1_w_down, mix_norm, w_in, rg_conv_w, rg_conv_b, rg_w_r, rg_b_r, rg_w_i, rg_b_i, rg_lambda, att_q_norm, att_k_norm, dn_conv_w, dn_a_log, dn_dt_bias, dn_out_norm, w_branch, w_out, ffn2_norm, ffn2_w_gate, ffn2_w_up, ffn2_w_down):
    bsz, t, d = x.shape
    n = bsz * t
    n_layers = ffn1_norm.shape[0]
    ffn1_w = [w[0].astype(BF16) for w in (ffn1_w_gate, ffn1_w_up, ffn1_w_down)]
    for l in range(n_layers):
        casts = [(w, l) for w in (w_in, w_branch, w_out, ffn2_w_gate, ffn2_w_up, ffn2_w_down)]
        x, (w_in_b, w_branch_b, w_out_b, *ffn2_w) = _ffn(x.reshape(n, d), _row(ffn1_norm[l]), *ffn1_w, casts)
        x = _mixer(x.reshape(bsz, t, d), mix_norm[l], w_in_b, rg_conv_w[l], rg_conv_b[l], rg_w_r[l], rg_b_r[l],
                   rg_w_i[l], rg_b_i[l], rg_lambda[l], att_q_norm[l], att_k_norm[l], dn_conv_w[l], dn_a_log[l],
                   dn_dt_bias[l], dn_out_norm[l], w_branch_b, w_out_b)
        casts = [(w, l + 1) for w in (ffn1_w_gate, ffn1_w_up, ffn1_w_down)] if l + 1 < n_layers else []
        x, ffn1_w = _ffn(x.reshape(n, d), _row(ffn2_norm[l]), *ffn2_w, casts)
    return x.reshape(bsz, t, d)
```

```python
import functools
import math

import jax
import jax.numpy as jnp
from jax import lax
from jax.experimental import pallas as pl
from jax.experimental.pallas import tpu as pltpu

F32 = jnp.float32
BF16 = jnp.bfloat16

EPS = 1e-6
NEG_INF = -1e30
RG_C = 8.0
RG_BLOCK_DIM = 64
CONV_WIDTH = 4
ATT_GROUPS = ((128, 1), (512, 4), (2048, 16))
ATT_HEADS_PER_GROUP = 4
ATT_HEAD_DIM = 64
ATT_SPAN = 128
DN_HEAD_DIM = 128
DN_CHUNK = 64

LANES = 128
SUBLANES = 8
MXU_DIM = 256
VMEM_LIMIT_BYTES = 56 * 1024 * 1024


def _rms(x, g):
    return x * lax.rsqrt(jnp.mean(x * x, axis=-1, keepdims=True) + EPS) * g


def _dot(a, b):
    return jnp.dot(a, b, preferred_element_type=F32)


def _dot_nt(a, b):
    return lax.dot_general(a, b, (((1,), (1,)), ((), ())), preferred_element_type=F32)


def _dot_tn(a, b):
    return lax.dot_general(a, b, (((0,), (0,)), ((), ())), preferred_element_type=F32)


def _softplus(z):
    return jnp.maximum(z, 0.0) + jnp.log(1.0 + jnp.exp(-jnp.abs(z)))


def _const_spec(shape, single_buffer=False):
    nd = len(shape)
    mode = pl.Buffered(1) if single_buffer else None
    return pl.BlockSpec(shape, lambda *_: (0,) * nd, pipeline_mode=mode)


def _col_block_spec(rows, cols, col_block, layer):
    return pl.BlockSpec((None, rows, cols), lambda *_: (layer, 0, col_block))


def _params(semantics):
    return pltpu.CompilerParams(dimension_semantics=semantics, vmem_limit_bytes=VMEM_LIMIT_BYTES)


BF16_SUBLANES = 16


def _cast_blocking(rows, n_steps):
    rpb = BF16_SUBLANES
    while rows % rpb or rows // rpb > n_steps:
        rpb += BF16_SUBLANES
        assert rpb <= rows
    return rpb, rows // rpb


def _ffn_kernel(x_ref, g_ref, wg_ref, wu_ref, wd_ref, *rest, f_chunk):
    n_cast = (len(rest) - 1) // 2
    o_ref = rest[n_cast]
    for src, dst in zip(rest[:n_cast], rest[n_cast + 1:]):
        dst[...] = src[...].astype(BF16)
    x = x_ref[...]
    u = _rms(x, g_ref[...]).astype(BF16)
    acc = jnp.zeros_like(x)
    for c in range(wg_ref.shape[1] // f_chunk):
        sl = slice(c * f_chunk, (c + 1) * f_chunk)
        hg = _dot(u, wg_ref[:, sl])
        hu = _dot(u, wu_ref[:, sl])
        h = (hg * jax.nn.sigmoid(hg) * hu).astype(BF16)
        acc = acc + _dot(h, wd_ref[sl, :])
    o_ref[...] = x + 0.5 * acc


def _ffn(x2, g, wg, wu, wd, casts=(), *, tm=1024, f_chunk=256):
    n, d = x2.shape
    f = wg.shape[1]
    n_steps = n // tm
    cast_in, cast_out, cast_shapes = [], [], []
    for stack, layer in casts:
        _, rows, cols = stack.shape
        rpb, nb = _cast_blocking(rows, n_steps)
        cast_in.append(pl.BlockSpec((None, rpb, cols), lambda i, layer=layer, nb=nb: (layer, jnp.minimum(i, nb - 1), 0)))
        cast_out.append(pl.BlockSpec((rpb, cols), lambda i, nb=nb: (jnp.minimum(i, nb - 1), 0)))
        cast_shapes.append(jax.ShapeDtypeStruct((rows, cols), BF16))
    outs = pl.pallas_call(
        functools.partial(_ffn_kernel, f_chunk=f_chunk),
        grid=(n_steps,),
        in_specs=[pl.BlockSpec((tm, d), lambda i: (i, 0)), _const_spec((1, d)),
                  _const_spec((d, f), True), _const_spec((d, f), True), _const_spec((f, d), True)] + cast_in,
        out_specs=[pl.BlockSpec((tm, d), lambda i: (i, 0))] + cast_out,
        out_shape=[jax.ShapeDtypeStruct((n, d), F32)] + cast_shapes,
        compiler_params=_params(("arbitrary",)),
        name="ffn",
    )(x2, g, wg, wu, wd, *[stack for stack, _ in casts])
    return outs[0], list(outs[1:])


def _shift_rows(cur, prev8, s):
    rows = lax.broadcasted_iota(jnp.int32, cur.shape, 0)
    return jnp.where(rows < s, pltpu.roll(prev8, s, 0), pltpu.roll(cur, s, 0))


def _causal_conv(p, tail8, w):
    kw = w.shape[0]
    acc = p * w[kw - 1:kw, :]
    top = p[0:SUBLANES] * w[kw - 1:kw, :]
    for s in range(1, kw):
        wk = w[kw - 1 - s:kw - s, :]
        acc = acc + pltpu.roll(p, s, 0) * wk
        top = top + _shift_rows(p[0:SUBLANES], tail8, s) * wk
    return jnp.concatenate([top, acc[SUBLANES:]], axis=0)


def _segment_perm(tt):
    r = jnp.arange(tt)
    src = (r % SUBLANES) * (tt // SUBLANES) + r // SUBLANES
    return (src[:, None] == jnp.arange(tt)[None, :]).astype(BF16)


def _rg_kernel(x_ref, g_ref, perm_ref, permt_ref, wx_ref, wgate_ref, cw_ref, cb_ref, wr_ref, br_ref, wi_ref,
               bi_ref, lamc_ref, o_ref, tail_ref, carry_ref, a_scr, h_scr):
    tt, width = o_ref.shape
    n_grp = tt // SUBLANES
    kw = cw_ref.shape[0]

    @pl.when(pl.program_id(1) == 0)
    def _():
        tail_ref[...] = jnp.zeros_like(tail_ref)
        carry_ref[...] = jnp.zeros_like(carry_ref)

    u = _rms(x_ref[...], g_ref[...]).astype(BF16)
    up = _dot(perm_ref[...], u).astype(BF16)
    nc = width // MXU_DIM
    rows8 = lax.broadcasted_iota(jnp.int32, (SUBLANES, MXU_DIM), 0)

    def project(c):
        cs = slice(c * MXU_DIM, (c + 1) * MXU_DIM)
        return _dot(up, wx_ref[:, cs]), _dot(up, wgate_ref[:, cs])

    def conv(c, px):
        cs = slice(c * MXU_DIM, (c + 1) * MXU_DIM)
        tail = tail_ref[:, cs]
        head = [_shift_rows(px[(n_grp - k) * SUBLANES:(n_grp - k + 1) * SUBLANES],
                            tail[(kw - 1 - k) * SUBLANES:(kw - k) * SUBLANES], 1) for k in range(kw - 1, 0, -1)]
        ext = jnp.concatenate(head + [px], axis=0)
        tail_ref[:, cs] = px[tt - (kw - 1) * SUBLANES:tt]
        xa = cb_ref[:, cs] + ext[0:tt] * cw_ref[0:1, cs]
        for k in range(1, kw):
            xa = xa + ext[k * SUBLANES:k * SUBLANES + tt] * cw_ref[k:k + 1, cs]
        return xa

    def gates(c, xa):
        xab = xa.astype(BF16)
        return _dot(xab, wr_ref[c]), _dot(xab, wi_ref[c])

    def scan(c, xa, r_lin, i_lin, pg):
        cs = slice(c * MXU_DIM, (c + 1) * MXU_DIM)
        a = jnp.exp(lamc_ref[:, cs] * jax.nn.sigmoid(r_lin + br_ref[:, cs]))
        a_scr[:, cs] = a
        h_scr[:, cs] = jnp.sqrt(1.0 - a * a) * (jax.nn.sigmoid(i_lin + bi_ref[:, cs]) * xa)
        hl = jnp.zeros((SUBLANES, MXU_DIM), F32)
        ap = jnp.ones((SUBLANES, MXU_DIM), F32)
        for j in range(n_grp):
            rs = slice(j * SUBLANES, (j + 1) * SUBLANES)
            aj = a_scr[rs, cs]
            hl = aj * hl + h_scr[rs, cs]
            ap = aj * ap
            h_scr[rs, cs] = hl
            a_scr[rs, cs] = ap
        for s in (1, 2, 4):
            m = rows8 >= s
            hl = jnp.where(m, ap * pltpu.roll(hl, s, 0) + hl, hl)
            ap = jnp.where(m, ap * pltpu.roll(ap, s, 0), ap)
        cin = carry_ref[:, cs]
        seg_end = hl + ap * cin
        carry_ref[:, cs] = jnp.broadcast_to(seg_end[SUBLANES - 1:SUBLANES, :], seg_end.shape)
        seg_in = jnp.where(rows8 < 1, cin, pltpu.roll(seg_end, 1, 0))
        h = (h_scr[:, cs].reshape(n_grp, SUBLANES, MXU_DIM)
             + a_scr[:, cs].reshape(n_grp, SUBLANES, MXU_DIM) * seg_in[None])
        return (h.reshape(tt, MXU_DIM) * jax.nn.gelu(pg)).astype(BF16)

    def emit(c, y):
        o_ref[:, c * MXU_DIM:(c + 1) * MXU_DIM] = _dot(permt_ref[...], y).astype(BF16)

    proj, xas, gts, ys = {}, {}, {}, {}
    for step in range(nc + 3):
        if step < nc:
            proj[step] = project(step)
        c = step - 1
        if 0 <= c < nc:
            xas[c] = conv(c, proj[c][0])
            gts[c] = gates(c, xas[c])
        c = step - 2
        if 0 <= c < nc:
            ys[c] = scan(c, xas[c], gts[c][0], gts[c][1], proj[c][1])
        c = step - 3
        if 0 <= c < nc:
            emit(c, ys[c])


def _rg_branch(x, g, w_in, layer, cw, cb, wr_t, br, wi_t, bi, lamc, *, tt=256):
    bsz, t, d = x.shape
    width = cw.shape[1]
    perm = _segment_perm(tt)
    return pl.pallas_call(
        _rg_kernel,
        grid=(bsz, t // tt),
        in_specs=[pl.BlockSpec((None, tt, d), lambda b, i: (b, i, 0)), _const_spec((1, d)),
                  _const_spec((tt, tt)), _const_spec((tt, tt)),
                  _col_block_spec(d, width, 0, layer), _col_block_spec(d, width, 1, layer),
                  _const_spec(cw.shape), _const_spec((1, width)),
                  _const_spec(wr_t.shape), _const_spec((1, width)),
                  _const_spec(wi_t.shape), _const_spec((1, width)), _const_spec((1, width))],
        out_specs=pl.BlockSpec((None, tt, width), lambda b, i: (b, i, 0)),
        out_shape=jax.ShapeDtypeStruct((bsz, t, width), BF16),
        scratch_shapes=[pltpu.VMEM(((cw.shape[0] - 1) * SUBLANES, width), F32), pltpu.VMEM((SUBLANES, width), F32),
                        pltpu.VMEM((tt, width), F32), pltpu.VMEM((tt, width), F32)],
        compiler_params=_params(("arbitrary", "arbitrary")),
        name="rg_branch",
    )(x, g, perm, perm.T, w_in, w_in, cw, cb, wr_t, br, wi_t, bi, lamc)


ATT_BLOCK_BATCH = 4


def _att_kernel(x_ref, g_ref, w_ref, gq_ref, gk_ref, e4_ref, mb_ref, o_ref, qkv_scr, o_scr, lse_scr, *, rt):
    t_total = o_ref.shape[0]
    n_slab = o_scr.shape[0]
    i = pl.program_id(1)
    row0 = pl.multiple_of(i * rt, rt)
    span = ATT_SPAN

    u = _rms(x_ref[...], g_ref[...]).astype(BF16)
    qkv = _dot(u, w_ref[...])
    width = n_slab * LANES
    e4 = e4_ref[...]
    for s in range(n_slab):
        cs = slice(s * LANES, (s + 1) * LANES)
        qk = jnp.concatenate([qkv[:, cs], qkv[:, width + s * LANES: width + (s + 1) * LANES]], axis=1)
        gain = jnp.concatenate([gq_ref[:, cs] * (ATT_HEAD_DIM ** -0.5), gk_ref[:, cs]], axis=1)
        ms = _dot((qk * qk).astype(BF16), e4)
        qk = qk * lax.rsqrt(ms + EPS) * gain
        qkv_scr[s, pl.ds(row0, rt), :] = qk[:, :LANES]
        qkv_scr[n_slab + s, pl.ds(row0, rt), :] = qk[:, LANES:]
        qkv_scr[2 * n_slab + s, pl.ds(row0, rt), :] = qkv[:, 2 * width + s * LANES: 2 * width + (s + 1) * LANES]

    @pl.when(i == pl.num_programs(1) - 1)
    def _():
        lane = lax.broadcasted_iota(jnp.int32, (span, LANES), 1)
        first_head = lane < ATT_HEAD_DIM

        for g, (window, dil) in enumerate(ATT_GROUPS):
            assert window // dil == span
            nblk = t_total // dil // span
            blocks = [(r, n) for n in range(nblk) for r in range(dil)]

            def rows(start, dil=dil):
                return pl.ds(start, span) if dil == 1 else pl.ds(start, span, stride=dil)

            for b0 in range(0, len(blocks), ATT_BLOCK_BATCH):
                probs = [(r + n * span * dil, n > 0, 2 * g + sl)
                         for r, n in blocks[b0:b0 + ATT_BLOCK_BATCH] for sl in range(2)]
                qs, kk, vv = [], [], []
                for start, has_prev, slab in probs:
                    q2 = qkv_scr[slab, rows(start), :]
                    qs.append(jnp.concatenate([jnp.where(first_head, q2, 0.0), jnp.where(first_head, 0.0, q2)],
                                              axis=0).astype(BF16))
                    kc = qkv_scr[n_slab + slab, rows(start), :]
                    vc = qkv_scr[2 * n_slab + slab, rows(start), :]
                    if has_prev:
                        prev = start - span * dil
                        kc = jnp.concatenate([qkv_scr[n_slab + slab, rows(prev), :], kc], axis=0)
                        vc = jnp.concatenate([qkv_scr[2 * n_slab + slab, rows(prev), :], vc], axis=0)
                    kk.append(kc.astype(BF16))
                    vv.append(vc.astype(BF16))
                sc = [_dot_nt(q_, k_) for q_, k_ in zip(qs, kk)]
                sc = [s_ + (mb_ref[slab] if has_prev else mb_ref[slab, :, span:])
                      for s_, (_, has_prev, slab) in zip(sc, probs)]
                mx = [jnp.max(s_, axis=-1, keepdims=True) for s_ in sc]
                pr = [jnp.exp(s_ - m_) for s_, m_ in zip(sc, mx)]
                den = [jnp.sum(p_, axis=-1, keepdims=True) for p_ in pr]
                pv = [_dot(p_.astype(BF16), v_) for p_, v_ in zip(pr, vv)]
                for (start, _, slab), pv_, m_, d_ in zip(probs, pv, mx, den):
                    on = pv_ * (1.0 / d_)
                    lse = jnp.broadcast_to(m_ + jnp.log(d_), on.shape)
                    o_scr[slab, rows(start), :] = jnp.where(first_head, on[:span], on[span:])
                    lse_scr[slab, rows(start), :] = jnp.where(first_head, lse[:span], lse[span:])

        n_grp = len(ATT_GROUPS)
        ct = 256

        def combine(j, carry):
            rws = pl.ds(pl.multiple_of(j * ct, ct), ct)
            for sl in range(2):
                ls = [lse_scr[2 * g + sl, rws, :] for g in range(n_grp)]
                mx = functools.reduce(jnp.maximum, ls)
                es = [jnp.exp(l - mx) for l in ls]
                inv = 1.0 / functools.reduce(lambda a, b: a + b, es)
                for g in range(n_grp):
                    slab = 2 * g + sl
                    o_ref[rws, slab * LANES:(slab + 1) * LANES] = (o_scr[slab, rws, :] * (es[g] * inv)).astype(BF16)
            return carry

        lax.fori_loop(0, t_total // ct, combine, 0)


def _att_branch(x, g, w_qkv, gq, gk, e4, maskbias, *, rt=256):
    bsz, t, d = x.shape
    width = gq.shape[1]
    n_slab = width // LANES
    return pl.pallas_call(
        functools.partial(_att_kernel, rt=rt),
        grid=(bsz, t // rt),
        in_specs=[pl.BlockSpec((None, rt, d), lambda b, i: (b, i, 0)), _const_spec((1, d)),
                  _const_spec(w_qkv.shape), _const_spec((1, width)), _const_spec((1, width)),
                  _const_spec(e4.shape), _const_spec(maskbias.shape)],
        out_specs=pl.BlockSpec((None, t, width), lambda b, i: (b, 0, 0)),
        out_shape=jax.ShapeDtypeStruct((bsz, t, width), BF16),
        scratch_shapes=[pltpu.VMEM((3 * n_slab, t, LANES), F32), pltpu.VMEM((n_slab, t, LANES), F32),
                        pltpu.VMEM((n_slab, t, LANES), F32)],
        compiler_params=_params(("arbitrary", "arbitrary")),
        name="att_branch",
    )(x, g, w_qkv, gq, gk, e4, maskbias)


def _att_maskbias(n_heads):
    slopes = jnp.exp2(-8.0 * jnp.arange(1, n_heads + 1, dtype=F32) / n_heads)
    qi = jnp.arange(ATT_SPAN)[:, None]
    kj = jnp.arange(2 * ATT_SPAN)[None, :]
    delta = qi + ATT_SPAN - kj
    valid = (delta >= 0) & (delta <= ATT_SPAN)
    dil = jnp.repeat(jnp.array([d for _, d in ATT_GROUPS], F32), ATT_HEADS_PER_GROUP)
    bias = -(slopes * dil)[:, None, None] * delta.astype(F32)[None]
    return jnp.where(valid[None], bias, NEG_INF).reshape(n_heads // 2, 2 * ATT_SPAN, 2 * ATT_SPAN)


def _head_mean_matrix(n_lanes, head_dim):
    lane_head = jnp.arange(n_lanes) // head_dim
    return ((lane_head[:, None] == lane_head[None, :]).astype(F32) / head_dim).astype(BF16)


def _cumsum_rows(x):
    n = x.shape[0]
    rows = lax.broadcasted_iota(jnp.int32, x.shape, 0)
    s = 1
    while s < n:
        x = x + jnp.where(rows >= s, pltpu.roll(x, s, 0), 0.0)
        s *= 2
    return x


def _dn_kernel(x_ref, g_ref, w_ref, cw_ref, aexp_ref, dtb_ref, on_ref, o_ref, tail_ref, s_scr,
               kbq_s, k_s, rhs_s, kd_s, qeg_s, gam_s, gamt_s, z_s, *, tiles_per_seq):
    tt, width = o_ref.shape
    n_heads = width // DN_HEAD_DIM
    c = DN_CHUNK
    step = pl.program_id(0)
    stage_scratch = (kbq_s, k_s, rhs_s, kd_s, qeg_s, gam_s, gamt_s, z_s)

    @pl.when(step == 0)
    def _():
        for ref in stage_scratch:
            ref[...] = jnp.zeros_like(ref)

    @pl.when(step % tiles_per_seq == 0)
    def _():
        tail_ref[...] = jnp.zeros_like(tail_ref)

    @pl.when((step % tiles_per_seq == 1) | (step == 0))
    def _():
        s_scr[...] = jnp.zeros_like(s_scr)

    u = _rms(x_ref[...], g_ref[...]).astype(BF16)
    parts = {}

    def project(*idx):
        for j in idx:
            parts[j] = _dot(u, w_ref[:, j * MXU_DIM:(j + 1) * MXU_DIM])

    assert 2 * c == LANES and n_heads % 4 == 0
    n_ch = tt // c
    n_pair = n_heads // 2
    row2 = lax.broadcasted_iota(jnp.int32, (c, LANES), 0)
    lane2 = lax.broadcasted_iota(jnp.int32, (c, LANES), 1)
    col2 = lane2 & (c - 1)
    incl = col2 <= row2
    strict = col2 < row2
    lo2 = lane2 < c
    lane2x = lax.broadcasted_iota(jnp.int32, (LANES, LANES), 1)
    lo2x = lane2x < c
    row4 = lax.broadcasted_iota(jnp.int32, (c, 2 * LANES), 0)
    lane4 = lax.broadcasted_iota(jnp.int32, (c, 2 * LANES), 1)
    eye4 = ((lane4 & (c - 1)) == row4).astype(F32)
    blk4 = lane4 // c
    zero_k = jnp.zeros((c, DN_HEAD_DIM), BF16)

    def block_diag(m):
        return jnp.concatenate([jnp.where(blk4 == i, m, 0.0) for i in range(4)], axis=0).astype(BF16)

    def cols(base, h):
        return slice(base * width + h * DN_HEAD_DIM, base * width + (h + 1) * DN_HEAD_DIM)

    def l2n(m):
        return m * lax.rsqrt(jnp.sum(m * m, axis=-1, keepdims=True) + EPS)

    probs = [(ch, h) for ch in range(n_ch) for h in range(n_heads)]
    pairs = [(ch, p) for ch in range(n_ch) for p in range(n_pair)]
    quads = [(ch, qd) for ch in range(n_ch) for qd in range(n_pair // 2)]
    pix = {ph: i for i, ph in enumerate(probs)}

    n_prob = len(probs)
    staged = {name: [None] * n_prob for name in ("kbq", "k", "rhs", "kd", "qeg")}
    common = {}
    ba_piece = 4 * n_pair

    def prep_common():
        ba = parts[ba_piece][:, :LANES]
        common["beta"] = jax.nn.sigmoid(ba)
        glog_all = -aexp_ref[...] * _softplus(ba + dtb_ref[...])
        common["gam"] = [_cumsum_rows(glog_all[ch * c:(ch + 1) * c]) for ch in range(n_ch)]
        common["gam_t"] = [jnp.concatenate([g_, g_], axis=0).T for g_ in common["gam"]]

    def prep_pair(p):
        def conv_silu(section):
            blk = parts[section * n_pair + p]
            cs = slice(section * width + p * MXU_DIM, section * width + (p + 1) * MXU_DIM)
            y = _causal_conv(blk, tail_ref[:, cs], cw_ref[:, cs])
            tail_ref[:, cs] = blk[tt - SUBLANES:tt]
            return y * jax.nn.sigmoid(y)

        qs, ks, vs = conv_silu(0), conv_silu(1), conv_silu(2)
        for ch in range(n_ch):
            rs = slice(ch * c, (ch + 1) * c)
            for sub in range(2):
                h = 2 * p + sub
                ls = slice(sub * DN_HEAD_DIM, (sub + 1) * DN_HEAD_DIM)
                q_ = l2n(qs[rs, ls]) * (DN_HEAD_DIM ** -0.5)
                k_ = l2n(ks[rs, ls])
                beta = common["beta"][rs, h:h + 1]
                gc = common["gam"][ch][:, n_heads + h:n_heads + h + 1]
                eg_ = jnp.exp(gc)
                kb_ = k_ * beta
                i = pix[(ch, h)]
                staged["rhs"][i] = jnp.concatenate([vs[rs, ls] * beta, kb_ * eg_], axis=1).astype(BF16)
                staged["kbq"][i] = jnp.concatenate([kb_, q_], axis=0).astype(BF16)
                staged["kd"][i] = k_ * jnp.exp(gc[c - 1:c, :] - gc)
                staged["qeg"][i] = (q_ * eg_).astype(BF16)
                staged["k"][i] = k_.astype(BF16)

    fill = iter([
        lambda: project(ba_piece, 0, n_pair, 2 * n_pair),
        lambda: (prep_common(), project(1, n_pair + 1, 2 * n_pair + 1)),
        lambda: prep_pair(0),
        lambda: project(2, n_pair + 2, 2 * n_pair + 2),
        lambda: prep_pair(1),
        lambda: project(3, n_pair + 3, 2 * n_pair + 3),
        lambda: prep_pair(2),
        lambda: project(3 * n_pair, 3 * n_pair + 1),
        lambda: prep_pair(3),
        lambda: project(3 * n_pair + 2),
        lambda: project(3 * n_pair + 3),
    ])
    assert n_pair == 4

    def interleave():
        action = next(fill, None)
        if action is not None:
            action()

    rd = (step + 1) % 2
    wr = step % 2
    kbq = [kbq_s[rd, i] for i in range(n_prob)]
    k = [k_s[rd, i] for i in range(n_prob)]
    rhs = [rhs_s[rd, i] for i in range(n_prob)]
    kd = [kd_s[rd, i] for i in range(n_prob)]
    qeg = [qeg_s[rd, i] for i in range(n_prob)]
    gam_all = [gam_s[rd, ch] for ch in range(n_ch)]
    gam_t = [gamt_s[rd, ch] for ch in range(n_ch)]
    z = z_s[rd]
    gcol = [gam_all[ch][:, n_heads + h:n_heads + h + 1] for ch, h in probs]
    glast = [gc[c - 1:c, :] for gc in gcol]

    def both(lst, ch, p):
        return lst[pix[(ch, 2 * p)]], lst[pix[(ch, 2 * p + 1)]]

    interleave()
    kq = []
    for ch, p in pairs:
        (x0, x1), (k0, k1) = both(kbq, ch, p), both(k, ch, p)
        kq.append(_dot_nt(x0, jnp.concatenate([k0.astype(BF16), zero_k], axis=0))
                  + _dot_nt(x1, jnp.concatenate([zero_k, k1.astype(BF16)], axis=0)))
    interleave()
    decay = []
    for ch, p in pairs:
        g0, g1 = both(gcol, ch, p)
        gr = jnp.where(lo2[0:1], gam_t[ch][n_heads + 2 * p:n_heads + 2 * p + 1, :],
                       gam_t[ch][n_heads + 2 * p + 1:n_heads + 2 * p + 2, :])
        diff = jnp.where(lo2, g0, g1) - gr
        decay.append(jnp.where(incl, jnp.exp(jnp.where(incl, diff, 0.0)), 0.0))
    a2 = [jnp.where(strict, kq_[:c] * d_, 0.0) for kq_, d_ in zip(kq, decay)]
    qk2 = [jnp.where(incl, kq_[c:] * d_, 0.0) for kq_, d_ in zip(kq, decay)]

    a4 = [jnp.concatenate([a2[ch * n_pair + 2 * qd], a2[ch * n_pair + 2 * qd + 1]], axis=1) for ch, qd in quads]
    tinv = [eye4 - a_ for a_ in a4]
    pw = [_dot(a_.astype(BF16), block_diag(a_)) for a_ in a4]
    interleave()
    for _ in range(int(math.log2(c)) - 2):
        r = [_dot(jnp.concatenate([t_, p_], axis=0).astype(BF16), block_diag(p_)) for t_, p_ in zip(tinv, pw)]
        tinv = [t_ + r_[:c] for t_, r_ in zip(tinv, r)]
        pw = [r_[c:] for r_ in r]
        interleave()
    tinv = [t_ + _dot(t_.astype(BF16), block_diag(p_)) for t_, p_ in zip(tinv, pw)]
    interleave()

    def split_rows(m):
        return jnp.concatenate([jnp.where(lo2, m, 0.0), jnp.where(lo2, 0.0, m)], axis=0).astype(BF16)

    sol, lhs_o = [], []
    for i, (ch, p) in enumerate(pairs):
        t2 = tinv[i // 2][:, (i % 2) * LANES:(i % 2 + 1) * LANES]
        r0, r1 = both(rhs, ch, p)
        sol.append(_dot(split_rows(t2), jnp.concatenate([r0, r1], axis=0)))
        kd0, kd1 = both(kd, ch, p)
        kdt = jnp.concatenate([kd0, kd1], axis=0).T
        lhs_o.append(jnp.concatenate([split_rows(qk2[i]), jnp.where(lo2x, kdt, 0.0).astype(BF16),
                                      jnp.where(lo2x, 0.0, kdt).astype(BF16)], axis=0))
    wq = []
    for ch, h in probs:
        s_ = sol[ch * n_pair + h // 2][(h % 2) * c:(h % 2 + 1) * c]
        i = pix[(ch, h)]
        wq.append(jnp.concatenate([s_[:, DN_HEAD_DIM:].astype(BF16), qeg[i]], axis=0))
    interleave()

    for ch in range(n_ch):
        rs = slice(ch * c, (ch + 1) * c)
        state = [s_scr[h] for h in range(n_heads)]
        ws = [_dot(wq[pix[(ch, h)]], state[h].astype(BF16)) for h in range(n_heads)]
        interleave()
        un = [sol[ch * n_pair + h // 2][(h % 2) * c:(h % 2 + 1) * c, :DN_HEAD_DIM] - ws[h][:c] for h in range(n_heads)]
        ou = [_dot(lhs_o[ch * n_pair + p], jnp.concatenate([un[2 * p], un[2 * p + 1]], axis=0).astype(BF16))
              for p in range(n_pair)]
        interleave()
        for h in range(n_heads):
            hs = slice(h * DN_HEAD_DIM, (h + 1) * DN_HEAD_DIM)
            o2 = ou[h // 2]
            sub = h % 2
            s_scr[h] = (state[h] * jnp.exp(glast[pix[(ch, h)]])
                        + o2[2 * c + sub * DN_HEAD_DIM:2 * c + (sub + 1) * DN_HEAD_DIM])
            on = _rms(ws[h][c:] + o2[sub * c:(sub + 1) * c], on_ref[:, hs])
            zz = z[rs, hs]
            o_ref[rs, hs] = (on * (zz * jax.nn.sigmoid(zz))).astype(BF16)

    for action in fill:
        action()
    for ref, vals in ((kbq_s, staged["kbq"]), (k_s, staged["k"]), (rhs_s, staged["rhs"]), (kd_s, staged["kd"]),
                      (qeg_s, staged["qeg"]), (gam_s, common["gam"]), (gamt_s, common["gam_t"])):
        for i, v in enumerate(vals):
            ref[wr, i] = v
    z_s[wr] = jnp.concatenate([parts[3 * n_pair + j] for j in range(n_pair)], axis=1)


def _dn_branch(x, g, w_in, layer, w_block, cw, aexp, dtb, onorm, *, tt=256):
    bsz, t, d = x.shape
    width = onorm.shape[1]
    n_heads = width // DN_HEAD_DIM
    w_cols = 4 * width + 2 * LANES
    tiles_per_seq = t // tt
    n_tiles = bsz * tiles_per_seq
    n_prob = (tt // DN_CHUNK) * n_heads
    c = DN_CHUNK

    def x_map(s):
        j = jnp.minimum(s, n_tiles - 1)
        return j // tiles_per_seq, j % tiles_per_seq, 0

    def o_map(s):
        j = jnp.maximum(s - 1, 0)
        return j // tiles_per_seq, j % tiles_per_seq, 0

    assert tiles_per_seq > 1
    return pl.pallas_call(
        functools.partial(_dn_kernel, tiles_per_seq=tiles_per_seq),
        grid=(n_tiles + 1,),
        in_specs=[pl.BlockSpec((None, tt, d), x_map), _const_spec((1, d)),
                  _col_block_spec(d, w_cols, w_block, layer), _const_spec(cw.shape),
                  _const_spec((1, LANES)), _const_spec((1, LANES)), _const_spec((1, width))],
        out_specs=pl.BlockSpec((None, tt, width), o_map),
        out_shape=jax.ShapeDtypeStruct((bsz, t, width), BF16),
        scratch_shapes=[pltpu.VMEM((SUBLANES, 3 * width), F32),
                        pltpu.VMEM((n_heads, DN_HEAD_DIM, DN_HEAD_DIM), F32),
                        pltpu.VMEM((2, n_prob, 2 * c, DN_HEAD_DIM), BF16),
                        pltpu.VMEM((2, n_prob, c, DN_HEAD_DIM), BF16),
                        pltpu.VMEM((2, n_prob, c, 2 * DN_HEAD_DIM), BF16),
                        pltpu.VMEM((2, n_prob, c, DN_HEAD_DIM), F32),
                        pltpu.VMEM((2, n_prob, c, DN_HEAD_DIM), BF16),
                        pltpu.VMEM((2, tt // c, c, LANES), F32),
                        pltpu.VMEM((2, tt // c, LANES, LANES), F32),
                        pltpu.VMEM((2, tt, width), F32)],
        compiler_params=_params(("arbitrary",)),
        name="dn_branch",
    )(x, g, w_in, cw, aexp, dtb, onorm)


def _merge_kernel(x_ref, g_ref, ya_ref, yb_ref, yc_ref, wm_ref, wa_ref, wb_ref, wc_ref, wo_ref, o_ref):
    x = x_ref[...]
    d = x.shape[1]
    u = _rms(x, g_ref[...]).astype(BF16)
    y = jnp.zeros_like(x)
    for j, (yr, wr) in enumerate(((ya_ref, wa_ref), (yb_ref, wb_ref), (yc_ref, wc_ref))):
        gate = jax.nn.sigmoid(_dot(u, wm_ref[:, j * d:(j + 1) * d]))
        y = y + gate * _dot(yr[...], wr[...])
    o_ref[...] = x + _dot(y.astype(BF16), wo_ref[...])


def _merge(x2, g, ya, yb, yc, wm, wa, wb, wc, wo, *, tm=1024):
    n, d = x2.shape
    row = lambda i: (i, 0)
    return pl.pallas_call(
        _merge_kernel,
        grid=(n // tm,),
        in_specs=[pl.BlockSpec((tm, d), row), _const_spec((1, d)),
                  pl.BlockSpec((tm, ya.shape[1]), row), pl.BlockSpec((tm, yb.shape[1]), row),
                  pl.BlockSpec((tm, yc.shape[1]), row),
                  _const_spec(wm.shape, True), _const_spec(wa.shape, True), _const_spec(wb.shape, True),
                  _const_spec(wc.shape, True), _const_spec(wo.shape, True)],
        out_specs=pl.BlockSpec((tm, d), row),
        out_shape=jax.ShapeDtypeStruct((n, d), F32),
        compiler_params=_params(("arbitrary",)),
        name="merge",
    )(x2, g, ya, yb, yc, wm, wa, wb, wc, wo)


def _block_diag_tiles(w):
    nb, k, _ = w.shape
    per = MXU_DIM // k
    w4 = w.reshape(nb // per, per, k, k)
    eye = jnp.eye(per, dtype=w.dtype)
    return jnp.einsum('tpij,pq->tpiqj', w4, eye).reshape(nb // per, MXU_DIM, MXU_DIM)


def _row(v):
    return v.reshape(1, -1).astype(F32)


def _mixer(x, layer, mix_norm, w_in, rg_conv_w, rg_conv_b, rg_w_r, rg_b_r, rg_w_i, rg_b_i, rg_lambda,
           att_q_norm, att_k_norm, dn_conv_w, dn_a_log, dn_dt_bias, dn_out_norm, w_branch, w_out):
    bsz, t, d = x.shape
    rg_w = rg_conv_w.shape[1]
    att_w = att_q_norm.size
    dn_w = dn_out_norm.size
    dn_h = dn_out_norm.shape[0]
    att_off = 2 * rg_w
    dn_off = att_off + 3 * att_w
    merge_off = dn_off + 4 * dn_w + 2 * dn_h
    dn_cols = 4 * dn_w + 2 * LANES
    assert dn_off % dn_cols == 0 and dn_off + dn_cols <= w_in.shape[2] and 2 * dn_h <= LANES
    w_att = w_in[layer, :, att_off:dn_off]
    w_merge = w_in[layer, :, merge_off:merge_off + 3 * d]
    g = _row(mix_norm)

    lamc = _row(-RG_C * jax.nn.softplus(-rg_lambda.astype(F32)))
    ya = _rg_branch(x, g, w_in, layer, rg_conv_w, _row(rg_conv_b),
                    _block_diag_tiles(rg_w_r).astype(BF16), _row(rg_b_r),
                    _block_diag_tiles(rg_w_i).astype(BF16), _row(rg_b_i), lamc)

    n_att_heads = att_q_norm.shape[0]
    yb = _att_branch(x, g, w_att, _row(att_q_norm), _row(att_k_norm),
                     _head_mean_matrix(2 * LANES, ATT_HEAD_DIM), _att_maskbias(n_att_heads))

    pad = jnp.zeros((LANES - 2 * dn_h,), F32)
    aexp = _row(jnp.concatenate([jnp.zeros((dn_h,), F32), jnp.exp(dn_a_log.astype(F32)), pad]))
    dtb = _row(jnp.concatenate([jnp.zeros((dn_h,), F32), dn_dt_bias.astype(F32), pad]))
    yc = _dn_branch(x, g, w_in, layer, dn_off // dn_cols, dn_conv_w, aexp, dtb, _row(dn_out_norm))

    wa = w_branch[:rg_w]
    wb = w_branch[rg_w:rg_w + att_w]
    wc = w_branch[rg_w + att_w:]
    n = bsz * t
    out = _merge(x.reshape(n, d), g, ya.reshape(n, rg_w), yb.reshape(n, att_w), yc.reshape(n, dn_w),
                 w_merge, wa, wb, wc, w_out)
    return out.reshape(bsz, t, d)


def kernel(x, ffn1_norm, ffn1_w_gate, ffn1_w_up, ffn1_w_down, mix_norm, w_in, rg_conv_w, rg_conv_b, rg_w_r, rg_b_r, rg_w_i, rg_b_i, rg_lambda, att_q_norm, att_k_norm, dn_conv_w, dn_a_log, dn_dt_bias, dn_out_norm, w_branch, w_out, ffn2_norm, ffn2_w_gate, ffn2_w_up, ffn2_w_down):
    bsz, t, d = x.shape
    n = bsz * t
    n_layers = ffn1_norm.shape[0]
    ffn1_w = [w[0].astype(BF16) for w in (ffn1_w_gate, ffn1_w_up, ffn1_w_down)]
    w_in_all = w_in.astype(BF16)
    for l in range(n_layers):
        casts = [(w, l) for w in (w_branch, w_out, ffn2_w_gate, ffn2_w_up, ffn2_w_down)]
        x, (w_branch_b, w_out_b, *ffn2_w) = _ffn(x.reshape(n, d), _row(ffn1_norm[l]), *ffn1_w, casts)
        x = _mixer(x.reshape(bsz, t, d), l, mix_norm[l], w_in_all, rg_conv_w[l], rg_conv_b[l], rg_w_r[l], rg_b_r[l],
                   rg_w_i[l], rg_b_i[l], rg_lambda[l], att_q_norm[l], att_k_norm[l], dn_conv_w[l], dn_a_log[l],
                   dn_dt_bias[l], dn_out_norm[l], w_branch_b, w_out_b)
        casts = [(w, l + 1) for w in (ffn1_w_gate, ffn1_w_up, ffn1_w_down)] if l + 1 < n_layers else []
        x, ffn1_w = _ffn(x.reshape(n, d), _row(ffn2_norm[l]), *ffn2_w, casts)
    return x.reshape(bsz, t, d)
```

```python
import functools
import math

import jax
import jax.numpy as jnp
from jax import lax
from jax.experimental import pallas as pl
from jax.experimental.pallas import tpu as pltpu

F32 = jnp.float32
BF16 = jnp.bfloat16

EPS = 1e-6
NEG_INF = -1e30
RG_C = 8.0
RG_BLOCK_DIM = 64
CONV_WIDTH = 4
ATT_GROUPS = ((128, 1), (512, 4), (2048, 16))
ATT_HEADS_PER_GROUP = 4
ATT_HEAD_DIM = 64
ATT_SPAN = 128
DN_HEAD_DIM = 128
DN_CHUNK = 64

LANES = 128
SUBLANES = 8
MXU_DIM = 256
VMEM_LIMIT_BYTES = 56 * 1024 * 1024


def _rms(x, g):
    return x * lax.rsqrt(jnp.mean(x * x, axis=-1, keepdims=True) + EPS) * g


def _dot(a, b):
    return jnp.dot(a, b, preferred_element_type=F32)


def _dot_nt(a, b):
    return lax.dot_general(a, b, (((1,), (1,)), ((), ())), preferred_element_type=F32)


def _dot_tn(a, b):
    return lax.dot_general(a, b, (((0,), (0,)), ((), ())), preferred_element_type=F32)


def _softplus(z):
    return jnp.maximum(z, 0.0) + jnp.log(1.0 + jnp.exp(-jnp.abs(z)))


def _const_spec(shape, single_buffer=False):
    nd = len(shape)
    mode = pl.Buffered(1) if single_buffer else None
    return pl.BlockSpec(shape, lambda *_: (0,) * nd, pipeline_mode=mode)


def _col_block_spec(rows, cols, col_block, layer):
    return pl.BlockSpec((None, rows, cols), lambda *_: (layer, 0, col_block))


def _params(semantics):
    return pltpu.CompilerParams(dimension_semantics=semantics, vmem_limit_bytes=VMEM_LIMIT_BYTES)


BF16_SUBLANES = 16


def _cast_blocking(rows, n_steps):
    rpb = BF16_SUBLANES
    while rows % rpb or rows // rpb > n_steps:
        rpb += BF16_SUBLANES
        assert rpb <= rows
    return rpb, rows // rpb


def _ffn_kernel(x_ref, g_ref, wg_ref, wu_ref, wd_ref, *rest, f_chunk):
    n_cast = (len(rest) - 1) // 2
    o_ref = rest[n_cast]
    for src, dst in zip(rest[:n_cast], rest[n_cast + 1:]):
        dst[...] = src[...].astype(BF16)
    x = x_ref[...]
    u = _rms(x, g_ref[...]).astype(BF16)
    acc = jnp.zeros_like(x)
    for c in range(wg_ref.shape[1] // f_chunk):
        sl = slice(c * f_chunk, (c + 1) * f_chunk)
        hg = _dot(u, wg_ref[:, sl])
        hu = _dot(u, wu_ref[:, sl])
        h = (hg * jax.nn.sigmoid(hg) * hu).astype(BF16)
        acc = acc + _dot(h, wd_ref[sl, :])
    o_ref[...] = x + 0.5 * acc


def _ffn(x2, g, wg, wu, wd, casts=(), *, tm=1024, f_chunk=256):
    n, d = x2.shape
    f = wg.shape[1]
    n_steps = n // tm
    cast_in, cast_out, cast_shapes = [], [], []
    for stack, layer in casts:
        _, rows, cols = stack.shape
        rpb, nb = _cast_blocking(rows, n_steps)
        cast_in.append(pl.BlockSpec((None, rpb, cols), lambda i, layer=layer, nb=nb: (layer, jnp.minimum(i, nb - 1), 0)))
        cast_out.append(pl.BlockSpec((rpb, cols), lambda i, nb=nb: (jnp.minimum(i, nb - 1), 0)))
        cast_shapes.append(jax.ShapeDtypeStruct((rows, cols), BF16))
    outs = pl.pallas_call(
        functools.partial(_ffn_kernel, f_chunk=f_chunk),
        grid=(n_steps,),
        in_specs=[pl.BlockSpec((tm, d), lambda i: (i, 0)), _const_spec((1, d)),
                  _const_spec((d, f), True), _const_spec((d, f), True), _const_spec((f, d), True)] + cast_in,
        out_specs=[pl.BlockSpec((tm, d), lambda i: (i, 0))] + cast_out,
        out_shape=[jax.ShapeDtypeStruct((n, d), F32)] + cast_shapes,
        compiler_params=_params(("arbitrary",)),
        name="ffn",
    )(x2, g, wg, wu, wd, *[stack for stack, _ in casts])
    return outs[0], list(outs[1:])


def _shift_rows(cur, prev8, s):
    rows = lax.broadcasted_iota(jnp.int32, cur.shape, 0)
    return jnp.where(rows < s, pltpu.roll(prev8, s, 0), pltpu.roll(cur, s, 0))


def _causal_conv(p, tail8, w):
    kw = w.shape[0]
    acc = p * w[kw - 1:kw, :]
    top = p[0:SUBLANES] * w[kw - 1:kw, :]
    for s in range(1, kw):
        wk = w[kw - 1 - s:kw - s, :]
        acc = acc + pltpu.roll(p, s, 0) * wk
        top = top + _shift_rows(p[0:SUBLANES], tail8, s) * wk
    return jnp.concatenate([top, acc[SUBLANES:]], axis=0)


def _segment_perm(tt):
    r = jnp.arange(tt)
    src = (r % SUBLANES) * (tt // SUBLANES) + r // SUBLANES
    return (src[:, None] == jnp.arange(tt)[None, :]).astype(BF16)


def _rg_kernel(x_ref, g_ref, perm_ref, permt_ref, wx_ref, wgate_ref, cw_ref, cb_ref, wr_ref, br_ref, wi_ref,
               bi_ref, lamc_ref, o_ref, tail_ref, carry_ref, a_scr, h_scr):
    tt, width = o_ref.shape
    n_grp = tt // SUBLANES
    kw = cw_ref.shape[0]

    @pl.when(pl.program_id(1) == 0)
    def _():
        tail_ref[...] = jnp.zeros_like(tail_ref)
        carry_ref[...] = jnp.zeros_like(carry_ref)

    u = _rms(x_ref[...], g_ref[...]).astype(BF16)
    up = _dot(perm_ref[...], u).astype(BF16)
    nc = width // MXU_DIM
    rows8 = lax.broadcasted_iota(jnp.int32, (SUBLANES, MXU_DIM), 0)

    def project(c):
        cs = slice(c * MXU_DIM, (c + 1) * MXU_DIM)
        return _dot(up, wx_ref[:, cs]), _dot(up, wgate_ref[:, cs])

    def conv(c, px):
        cs = slice(c * MXU_DIM, (c + 1) * MXU_DIM)
        tail = tail_ref[:, cs]
        head = [_shift_rows(px[(n_grp - k) * SUBLANES:(n_grp - k + 1) * SUBLANES],
                            tail[(kw - 1 - k) * SUBLANES:(kw - k) * SUBLANES], 1) for k in range(kw - 1, 0, -1)]
        ext = jnp.concatenate(head + [px], axis=0)
        tail_ref[:, cs] = px[tt - (kw - 1) * SUBLANES:tt]
        xa = cb_ref[:, cs] + ext[0:tt] * cw_ref[0:1, cs]
        for k in range(1, kw):
            xa = xa + ext[k * SUBLANES:k * SUBLANES + tt] * cw_ref[k:k + 1, cs]
        return xa

    def gates(c, xa):
        xab = xa.astype(BF16)
        return _dot(xab, wr_ref[c]), _dot(xab, wi_ref[c])

    def scan(c, xa, r_lin, i_lin, pg):
        cs = slice(c * MXU_DIM, (c + 1) * MXU_DIM)
        a = jnp.exp(lamc_ref[:, cs] * jax.nn.sigmoid(r_lin + br_ref[:, cs]))
        a_scr[:, cs] = a
        h_scr[:, cs] = jnp.sqrt(1.0 - a * a) * (jax.nn.sigmoid(i_lin + bi_ref[:, cs]) * xa)
        hl = jnp.zeros((SUBLANES, MXU_DIM), F32)
        ap = jnp.ones((SUBLANES, MXU_DIM), F32)
        for j in range(n_grp):
            rs = slice(j * SUBLANES, (j + 1) * SUBLANES)
            aj = a_scr[rs, cs]
            hl = aj * hl + h_scr[rs, cs]
            ap = aj * ap
            h_scr[rs, cs] = hl
            a_scr[rs, cs] = ap
        for s in (1, 2, 4):
            m = rows8 >= s
            hl = jnp.where(m, ap * pltpu.roll(hl, s, 0) + hl, hl)
            ap = jnp.where(m, ap * pltpu.roll(ap, s, 0), ap)
        cin = carry_ref[:, cs]
        seg_end = hl + ap * cin
        carry_ref[:, cs] = jnp.broadcast_to(seg_end[SUBLANES - 1:SUBLANES, :], seg_end.shape)
        seg_in = jnp.where(rows8 < 1, cin, pltpu.roll(seg_end, 1, 0))
        h = (h_scr[:, cs].reshape(n_grp, SUBLANES, MXU_DIM)
             + a_scr[:, cs].reshape(n_grp, SUBLANES, MXU_DIM) * seg_in[None])
        return (h.reshape(tt, MXU_DIM) * jax.nn.gelu(pg)).astype(BF16)

    def emit(c, y):
        o_ref[:, c * MXU_DIM:(c + 1) * MXU_DIM] = _dot(permt_ref[...], y).astype(BF16)

    proj, xas, gts, ys = {}, {}, {}, {}
    for step in range(nc + 3):
        if step < nc:
            proj[step] = project(step)
        c = step - 1
        if 0 <= c < nc:
            xas[c] = conv(c, proj[c][0])
            gts[c] = gates(c, xas[c])
        c = step - 2
        if 0 <= c < nc:
            ys[c] = scan(c, xas[c], gts[c][0], gts[c][1], proj[c][1])
        c = step - 3
        if 0 <= c < nc:
            emit(c, ys[c])


def _rg_branch(x, g, w_in, layer, cw, cb, wr_t, br, wi_t, bi, lamc, *, tt=256):
    bsz, t, d = x.shape
    width = cw.shape[1]
    perm = _segment_perm(tt)
    return pl.pallas_call(
        _rg_kernel,
        grid=(bsz, t // tt),
        in_specs=[pl.BlockSpec((None, tt, d), lambda b, i: (b, i, 0)), _const_spec((1, d)),
                  _const_spec((tt, tt)), _const_spec((tt, tt)),
                  _col_block_spec(d, width, 0, layer), _col_block_spec(d, width, 1, layer),
                  _const_spec(cw.shape), _const_spec((1, width)),
                  _const_spec(wr_t.shape), _const_spec((1, width)),
                  _const_spec(wi_t.shape), _const_spec((1, width)), _const_spec((1, width))],
        out_specs=pl.BlockSpec((None, tt, width), lambda b, i: (b, i, 0)),
        out_shape=jax.ShapeDtypeStruct((bsz, t, width), BF16),
        scratch_shapes=[pltpu.VMEM(((cw.shape[0] - 1) * SUBLANES, width), F32), pltpu.VMEM((SUBLANES, width), F32),
                        pltpu.VMEM((tt, width), F32), pltpu.VMEM((tt, width), F32)],
        compiler_params=_params(("arbitrary", "arbitrary")),
        name="rg_branch",
    )(x, g, perm, perm.T, w_in, w_in, cw, cb, wr_t, br, wi_t, bi, lamc)


ATT_BLOCK_BATCH = 4


def _att_kernel(x_ref, g_ref, w_ref, gq_ref, gk_ref, e4_ref, mb_ref, o_ref, qkv_scr, o_scr, lse_scr, *, rt):
    t_total = o_ref.shape[0]
    n_slab = o_scr.shape[0]
    i = pl.program_id(1)
    row0 = pl.multiple_of(i * rt, rt)
    span = ATT_SPAN

    u = _rms(x_ref[...], g_ref[...]).astype(BF16)
    qkv = _dot(u, w_ref[...])
    width = n_slab * LANES
    e4 = e4_ref[...]
    for s in range(n_slab):
        cs = slice(s * LANES, (s + 1) * LANES)
        qk = jnp.concatenate([qkv[:, cs], qkv[:, width + s * LANES: width + (s + 1) * LANES]], axis=1)
        gain = jnp.concatenate([gq_ref[:, cs] * (ATT_HEAD_DIM ** -0.5), gk_ref[:, cs]], axis=1)
        ms = _dot((qk * qk).astype(BF16), e4)
        qk = qk * lax.rsqrt(ms + EPS) * gain
        qkv_scr[s, pl.ds(row0, rt), :] = qk[:, :LANES]
        qkv_scr[n_slab + s, pl.ds(row0, rt), :] = qk[:, LANES:]
        qkv_scr[2 * n_slab + s, pl.ds(row0, rt), :] = qkv[:, 2 * width + s * LANES: 2 * width + (s + 1) * LANES]

    @pl.when(i == pl.num_programs(1) - 1)
    def _():
        lane = lax.broadcasted_iota(jnp.int32, (span, LANES), 1)
        first_head = lane < ATT_HEAD_DIM

        for g, (window, dil) in enumerate(ATT_GROUPS):
            assert window // dil == span
            nblk = t_total // dil // span
            blocks = [(r, n) for n in range(nblk) for r in range(dil)]

            def rows(start, dil=dil):
                return pl.ds(start, span) if dil == 1 else pl.ds(start, span, stride=dil)

            for b0 in range(0, len(blocks), ATT_BLOCK_BATCH):
                probs = [(r + n * span * dil, n > 0, 2 * g + sl)
                         for r, n in blocks[b0:b0 + ATT_BLOCK_BATCH] for sl in range(2)]
                qs, kk, vv = [], [], []
                for start, has_prev, slab in probs:
                    q2 = qkv_scr[slab, rows(start), :]
                    qs.append(jnp.concatenate([jnp.where(first_head, q2, 0.0), jnp.where(first_head, 0.0, q2)],
                                              axis=0).astype(BF16))
                    kc = qkv_scr[n_slab + slab, rows(start), :]
                    vc = qkv_scr[2 * n_slab + slab, rows(start), :]
                    if has_prev:
                        prev = start - span * dil
                        kc = jnp.concatenate([qkv_scr[n_slab + slab, rows(prev), :], kc], axis=0)
                        vc = jnp.concatenate([qkv_scr[2 * n_slab + slab, rows(prev), :], vc], axis=0)
                    kk.append(kc.astype(BF16))
                    vv.append(vc.astype(BF16))
                sc = [_dot_nt(q_, k_) for q_, k_ in zip(qs, kk)]
                sc = [s_ + (mb_ref[slab] if has_prev else mb_ref[slab, :, span:])
                      for s_, (_, has_prev, slab) in zip(sc, probs)]
                mx = [jnp.max(s_, axis=-1, keepdims=True) for s_ in sc]
                pr = [jnp.exp(s_ - m_) for s_, m_ in zip(sc, mx)]
                den = [jnp.sum(p_, axis=-1, keepdims=True) for p_ in pr]
                pv = [_dot(p_.astype(BF16), v_) for p_, v_ in zip(pr, vv)]
                for (start, _, slab), pv_, m_, d_ in zip(probs, pv, mx, den):
                    on = pv_ * (1.0 / d_)
                    lse = jnp.broadcast_to(m_ + jnp.log(d_), on.shape)
                    o_scr[slab, rows(start), :] = jnp.where(first_head, on[:span], on[span:])
                    lse_scr[slab, rows(start), :] = jnp.where(first_head, lse[:span], lse[span:])

        n_grp = len(ATT_GROUPS)
        ct = 256

        def combine(j, carry):
            rws = pl.ds(pl.multiple_of(j * ct, ct), ct)
            for sl in range(2):
                ls = [lse_scr[2 * g + sl, rws, :] for g in range(n_grp)]
                mx = functools.reduce(jnp.maximum, ls)
                es = [jnp.exp(l - mx) for l in ls]
                inv = 1.0 / functools.reduce(lambda a, b: a + b, es)
                for g in range(n_grp):
                    slab = 2 * g + sl
                    o_ref[rws, slab * LANES:(slab + 1) * LANES] = (o_scr[slab, rws, :] * (es[g] * inv)).astype(BF16)
            return carry

        lax.fori_loop(0, t_total // ct, combine, 0)


def _att_branch(x, g, w_qkv, gq, gk, e4, maskbias, *, rt=256):
    bsz, t, d = x.shape
    width = gq.shape[1]
    n_slab = width // LANES
    return pl.pallas_call(
        functools.partial(_att_kernel, rt=rt),
        grid=(bsz, t // rt),
        in_specs=[pl.BlockSpec((None, rt, d), lambda b, i: (b, i, 0)), _const_spec((1, d)),
                  _const_spec(w_qkv.shape), _const_spec((1, width)), _const_spec((1, width)),
                  _const_spec(e4.shape), _const_spec(maskbias.shape)],
        out_specs=pl.BlockSpec((None, t, width), lambda b, i: (b, 0, 0)),
        out_shape=jax.ShapeDtypeStruct((bsz, t, width), BF16),
        scratch_shapes=[pltpu.VMEM((3 * n_slab, t, LANES), F32), pltpu.VMEM((n_slab, t, LANES), F32),
                        pltpu.VMEM((n_slab, t, LANES), F32)],
        compiler_params=_params(("arbitrary", "arbitrary")),
        name="att_branch",
    )(x, g, w_qkv, gq, gk, e4, maskbias)


def _att_maskbias(n_heads):
    slopes = jnp.exp2(-8.0 * jnp.arange(1, n_heads + 1, dtype=F32) / n_heads)
    qi = jnp.arange(ATT_SPAN)[:, None]
    kj = jnp.arange(2 * ATT_SPAN)[None, :]
    delta = qi + ATT_SPAN - kj
    valid = (delta >= 0) & (delta <= ATT_SPAN)
    dil = jnp.repeat(jnp.array([d for _, d in ATT_GROUPS], F32), ATT_HEADS_PER_GROUP)
    bias = -(slopes * dil)[:, None, None] * delta.astype(F32)[None]
    return jnp.where(valid[None], bias, NEG_INF).reshape(n_heads // 2, 2 * ATT_SPAN, 2 * ATT_SPAN)


def _head_mean_matrix(n_lanes, head_dim):
    lane_head = jnp.arange(n_lanes) // head_dim
    return ((lane_head[:, None] == lane_head[None, :]).astype(F32) / head_dim).astype(BF16)


def _cumsum_rows(x):
    n = x.shape[0]
    rows = lax.broadcasted_iota(jnp.int32, x.shape, 0)
    s = 1
    while s < n:
        x = x + jnp.where(rows >= s, pltpu.roll(x, s, 0), 0.0)
        s *= 2
    return x


def _dn_kernel(*refs, tiles_per_seq):
    tail_ref, s_scr, *stage_scratch = refs[8:]
    step = pl.program_id(0)

    @pl.when(step == 0)
    def _():
        for ref in stage_scratch:
            ref[...] = jnp.zeros_like(ref)

    @pl.when(step % tiles_per_seq == 0)
    def _():
        tail_ref[...] = jnp.zeros_like(tail_ref)

    @pl.when((step % tiles_per_seq == 1) | (step == 0))
    def _():
        s_scr[...] = jnp.zeros_like(s_scr)

    for parity in range(2):
        pl.when(step % 2 == parity)(functools.partial(_dn_body, *refs, rd=1 - parity, wr=parity))


def _dn_body(x_ref, g_ref, w_ref, cw_ref, aexp_ref, dtb_ref, on_ref, o_ref, tail_ref, s_scr,
             kbq_s, k_s, rhs_s, kd_s, qeg_s, gam_s, gamt_s, z_s, *, rd, wr):
    tt, width = o_ref.shape
    n_heads = width // DN_HEAD_DIM
    c = DN_CHUNK

    u = _rms(x_ref[...], g_ref[...]).astype(BF16)
    parts = {}

    def project(*idx):
        for j in idx:
            parts[j] = _dot(u, w_ref[:, j * MXU_DIM:(j + 1) * MXU_DIM])

    assert 2 * c == LANES and n_heads % 4 == 0
    n_ch = tt // c
    n_pair = n_heads // 2
    row2 = lax.broadcasted_iota(jnp.int32, (c, LANES), 0)
    lane2 = lax.broadcasted_iota(jnp.int32, (c, LANES), 1)
    col2 = lane2 & (c - 1)
    incl = col2 <= row2
    strict = col2 < row2
    lo2 = lane2 < c
    lane2x = lax.broadcasted_iota(jnp.int32, (LANES, LANES), 1)
    lo2x = lane2x < c
    row4 = lax.broadcasted_iota(jnp.int32, (c, 2 * LANES), 0)
    lane4 = lax.broadcasted_iota(jnp.int32, (c, 2 * LANES), 1)
    eye4 = ((lane4 & (c - 1)) == row4).astype(F32)
    blk4 = lane4 // c
    zero_k = jnp.zeros((c, DN_HEAD_DIM), BF16)

    def block_diag(m):
        return jnp.concatenate([jnp.where(blk4 == i, m, 0.0) for i in range(4)], axis=0).astype(BF16)

    def cols(base, h):
        return slice(base * width + h * DN_HEAD_DIM, base * width + (h + 1) * DN_HEAD_DIM)

    def l2n(m):
        return m * lax.rsqrt(jnp.sum(m * m, axis=-1, keepdims=True) + EPS)

    probs = [(ch, h) for ch in range(n_ch) for h in range(n_heads)]
    pairs = [(ch, p) for ch in range(n_ch) for p in range(n_pair)]
    quads = [(ch, qd) for ch in range(n_ch) for qd in range(n_pair // 2)]
    pix = {ph: i for i, ph in enumerate(probs)}

    n_prob = len(probs)
    staged = {name: [None] * n_prob for name in ("kbq", "k", "rhs", "kd", "qeg")}
    common = {}
    ba_piece = 4 * n_pair

    def prep_common():
        ba = parts[ba_piece][:, :LANES]
        common["beta"] = jax.nn.sigmoid(ba)
        glog_all = -aexp_ref[...] * _softplus(ba + dtb_ref[...])
        common["gam"] = [_cumsum_rows(glog_all[ch * c:(ch + 1) * c]) for ch in range(n_ch)]
        common["gam_t"] = [jnp.concatenate([g_, g_], axis=0).T for g_ in common["gam"]]

    def prep_pair(p):
        def conv_silu(section):
            blk = parts[section * n_pair + p]
            cs = slice(section * width + p * MXU_DIM, section * width + (p + 1) * MXU_DIM)
            y = _causal_conv(blk, tail_ref[:, cs], cw_ref[:, cs])
            tail_ref[:, cs] = blk[tt - SUBLANES:tt]
            return y * jax.nn.sigmoid(y)

        qs, ks, vs = conv_silu(0), conv_silu(1), conv_silu(2)
        for ch in range(n_ch):
            rs = slice(ch * c, (ch + 1) * c)
            for sub in range(2):
                h = 2 * p + sub
                ls = slice(sub * DN_HEAD_DIM, (sub + 1) * DN_HEAD_DIM)
                q_ = l2n(qs[rs, ls]) * (DN_HEAD_DIM ** -0.5)
                k_ = l2n(ks[rs, ls])
                beta = common["beta"][rs, h:h + 1]
                gc = common["gam"][ch][:, n_heads + h:n_heads + h + 1]
                eg_ = jnp.exp(gc)
                kb_ = k_ * beta
                i = pix[(ch, h)]
                staged["rhs"][i] = jnp.concatenate([vs[rs, ls] * beta, kb_ * eg_], axis=1).astype(BF16)
                staged["kbq"][i] = jnp.concatenate([kb_, q_], axis=0).astype(BF16)
                staged["kd"][i] = k_ * jnp.exp(gc[c - 1:c, :] - gc)
                staged["qeg"][i] = (q_ * eg_).astype(BF16)
                staged["k"][i] = k_.astype(BF16)

    fill = iter([
        lambda: project(ba_piece, 0, n_pair, 2 * n_pair),
        lambda: (prep_common(), project(1, n_pair + 1, 2 * n_pair + 1)),
        lambda: prep_pair(0),
        lambda: project(2, n_pair + 2, 2 * n_pair + 2),
        lambda: prep_pair(1),
        lambda: project(3, n_pair + 3, 2 * n_pair + 3),
        lambda: prep_pair(2),
        lambda: project(3 * n_pair, 3 * n_pair + 1),
        lambda: prep_pair(3),
        lambda: project(3 * n_pair + 2),
        lambda: project(3 * n_pair + 3),
    ])
    assert n_pair == 4

    def interleave():
        action = next(fill, None)
        if action is not None:
            action()

    kbq = [kbq_s[rd, i] for i in range(n_prob)]
    k = [k_s[rd, i] for i in range(n_prob)]
    rhs = [rhs_s[rd, i] for i in range(n_prob)]
    kd = [kd_s[rd, i] for i in range(n_prob)]
    qeg = [qeg_s[rd, i] for i in range(n_prob)]
    gam_all = [gam_s[rd, ch] for ch in range(n_ch)]
    gam_t = [gamt_s[rd, ch] for ch in range(n_ch)]
    z = z_s[rd]
    gcol = [gam_all[ch][:, n_heads + h:n_heads + h + 1] for ch, h in probs]
    glast = [gc[c - 1:c, :] for gc in gcol]

    def both(lst, ch, p):
        return lst[pix[(ch, 2 * p)]], lst[pix[(ch, 2 * p + 1)]]

    interleave()
    kq = []
    for ch, p in pairs:
        (x0, x1), (k0, k1) = both(kbq, ch, p), both(k, ch, p)
        kq.append(_dot_nt(x0, jnp.concatenate([k0.astype(BF16), zero_k], axis=0))
                  + _dot_nt(x1, jnp.concatenate([zero_k, k1.astype(BF16)], axis=0)))
    interleave()
    decay = []
    for ch, p in pairs:
        g0, g1 = both(gcol, ch, p)
        gr = jnp.where(lo2[0:1], gam_t[ch][n_heads + 2 * p:n_heads + 2 * p + 1, :],
                       gam_t[ch][n_heads + 2 * p + 1:n_heads + 2 * p + 2, :])
        diff = jnp.where(lo2, g0, g1) - gr
        decay.append(jnp.where(incl, jnp.exp(jnp.where(incl, diff, 0.0)), 0.0))
    a2 = [jnp.where(strict, kq_[:c] * d_, 0.0) for kq_, d_ in zip(kq, decay)]
    qk2 = [jnp.where(incl, kq_[c:] * d_, 0.0) for kq_, d_ in zip(kq, decay)]

    a4 = [jnp.concatenate([a2[ch * n_pair + 2 * qd], a2[ch * n_pair + 2 * qd + 1]], axis=1) for ch, qd in quads]
    tinv = [eye4 - a_ for a_ in a4]
    pw = [_dot(a_.astype(BF16), block_diag(a_)) for a_ in a4]
    interleave()
    for _ in range(int(math.log2(c)) - 2):
        r = [_dot(jnp.concatenate([t_, p_], axis=0).astype(BF16), block_diag(p_)) for t_, p_ in zip(tinv, pw)]
        tinv = [t_ + r_[:c] for t_, r_ in zip(tinv, r)]
        pw = [r_[c:] for r_ in r]
        interleave()
    tinv = [t_ + _dot(t_.astype(BF16), block_diag(p_)) for t_, p_ in zip(tinv, pw)]
    interleave()

    def split_rows(m):
        return jnp.concatenate([jnp.where(lo2, m, 0.0), jnp.where(lo2, 0.0, m)], axis=0).astype(BF16)

    sol, lhs_o = [], []
    for i, (ch, p) in enumerate(pairs):
        t2 = tinv[i // 2][:, (i % 2) * LANES:(i % 2 + 1) * LANES]
        r0, r1 = both(rhs, ch, p)
        sol.append(_dot(split_rows(t2), jnp.concatenate([r0, r1], axis=0)))
        kd0, kd1 = both(kd, ch, p)
        kdt = jnp.concatenate([kd0, kd1], axis=0).T
        lhs_o.append(jnp.concatenate([split_rows(qk2[i]), jnp.where(lo2x, kdt, 0.0).astype(BF16),
                                      jnp.where(lo2x, 0.0, kdt).astype(BF16)], axis=0))
    wq = []
    for ch, h in probs:
        s_ = sol[ch * n_pair + h // 2][(h % 2) * c:(h % 2 + 1) * c]
        i = pix[(ch, h)]
        wq.append(jnp.concatenate([s_[:, DN_HEAD_DIM:].astype(BF16), qeg[i]], axis=0))
    interleave()

    for ch in range(n_ch):
        rs = slice(ch * c, (ch + 1) * c)
        state = [s_scr[h] for h in range(n_heads)]
        ws = [_dot(wq[pix[(ch, h)]], state[h].astype(BF16)) for h in range(n_heads)]
        interleave()
        un = [sol[ch * n_pair + h // 2][(h % 2) * c:(h % 2 + 1) * c, :DN_HEAD_DIM] - ws[h][:c] for h in range(n_heads)]
        ou = [_dot(lhs_o[ch * n_pair + p], jnp.concatenate([un[2 * p], un[2 * p + 1]], axis=0).astype(BF16))
              for p in range(n_pair)]
        interleave()
        for h in range(n_heads):
            hs = slice(h * DN_HEAD_DIM, (h + 1) * DN_HEAD_DIM)
            o2 = ou[h // 2]
            sub = h % 2
            s_scr[h] = (state[h] * jnp.exp(glast[pix[(ch, h)]])
                        + o2[2 * c + sub * DN_HEAD_DIM:2 * c + (sub + 1) * DN_HEAD_DIM])
            on = _rms(ws[h][c:] + o2[sub * c:(sub + 1) * c], on_ref[:, hs])
            zz = z[rs, hs]
            o_ref[rs, hs] = (on * (zz * jax.nn.sigmoid(zz))).astype(BF16)

    for action in fill:
        action()
    for ref, vals in ((kbq_s, staged["kbq"]), (k_s, staged["k"]), (rhs_s, staged["rhs"]), (kd_s, staged["kd"]),
                      (qeg_s, staged["qeg"]), (gam_s, common["gam"]), (gamt_s, common["gam_t"])):
        for i, v in enumerate(vals):
            ref[wr, i] = v
    z_s[wr] = jnp.concatenate([parts[3 * n_pair + j] for j in range(n_pair)], axis=1)


def _dn_branch(x, g, w_in, layer, w_block, cw, aexp, dtb, onorm, *, tt=256):
    bsz, t, d = x.shape
    width = onorm.shape[1]
    n_heads = width // DN_HEAD_DIM
    w_cols = 4 * width + 2 * LANES
    tiles_per_seq = t // tt
    n_tiles = bsz * tiles_per_seq
    n_prob = (tt // DN_CHUNK) * n_heads
    c = DN_CHUNK

    def x_map(s):
        j = jnp.minimum(s, n_tiles - 1)
        return j // tiles_per_seq, j % tiles_per_seq, 0

    def o_map(s):
        j = jnp.maximum(s - 1, 0)
        return j // tiles_per_seq, j % tiles_per_seq, 0

    assert tiles_per_seq > 1
    return pl.pallas_call(
        functools.partial(_dn_kernel, tiles_per_seq=tiles_per_seq),
        grid=(n_tiles + 1,),
        in_specs=[pl.BlockSpec((None, tt, d), x_map), _const_spec((1, d)),
                  _col_block_spec(d, w_cols, w_block, layer), _const_spec(cw.shape),
                  _const_spec((1, LANES)), _const_spec((1, LANES)), _const_spec((1, width))],
        out_specs=pl.BlockSpec((None, tt, width), o_map),
        out_shape=jax.ShapeDtypeStruct((bsz, t, width), BF16),
        scratch_shapes=[pltpu.VMEM((SUBLANES, 3 * width), F32),
                        pltpu.VMEM((n_heads, DN_HEAD_DIM, DN_HEAD_DIM), F32),
                        pltpu.VMEM((2, n_prob, 2 * c, DN_HEAD_DIM), BF16),
                        pltpu.VMEM((2, n_prob, c, DN_HEAD_DIM), BF16),
                        pltpu.VMEM((2, n_prob, c, 2 * DN_HEAD_DIM), BF16),
                        pltpu.VMEM((2, n_prob, c, DN_HEAD_DIM), F32),
                        pltpu.VMEM((2, n_prob, c, DN_HEAD_DIM), BF16),
                        pltpu.VMEM((2, tt // c, c, LANES), F32),
                        pltpu.VMEM((2, tt // c, LANES, LANES), F32),
                        pltpu.VMEM((2, tt, width), F32)],
        compiler_params=_params(("arbitrary",)),
        name="dn_branch",
    )(x, g, w_in, cw, aexp, dtb, onorm)


def _merge_kernel(x_ref, g_ref, ya_ref, yb_ref, yc_ref, wm_ref, wa_ref, wb_ref, wc_ref, wo_ref, o_ref):
    x = x_ref[...]
    d = x.shape[1]
    u = _rms(x, g_ref[...]).astype(BF16)
    y = jnp.zeros_like(x)
    for j, (yr, wr) in enumerate(((ya_ref, wa_ref), (yb_ref, wb_ref), (yc_ref, wc_ref))):
        gate = jax.nn.sigmoid(_dot(u, wm_ref[:, j * d:(j + 1) * d]))
        y = y + gate * _dot(yr[...], wr[...])
    o_ref[...] = x + _dot(y.astype(BF16), wo_ref[...])


def _merge(x2, g, ya, yb, yc, wm, wa, wb, wc, wo, *, tm=1024):
    n, d = x2.shape
    row = lambda i: (i, 0)
    return pl.pallas_call(
        _merge_kernel,
        grid=(n // tm,),
        in_specs=[pl.BlockSpec((tm, d), row), _const_spec((1, d)),
                  pl.BlockSpec((tm, ya.shape[1]), row), pl.BlockSpec((tm, yb.shape[1]), row),
                  pl.BlockSpec((tm, yc.shape[1]), row),
                  _const_spec(wm.shape, True), _const_spec(wa.shape, True), _const_spec(wb.shape, True),
                  _const_spec(wc.shape, True), _const_spec(wo.shape, True)],
        out_specs=pl.BlockSpec((tm, d), row),
        out_shape=jax.ShapeDtypeStruct((n, d), F32),
        compiler_params=_params(("arbitrary",)),
        name="merge",
    )(x2, g, ya, yb, yc, wm, wa, wb, wc, wo)


def _block_diag_tiles(w):
    nb, k, _ = w.shape
    per = MXU_DIM // k
    w4 = w.reshape(nb // per, per, k, k)
    eye = jnp.eye(per, dtype=w.dtype)
    return jnp.einsum('tpij,pq->tpiqj', w4, eye).reshape(nb // per, MXU_DIM, MXU_DIM)


def _row(v):
    return v.reshape(1, -1).astype(F32)


def _mixer(x, layer, mix_norm, w_in, rg_conv_w, rg_conv_b, rg_w_r, rg_b_r, rg_w_i, rg_b_i, rg_lambda,
           att_q_norm, att_k_norm, dn_conv_w, dn_a_log, dn_dt_bias, dn_out_norm, w_branch, w_out):
    bsz, t, d = x.shape
    rg_w = rg_conv_w.shape[1]
    att_w = att_q_norm.size
    dn_w = dn_out_norm.size
    dn_h = dn_out_norm.shape[0]
    att_off = 2 * rg_w
    dn_off = att_off + 3 * att_w
    merge_off = dn_off + 4 * dn_w + 2 * dn_h
    dn_cols = 4 * dn_w + 2 * LANES
    assert dn_off % dn_cols == 0 and dn_off + dn_cols <= w_in.shape[2] and 2 * dn_h <= LANES
    w_att = w_in[layer, :, att_off:dn_off]
    w_merge = w_in[layer, :, merge_off:merge_off + 3 * d]
    g = _row(mix_norm)

    lamc = _row(-RG_C * jax.nn.softplus(-rg_lambda.astype(F32)))
    ya = _rg_branch(x, g, w_in, layer, rg_conv_w, _row(rg_conv_b),
                    _block_diag_tiles(rg_w_r).astype(BF16), _row(rg_b_r),
                    _block_diag_tiles(rg_w_i).astype(BF16), _row(rg_b_i), lamc)

    n_att_heads = att_q_norm.shape[0]
    yb = _att_branch(x, g, w_att, _row(att_q_norm), _row(att_k_norm),
                     _head_mean_matrix(2 * LANES, ATT_HEAD_DIM), _att_maskbias(n_att_heads))

    pad = jnp.zeros((LANES - 2 * dn_h,), F32)
    aexp = _row(jnp.concatenate([jnp.zeros((dn_h,), F32), jnp.exp(dn_a_log.astype(F32)), pad]))
    dtb = _row(jnp.concatenate([jnp.zeros((dn_h,), F32), dn_dt_bias.astype(F32), pad]))
    yc = _dn_branch(x, g, w_in, layer, dn_off // dn_cols, dn_conv_w, aexp, dtb, _row(dn_out_norm))

    wa = w_branch[:rg_w]
    wb = w_branch[rg_w:rg_w + att_w]
    wc = w_branch[rg_w + att_w:]
    n = bsz * t
    out = _merge(x.reshape(n, d), g, ya.reshape(n, rg_w), yb.reshape(n, att_w), yc.reshape(n, dn_w),
                 w_merge, wa, wb, wc, w_out)
    return out.reshape(bsz, t, d)


def kernel(x, ffn1_norm, ffn1_w_gate, ffn1_w_up, ffn1_w_down, mix_norm, w_in, rg_conv_w, rg_conv_b, rg_w_r, rg_b_r, rg_w_i, rg_b_i, rg_lambda, att_q_norm, att_k_norm, dn_conv_w, dn_a_log, dn_dt_bias, dn_out_norm, w_branch, w_out, ffn2_norm, ffn2_w_gate, ffn2_w_up, ffn2_w_down):
    bsz, t, d = x.shape
    n = bsz * t
    n_layers = ffn1_norm.shape[0]
    ffn1_w = [w[0].astype(BF16) for w in (ffn1_w_gate, ffn1_w_up, ffn1_w_down)]
    w_in_all = w_in.astype(BF16)
    for l in range(n_layers):
        casts = [(w, l) for w in (w_branch, w_out, ffn2_w_gate, ffn2_w_up, ffn2_w_down)]
        x, (w_branch_b, w_out_b, *ffn2_w) = _ffn(x.reshape(n, d), _row(ffn1_norm[l]), *ffn1_w, casts)
        x = _mixer(x.reshape(bsz, t, d), l, mix_norm[l], w_in_all, rg_conv_w[l], rg_conv_b[l], rg_w_r[l], rg_b_r[l],
                   rg_w_i[l], rg_b_i[l], rg_lambda[l], att_q_norm[l], att_k_norm[l], dn_conv_w[l], dn_a_log[l],
                   dn_dt_bias[l], dn_out_norm[l], w_branch_b, w_out_b)
        casts = [(w, l + 1) for w in (ffn1_w_gate, ffn1_w_up, ffn1_w_down)] if l + 1 < n_layers else []
        x, ffn1_w = _ffn(x.reshape(n, d), _row(ffn2_norm[l]), *ffn2_w, casts)
    return x.reshape(bsz, t, d)
```

```python
import functools
import math

import jax
import jax.numpy as jnp
from jax import lax
from jax.experimental import pallas as pl
from jax.experimental.pallas import tpu as pltpu

F32 = jnp.float32
BF16 = jnp.bfloat16

EPS = 1e-6
NEG_INF = -1e30
RG_C = 8.0
RG_BLOCK_DIM = 64
CONV_WIDTH = 4
ATT_GROUPS = ((128, 1), (512, 4), (2048, 16))
ATT_HEADS_PER_GROUP = 4
ATT_HEAD_DIM = 64
ATT_SPAN = 128
DN_HEAD_DIM = 128
DN_CHUNK = 64

LANES = 128
SUBLANES = 8
MXU_DIM = 256
VMEM_LIMIT_BYTES = 56 * 1024 * 1024


def _rms(x, g):
    return x * lax.rsqrt(jnp.mean(x * x, axis=-1, keepdims=True) + EPS) * g


def _dot(a, b):
    return jnp.dot(a, b, preferred_element_type=F32)


def _dot_nt(a, b):
    return lax.dot_general(a, b, (((1,), (1,)), ((), ())), preferred_element_type=F32)


def _dot_tn(a, b):
    return lax.dot_general(a, b, (((0,), (0,)), ((), ())), preferred_element_type=F32)


def _softplus(z):
    return jnp.maximum(z, 0.0) + jnp.log(1.0 + jnp.exp(-jnp.abs(z)))


def _const_spec(shape, single_buffer=False):
    nd = len(shape)
    mode = pl.Buffered(1) if single_buffer else None
    return pl.BlockSpec(shape, lambda *_: (0,) * nd, pipeline_mode=mode)


def _col_block_spec(rows, cols, col_block, layer):
    return pl.BlockSpec((None, rows, cols), lambda *_: (layer, 0, col_block))


def _params(semantics):
    return pltpu.CompilerParams(dimension_semantics=semantics, vmem_limit_bytes=VMEM_LIMIT_BYTES)


BF16_SUBLANES = 16


def _cast_blocking(rows, n_steps):
    rpb = BF16_SUBLANES
    while rows % rpb or rows // rpb > n_steps:
        rpb += BF16_SUBLANES
        assert rpb <= rows
    return rpb, rows // rpb


def _ffn_kernel(x_ref, g_ref, wg_ref, wu_ref, wd_ref, *rest, f_chunk):
    n_cast = (len(rest) - 1) // 2
    o_ref = rest[n_cast]
    for src, dst in zip(rest[:n_cast], rest[n_cast + 1:]):
        dst[...] = src[...].astype(BF16)
    x = x_ref[...]
    u = _rms(x, g_ref[...]).astype(BF16)
    acc = jnp.zeros_like(x)
    for c in range(wg_ref.shape[1] // f_chunk):
        sl = slice(c * f_chunk, (c + 1) * f_chunk)
        hg = _dot(u, wg_ref[:, sl])
        hu = _dot(u, wu_ref[:, sl])
        h = (hg * jax.nn.sigmoid(hg) * hu).astype(BF16)
        acc = acc + _dot(h, wd_ref[sl, :])
    o_ref[...] = x + 0.5 * acc


def _ffn(x2, g, wg, wu, wd, casts=(), *, tm=1024, f_chunk=256):
    n, d = x2.shape
    f = wg.shape[1]
    n_steps = n // tm
    cast_in, cast_out, cast_shapes = [], [], []
    for stack, layer in casts:
        _, rows, cols = stack.shape
        rpb, nb = _cast_blocking(rows, n_steps)
        cast_in.append(pl.BlockSpec((None, rpb, cols), lambda i, layer=layer, nb=nb: (layer, jnp.minimum(i, nb - 1), 0)))
        cast_out.append(pl.BlockSpec((rpb, cols), lambda i, nb=nb: (jnp.minimum(i, nb - 1), 0)))
        cast_shapes.append(jax.ShapeDtypeStruct((rows, cols), BF16))
    outs = pl.pallas_call(
        functools.partial(_ffn_kernel, f_chunk=f_chunk),
        grid=(n_steps,),
        in_specs=[pl.BlockSpec((tm, d), lambda i: (i, 0)), _const_spec((1, d)),
                  _const_spec((d, f), True), _const_spec((d, f), True), _const_spec((f, d), True)] + cast_in,
        out_specs=[pl.BlockSpec((tm, d), lambda i: (i, 0))] + cast_out,
        out_shape=[jax.ShapeDtypeStruct((n, d), F32)] + cast_shapes,
        compiler_params=_params(("arbitrary",)),
        name="ffn",
    )(x2, g, wg, wu, wd, *[stack for stack, _ in casts])
    return outs[0], list(outs[1:])


def _shift_rows(cur, prev8, s):
    rows = lax.broadcasted_iota(jnp.int32, cur.shape, 0)
    return jnp.where(rows < s, pltpu.roll(prev8, s, 0), pltpu.roll(cur, s, 0))


def _causal_conv(p, tail8, w):
    kw = w.shape[0]
    acc = p * w[kw - 1:kw, :]
    top = p[0:SUBLANES] * w[kw - 1:kw, :]
    for s in range(1, kw):
        wk = w[kw - 1 - s:kw - s, :]
        acc = acc + pltpu.roll(p, s, 0) * wk
        top = top + _shift_rows(p[0:SUBLANES], tail8, s) * wk
    return jnp.concatenate([top, acc[SUBLANES:]], axis=0)


def _segment_perm(tt):
    r = jnp.arange(tt)
    src = (r % SUBLANES) * (tt // SUBLANES) + r // SUBLANES
    return (src[:, None] == jnp.arange(tt)[None, :]).astype(BF16)


def _rg_kernel(x_ref, g_ref, perm_ref, permt_ref, wx_ref, wgate_ref, cw_ref, cb_ref, wr_ref, br_ref, wi_ref,
               bi_ref, lamc_ref, o_ref, tail_ref, carry_ref, a_scr, h_scr):
    rows, width = o_ref.shape
    tt = perm_ref.shape[0]
    n_sub = rows // tt
    n_grp = tt // SUBLANES
    kw = cw_ref.shape[0]

    @pl.when(pl.program_id(1) == 0)
    def _():
        tail_ref[...] = jnp.zeros_like(tail_ref)
        carry_ref[...] = jnp.zeros_like(carry_ref)

    u = _rms(x_ref[...], g_ref[...]).astype(BF16)
    nc = width // MXU_DIM
    rows8 = lax.broadcasted_iota(jnp.int32, (SUBLANES, MXU_DIM), 0)
    up = {}
    tails = {c: tail_ref[:, c * MXU_DIM:(c + 1) * MXU_DIM] for c in range(nc)}
    carries = {c: carry_ref[:, c * MXU_DIM:(c + 1) * MXU_DIM] for c in range(nc)}

    def project(t, c):
        if t not in up:
            up[t] = _dot(perm_ref[...], u[t * tt:(t + 1) * tt]).astype(BF16)
        cs = slice(c * MXU_DIM, (c + 1) * MXU_DIM)
        return _dot(up[t], wx_ref[:, cs]), _dot(up[t], wgate_ref[:, cs])

    def conv(c, px):
        cs = slice(c * MXU_DIM, (c + 1) * MXU_DIM)
        tail = tails[c]
        head = [_shift_rows(px[(n_grp - k) * SUBLANES:(n_grp - k + 1) * SUBLANES],
                            tail[(kw - 1 - k) * SUBLANES:(kw - k) * SUBLANES], 1) for k in range(kw - 1, 0, -1)]
        ext = jnp.concatenate(head + [px], axis=0)
        tails[c] = px[tt - (kw - 1) * SUBLANES:tt]
        xa = cb_ref[:, cs] + ext[0:tt] * cw_ref[0:1, cs]
        for k in range(1, kw):
            xa = xa + ext[k * SUBLANES:k * SUBLANES + tt] * cw_ref[k:k + 1, cs]
        return xa

    def gates(c, xa):
        xab = xa.astype(BF16)
        return _dot(xab, wr_ref[c]), _dot(xab, wi_ref[c])

    def scan(t, c, xa, r_lin, i_lin, pg):
        cs = slice(c * MXU_DIM, (c + 1) * MXU_DIM)
        rt = slice(t * tt, (t + 1) * tt)
        a = jnp.exp(lamc_ref[:, cs] * jax.nn.sigmoid(r_lin + br_ref[:, cs]))
        a_scr[rt, cs] = a
        h_scr[rt, cs] = jnp.sqrt(1.0 - a * a) * (jax.nn.sigmoid(i_lin + bi_ref[:, cs]) * xa)
        hl = jnp.zeros((SUBLANES, MXU_DIM), F32)
        ap = jnp.ones((SUBLANES, MXU_DIM), F32)
        for j in range(n_grp):
            rs = slice(t * tt + j * SUBLANES, t * tt + (j + 1) * SUBLANES)
            aj = a_scr[rs, cs]
            hl = aj * hl + h_scr[rs, cs]
            ap = aj * ap
            h_scr[rs, cs] = hl
            a_scr[rs, cs] = ap
        for s in (1, 2, 4):
            m = rows8 >= s
            hl = jnp.where(m, ap * pltpu.roll(hl, s, 0) + hl, hl)
            ap = jnp.where(m, ap * pltpu.roll(ap, s, 0), ap)
        cin = carries[c]
        seg_end = hl + ap * cin
        carries[c] = jnp.broadcast_to(seg_end[SUBLANES - 1:SUBLANES, :], seg_end.shape)
        seg_in = jnp.where(rows8 < 1, cin, pltpu.roll(seg_end, 1, 0))
        h = (h_scr[rt, cs].reshape(n_grp, SUBLANES, MXU_DIM)
             + a_scr[rt, cs].reshape(n_grp, SUBLANES, MXU_DIM) * seg_in[None])
        return (h.reshape(tt, MXU_DIM) * jax.nn.gelu(pg)).astype(BF16)

    def emit(t, c, y):
        o_ref[t * tt:(t + 1) * tt, c * MXU_DIM:(c + 1) * MXU_DIM] = _dot(permt_ref[...], y).astype(BF16)

    items = [(t, c) for t in range(n_sub) for c in range(nc)]
    proj, xas, gts, ys = {}, {}, {}, {}
    for step in range(len(items) + 3):
        if step < len(items):
            proj[step] = project(*items[step])
        k = step - 1
        if 0 <= k < len(items):
            xas[k] = conv(items[k][1], proj[k][0])
            gts[k] = gates(items[k][1], xas[k])
        k = step - 2
        if 0 <= k < len(items):
            ys[k] = scan(*items[k], xas[k], gts[k][0], gts[k][1], proj[k][1])
        k = step - 3
        if 0 <= k < len(items):
            emit(*items[k], ys[k])
    for c in range(nc):
        tail_ref[:, c * MXU_DIM:(c + 1) * MXU_DIM] = tails[c]
        carry_ref[:, c * MXU_DIM:(c + 1) * MXU_DIM] = carries[c]


def _rg_branch(x, g, w_in, layer, cw, cb, wr_t, br, wi_t, bi, lamc, *, tt=256, n_sub=4):
    bsz, t, d = x.shape
    width = cw.shape[1]
    perm = _segment_perm(tt)
    rows = tt * n_sub
    return pl.pallas_call(
        _rg_kernel,
        grid=(bsz, t // rows),
        in_specs=[pl.BlockSpec((None, rows, d), lambda b, i: (b, i, 0)), _const_spec((1, d)),
                  _const_spec((tt, tt)), _const_spec((tt, tt)),
                  _col_block_spec(d, width, 0, layer), _col_block_spec(d, width, 1, layer),
                  _const_spec(cw.shape), _const_spec((1, width)),
                  _const_spec(wr_t.shape), _const_spec((1, width)),
                  _const_spec(wi_t.shape), _const_spec((1, width)), _const_spec((1, width))],
        out_specs=pl.BlockSpec((None, rows, width), lambda b, i: (b, i, 0)),
        out_shape=jax.ShapeDtypeStruct((bsz, t, width), BF16),
        scratch_shapes=[pltpu.VMEM(((cw.shape[0] - 1) * SUBLANES, width), F32), pltpu.VMEM((SUBLANES, width), F32),
                        pltpu.VMEM((rows, width), F32), pltpu.VMEM((rows, width), F32)],
        compiler_params=_params(("arbitrary", "arbitrary")),
        name="rg_branch",
    )(x, g, perm, perm.T, w_in, w_in, cw, cb, wr_t, br, wi_t, bi, lamc)


ATT_BLOCK_BATCH = 2


def _att_kernel(x_ref, g_ref, w_ref, gq_ref, gk_ref, e4_ref, mb_ref, o_ref, qkv_scr, o_scr, lse_scr, *, rt):
    t_total = o_ref.shape[0]
    n_slab = o_scr.shape[0]
    i = pl.program_id(1)
    row0 = pl.multiple_of(i * rt, rt)
    span = ATT_SPAN

    u = _rms(x_ref[...], g_ref[...]).astype(BF16)
    qkv = _dot(u, w_ref[...])
    width = n_slab * LANES
    e4 = e4_ref[...]
    for s in range(n_slab):
        cs = slice(s * LANES, (s + 1) * LANES)
        qk = jnp.concatenate([qkv[:, cs], qkv[:, width + s * LANES: width + (s + 1) * LANES]], axis=1)
        gain = jnp.concatenate([gq_ref[:, cs] * (ATT_HEAD_DIM ** -0.5), gk_ref[:, cs]], axis=1)
        ms = _dot((qk * qk).astype(BF16), e4)
        qk = qk * lax.rsqrt(ms + EPS) * gain
        qkv_scr[s, pl.ds(row0, rt), :] = qk[:, :LANES]
        qkv_scr[n_slab + s, pl.ds(row0, rt), :] = qk[:, LANES:]
        qkv_scr[2 * n_slab + s, pl.ds(row0, rt), :] = qkv[:, 2 * width + s * LANES: 2 * width + (s + 1) * LANES]

    @pl.when(i == pl.num_programs(1) - 1)
    def _():
        lane = lax.broadcasted_iota(jnp.int32, (span, LANES), 1)
        first_head = lane < ATT_HEAD_DIM

        for g, (window, dil) in enumerate(ATT_GROUPS):
            assert window // dil == span
            nblk = t_total // dil // span
            blocks = [(r, n) for n in range(nblk) for r in range(dil)]

            def rows(start, dil=dil):
                return pl.ds(start, span) if dil == 1 else pl.ds(start, span, stride=dil)

            for b0 in range(0, len(blocks), ATT_BLOCK_BATCH):
                probs = [(r + n * span * dil, n > 0, 2 * g + sl)
                         for r, n in blocks[b0:b0 + ATT_BLOCK_BATCH] for sl in range(2)]
                qs, kk, vv = [], [], []
                for start, has_prev, slab in probs:
                    q2 = qkv_scr[slab, rows(start), :]
                    qs.append(jnp.concatenate([jnp.where(first_head, q2, 0.0), jnp.where(first_head, 0.0, q2)],
                                              axis=0).astype(BF16))
                    kc = qkv_scr[n_slab + slab, rows(start), :]
                    vc = qkv_scr[2 * n_slab + slab, rows(start), :]
                    if has_prev:
                        prev = start - span * dil
                        kc = jnp.concatenate([qkv_scr[n_slab + slab, rows(prev), :], kc], axis=0)
                        vc = jnp.concatenate([qkv_scr[2 * n_slab + slab, rows(prev), :], vc], axis=0)
                    kk.append(kc.astype(BF16))
                    vv.append(vc.astype(BF16))
                sc = [_dot_nt(q_, k_) for q_, k_ in zip(qs, kk)]
                sc = [s_ + (mb_ref[slab] if has_prev else mb_ref[slab, :, span:])
                      for s_, (_, has_prev, slab) in zip(sc, probs)]
                mx = [jnp.max(s_, axis=-1, keepdims=True) for s_ in sc]
                pr = [jnp.exp(s_ - m_) for s_, m_ in zip(sc, mx)]
                den = [jnp.sum(p_, axis=-1, keepdims=True) for p_ in pr]
                pv = [_dot(p_.astype(BF16), v_) for p_, v_ in zip(pr, vv)]
                for (start, _, slab), pv_, m_, d_ in zip(probs, pv, mx, den):
                    on = pv_ * (1.0 / d_)
                    lse = jnp.broadcast_to(m_ + jnp.log(d_), on.shape)
                    o_scr[slab, rows(start), :] = jnp.where(first_head, on[:span], on[span:])
                    lse_scr[slab, rows(start), :] = jnp.where(first_head, lse[:span], lse[span:])

        n_grp = len(ATT_GROUPS)
        ct = 256

        def combine(j, carry):
            rws = pl.ds(pl.multiple_of(j * ct, ct), ct)
            for sl in range(2):
                ls = [lse_scr[2 * g + sl, rws, :] for g in range(n_grp)]
                mx = functools.reduce(jnp.maximum, ls)
                es = [jnp.exp(l - mx) for l in ls]
                inv = 1.0 / functools.reduce(lambda a, b: a + b, es)
                for g in range(n_grp):
                    slab = 2 * g + sl
                    o_ref[rws, slab * LANES:(slab + 1) * LANES] = (o_scr[slab, rws, :] * (es[g] * inv)).astype(BF16)
            return carry

        lax.fori_loop(0, t_total // ct, combine, 0)


def _att_branch(x, g, w_qkv, gq, gk, e4, maskbias, *, rt=256):
    bsz, t, d = x.shape
    width = gq.shape[1]
    n_slab = width // LANES
    return pl.pallas_call(
        functools.partial(_att_kernel, rt=rt),
        grid=(bsz, t // rt),
        in_specs=[pl.BlockSpec((None, rt, d), lambda b, i: (b, i, 0)), _const_spec((1, d)),
                  _const_spec(w_qkv.shape), _const_spec((1, width)), _const_spec((1, width)),
                  _const_spec(e4.shape), _const_spec(maskbias.shape)],
        out_specs=pl.BlockSpec((None, t, width), lambda b, i: (b, 0, 0)),
        out_shape=jax.ShapeDtypeStruct((bsz, t, width), BF16),
        scratch_shapes=[pltpu.VMEM((3 * n_slab, t, LANES), F32), pltpu.VMEM((n_slab, t, LANES), F32),
                        pltpu.VMEM((n_slab, t, LANES), F32)],
        compiler_params=_params(("arbitrary", "arbitrary")),
        name="att_branch",
    )(x, g, w_qkv, gq, gk, e4, maskbias)


def _att_maskbias(n_heads):
    slopes = jnp.exp2(-8.0 * jnp.arange(1, n_heads + 1, dtype=F32) / n_heads)
    qi = jnp.arange(ATT_SPAN)[:, None]
    kj = jnp.arange(2 * ATT_SPAN)[None, :]
    delta = qi + ATT_SPAN - kj
    valid = (delta >= 0) & (delta <= ATT_SPAN)
    dil = jnp.repeat(jnp.array([d for _, d in ATT_GROUPS], F32), ATT_HEADS_PER_GROUP)
    bias = -(slopes * dil)[:, None, None] * delta.astype(F32)[None]
    return jnp.where(valid[None], bias, NEG_INF).reshape(n_heads // 2, 2 * ATT_SPAN, 2 * ATT_SPAN)


def _head_mean_matrix(n_lanes, head_dim):
    lane_head = jnp.arange(n_lanes) // head_dim
    return ((lane_head[:, None] == lane_head[None, :]).astype(F32) / head_dim).astype(BF16)


def _cumsum_rows(x):
    n = x.shape[0]
    rows = lax.broadcasted_iota(jnp.int32, x.shape, 0)
    s = 1
    while s < n:
        x = x + jnp.where(rows >= s, pltpu.roll(x, s, 0), 0.0)
        s *= 2
    return x


def _dn_kernel(*refs, tiles_per_seq):
    tail_ref, s_scr, *stage_scratch = refs[8:]
    step = pl.program_id(0)

    @pl.when(step == 0)
    def _():
        for ref in stage_scratch:
            ref[...] = jnp.zeros_like(ref)

    @pl.when(step % tiles_per_seq == 0)
    def _():
        tail_ref[...] = jnp.zeros_like(tail_ref)

    @pl.when((step % tiles_per_seq == 1) | (step == 0))
    def _():
        s_scr[...] = jnp.zeros_like(s_scr)

    for parity in range(2):
        pl.when(step % 2 == parity)(functools.partial(_dn_body, *refs, rd=1 - parity, wr=parity))


def _dn_body(x_ref, g_ref, w_ref, cw_ref, aexp_ref, dtb_ref, on_ref, o_ref, tail_ref, s_scr,
             kbq_s, k_s, rhs_s, kd_s, qeg_s, gam_s, gamt_s, z_s, *, rd, wr):
    tt, width = o_ref.shape
    n_heads = width // DN_HEAD_DIM
    c = DN_CHUNK

    u = _rms(x_ref[...], g_ref[...]).astype(BF16)
    parts = {}

    def project(*idx):
        for j in idx:
            parts[j] = _dot(u, w_ref[:, j * MXU_DIM:(j + 1) * MXU_DIM])

    assert 2 * c == LANES and n_heads % 4 == 0
    n_ch = tt // c
    n_pair = n_heads // 2
    row2 = lax.broadcasted_iota(jnp.int32, (c, LANES), 0)
    lane2 = lax.broadcasted_iota(jnp.int32, (c, LANES), 1)
    col2 = lane2 & (c - 1)
    incl = col2 <= row2
    strict = col2 < row2
    lo2 = lane2 < c
    lane2x = lax.broadcasted_iota(jnp.int32, (LANES, LANES), 1)
    lo2x = lane2x < c
    row4 = lax.broadcasted_iota(jnp.int32, (c, 2 * LANES), 0)
    lane4 = lax.broadcasted_iota(jnp.int32, (c, 2 * LANES), 1)
    eye4 = ((lane4 & (c - 1)) == row4).astype(F32)
    blk4 = lane4 // c
    zero_k = jnp.zeros((c, DN_HEAD_DIM), BF16)

    def block_diag(m):
        return jnp.concatenate([jnp.where(blk4 == i, m, 0.0) for i in range(4)], axis=0).astype(BF16)

    def cols(base, h):
        return slice(base * width + h * DN_HEAD_DIM, base * width + (h + 1) * DN_HEAD_DIM)

    def l2n(m):
        return m * lax.rsqrt(jnp.sum(m * m, axis=-1, keepdims=True) + EPS)

    probs = [(ch, h) for ch in range(n_ch) for h in range(n_heads)]
    pairs = [(ch, p) for ch in range(n_ch) for p in range(n_pair)]
    quads = [(ch, qd) for ch in range(n_ch) for qd in range(n_pair // 2)]
    pix = {ph: i for i, ph in enumerate(probs)}

    n_prob = len(probs)
    staged = {name: [None] * n_prob for name in ("kbq", "k", "rhs", "kd", "qeg")}
    common = {}
    ba_piece = 4 * n_pair

    def prep_common():
        ba = parts[ba_piece][:, :LANES]
        common["beta"] = jax.nn.sigmoid(ba)
        glog_all = -aexp_ref[...] * _softplus(ba + dtb_ref[...])
        common["gam"] = [_cumsum_rows(glog_all[ch * c:(ch + 1) * c]) for ch in range(n_ch)]
        common["gam_t"] = [jnp.concatenate([g_, g_], axis=0).T for g_ in common["gam"]]

    def prep_pair(p):
        def conv_silu(section):
            blk = parts[section * n_pair + p]
            cs = slice(section * width + p * MXU_DIM, section * width + (p + 1) * MXU_DIM)
            y = _causal_conv(blk, tail_ref[:, cs], cw_ref[:, cs])
            tail_ref[:, cs] = blk[tt - SUBLANES:tt]
            return y * jax.nn.sigmoid(y)

        qs, ks, vs = conv_silu(0), conv_silu(1), conv_silu(2)
        for ch in range(n_ch):
            rs = slice(ch * c, (ch + 1) * c)
            for sub in range(2):
                h = 2 * p + sub
                ls = slice(sub * DN_HEAD_DIM, (sub + 1) * DN_HEAD_DIM)
                q_ = l2n(qs[rs, ls]) * (DN_HEAD_DIM ** -0.5)
                k_ = l2n(ks[rs, ls])
                beta = common["beta"][rs, h:h + 1]
                gc = common["gam"][ch][:, n_heads + h:n_heads + h + 1]
                eg_ = jnp.exp(gc)
                kb_ = k_ * beta
                i = pix[(ch, h)]
                staged["rhs"][i] = jnp.concatenate([vs[rs, ls] * beta, kb_ * eg_], axis=1).astype(BF16)
                staged["kbq"][i] = jnp.concatenate([kb_, q_], axis=0).astype(BF16)
                staged["kd"][i] = k_ * jnp.exp(gc[c - 1:c, :] - gc)
                staged["qeg"][i] = (q_ * eg_).astype(BF16)
                staged["k"][i] = k_.astype(BF16)

    fill = iter([
        lambda: project(ba_piece, 0, n_pair, 2 * n_pair),
        lambda: (prep_common(), project(1, n_pair + 1, 2 * n_pair + 1)),
        lambda: prep_pair(0),
        lambda: project(2, n_pair + 2, 2 * n_pair + 2),
        lambda: prep_pair(1),
        lambda: project(3, n_pair + 3, 2 * n_pair + 3),
        lambda: prep_pair(2),
        lambda: project(3 * n_pair, 3 * n_pair + 1),
        lambda: prep_pair(3),
        lambda: project(3 * n_pair + 2),
        lambda: project(3 * n_pair + 3),
    ])
    assert n_pair == 4

    def interleave():
        action = next(fill, None)
        if action is not None:
            action()

    kbq = [kbq_s[rd, i] for i in range(n_prob)]
    k = [k_s[rd, i] for i in range(n_prob)]
    rhs = [rhs_s[rd, i] for i in range(n_prob)]
    kd = [kd_s[rd, i] for i in range(n_prob)]
    qeg = [qeg_s[rd, i] for i in range(n_prob)]
    gam_all = [gam_s[rd, ch] for ch in range(n_ch)]
    gam_t = [gamt_s[rd, ch] for ch in range(n_ch)]
    z = z_s[rd]
    gcol = [gam_all[ch][:, n_heads + h:n_heads + h + 1] for ch, h in probs]
    glast = [gc[c - 1:c, :] for gc in gcol]

    def both(lst, ch, p):
        return lst[pix[(ch, 2 * p)]], lst[pix[(ch, 2 * p + 1)]]

    interleave()
    kq = []
    for ch, p in pairs:
        (x0, x1), (k0, k1) = both(kbq, ch, p), both(k, ch, p)
        kq.append(_dot_nt(x0, jnp.concatenate([k0.astype(BF16), zero_k], axis=0))
                  + _dot_nt(x1, jnp.concatenate([zero_k, k1.astype(BF16)], axis=0)))
    interleave()
    decay = []
    for ch, p in pairs:
        g0, g1 = both(gcol, ch, p)
        gr = jnp.where(lo2[0:1], gam_t[ch][n_heads + 2 * p:n_heads + 2 * p + 1, :],
                       gam_t[ch][n_heads + 2 * p + 1:n_heads + 2 * p + 2, :])
        diff = jnp.where(lo2, g0, g1) - gr
        decay.append(jnp.where(incl, jnp.exp(jnp.where(incl, diff, 0.0)), 0.0))
    a2 = [jnp.where(strict, kq_[:c] * d_, 0.0) for kq_, d_ in zip(kq, decay)]
    qk2 = [jnp.where(incl, kq_[c:] * d_, 0.0) for kq_, d_ in zip(kq, decay)]

    a4 = [jnp.concatenate([a2[ch * n_pair + 2 * qd], a2[ch * n_pair + 2 * qd + 1]], axis=1) for ch, qd in quads]
    tinv = [eye4 - a_ for a_ in a4]
    pw = [_dot(a_.astype(BF16), block_diag(a_)) for a_ in a4]
    interleave()
    for _ in range(int(math.log2(c)) - 2):
        r = [_dot(jnp.concatenate([t_, p_], axis=0).astype(BF16), block_diag(p_)) for t_, p_ in zip(tinv, pw)]
        tinv = [t_ + r_[:c] for t_, r_ in zip(tinv, r)]
        pw = [r_[c:] for r_ in r]
        interleave()
    tinv = [t_ + _dot(t_.astype(BF16), block_diag(p_)) for t_, p_ in zip(tinv, pw)]
    interleave()

    def split_rows(m):
        return jnp.concatenate([jnp.where(lo2, m, 0.0), jnp.where(lo2, 0.0, m)], axis=0).astype(BF16)

    sol, lhs_o = [], []
    for i, (ch, p) in enumerate(pairs):
        t2 = tinv[i // 2][:, (i % 2) * LANES:(i % 2 + 1) * LANES]
        r0, r1 = both(rhs, ch, p)
        sol.append(_dot(split_rows(t2), jnp.concatenate([r0, r1], axis=0)))
        kd0, kd1 = both(kd, ch, p)
        kdt = jnp.concatenate([kd0, kd1], axis=0).T
        lhs_o.append(jnp.concatenate([split_rows(qk2[i]), jnp.where(lo2x, kdt, 0.0).astype(BF16),
                                      jnp.where(lo2x, 0.0, kdt).astype(BF16)], axis=0))
    wq = []
    for ch, h in probs:
        s_ = sol[ch * n_pair + h // 2][(h % 2) * c:(h % 2 + 1) * c]
        i = pix[(ch, h)]
        wq.append(jnp.concatenate([s_[:, DN_HEAD_DIM:].astype(BF16), qeg[i]], axis=0))
    interleave()

    for ch in range(n_ch):
        rs = slice(ch * c, (ch + 1) * c)
        state = [s_scr[h] for h in range(n_heads)]
        ws = [_dot(wq[pix[(ch, h)]], state[h].astype(BF16)) for h in range(n_heads)]
        interleave()
        un = [sol[ch * n_pair + h // 2][(h % 2) * c:(h % 2 + 1) * c, :DN_HEAD_DIM] - ws[h][:c] for h in range(n_heads)]
        ou = [_dot(lhs_o[ch * n_pair + p], jnp.concatenate([un[2 * p], un[2 * p + 1]], axis=0).astype(BF16))
              for p in range(n_pair)]
        interleave()
        for h in range(n_heads):
            hs = slice(h * DN_HEAD_DIM, (h + 1) * DN_HEAD_DIM)
            o2 = ou[h // 2]
            sub = h % 2
            s_scr[h] = (state[h] * jnp.exp(glast[pix[(ch, h)]])
                        + o2[2 * c + sub * DN_HEAD_DIM:2 * c + (sub + 1) * DN_HEAD_DIM])
            on = _rms(ws[h][c:] + o2[sub * c:(sub + 1) * c], on_ref[:, hs])
            zz = z[rs, hs]
            o_ref[rs, hs] = (on * (zz * jax.nn.sigmoid(zz))).astype(BF16)

    for action in fill:
        action()
    for ref, vals in ((kbq_s, staged["kbq"]), (k_s, staged["k"]), (rhs_s, staged["rhs"]), (kd_s, staged["kd"]),
                      (qeg_s, staged["qeg"]), (gam_s, common["gam"]), (gamt_s, common["gam_t"])):
        for i, v in enumerate(vals):
            ref[wr, i] = v
    z_s[wr] = jnp.concatenate([parts[3 * n_pair + j] for j in range(n_pair)], axis=1)


def _dn_branch(x, g, w_in, layer, w_block, cw, aexp, dtb, onorm, *, tt=256):
    bsz, t, d = x.shape
    width = onorm.shape[1]
    n_heads = width // DN_HEAD_DIM
    w_cols = 4 * width + 2 * LANES
    tiles_per_seq = t // tt
    n_tiles = bsz * tiles_per_seq
    n_prob = (tt // DN_CHUNK) * n_heads
    c = DN_CHUNK

    def x_map(s):
        j = jnp.minimum(s, n_tiles - 1)
        return j // tiles_per_seq, j % tiles_per_seq, 0

    def o_map(s):
        j = jnp.maximum(s - 1, 0)
        return j // tiles_per_seq, j % tiles_per_seq, 0

    assert tiles_per_seq > 1
    return pl.pallas_call(
        functools.partial(_dn_kernel, tiles_per_seq=tiles_per_seq),
        grid=(n_tiles + 1,),
        in_specs=[pl.BlockSpec((None, tt, d), x_map), _const_spec((1, d)),
                  _col_block_spec(d, w_cols, w_block, layer), _const_spec(cw.shape),
                  _const_spec((1, LANES)), _const_spec((1, LANES)), _const_spec((1, width))],
        out_specs=pl.BlockSpec((None, tt, width), o_map),
        out_shape=jax.ShapeDtypeStruct((bsz, t, width), BF16),
        scratch_shapes=[pltpu.VMEM((SUBLANES, 3 * width), F32),
                        pltpu.VMEM((n_heads, DN_HEAD_DIM, DN_HEAD_DIM), F32),
                        pltpu.VMEM((2, n_prob, 2 * c, DN_HEAD_DIM), BF16),
                        pltpu.VMEM((2, n_prob, c, DN_HEAD_DIM), BF16),
                        pltpu.VMEM((2, n_prob, c, 2 * DN_HEAD_DIM), BF16),
                        pltpu.VMEM((2, n_prob, c, DN_HEAD_DIM), F32),
                        pltpu.VMEM((2, n_prob, c, DN_HEAD_DIM), BF16),
                        pltpu.VMEM((2, tt // c, c, LANES), F32),
                        pltpu.VMEM((2, tt // c, LANES, LANES), F32),
                        pltpu.VMEM((2, tt, width), F32)],
        compiler_params=_params(("arbitrary",)),
        name="dn_branch",
    )(x, g, w_in, cw, aexp, dtb, onorm)


def _merge_kernel(x_ref, g_ref, ya_ref, yb_ref, yc_ref, wm_ref, wa_ref, wb_ref, wc_ref, wo_ref, o_ref):
    x = x_ref[...]
    d = x.shape[1]
    u = _rms(x, g_ref[...]).astype(BF16)
    y = jnp.zeros_like(x)
    for j, (yr, wr) in enumerate(((ya_ref, wa_ref), (yb_ref, wb_ref), (yc_ref, wc_ref))):
        gate = jax.nn.sigmoid(_dot(u, wm_ref[:, j * d:(j + 1) * d]))
        y = y + gate * _dot(yr[...], wr[...])
    o_ref[...] = x + _dot(y.astype(BF16), wo_ref[...])


def _merge(x2, g, ya, yb, yc, wm, wa, wb, wc, wo, *, tm=1024):
    n, d = x2.shape
    row = lambda i: (i, 0)
    return pl.pallas_call(
        _merge_kernel,
        grid=(n // tm,),
        in_specs=[pl.BlockSpec((tm, d), row), _const_spec((1, d)),
                  pl.BlockSpec((tm, ya.shape[1]), row), pl.BlockSpec((tm, yb.shape[1]), row),
                  pl.BlockSpec((tm, yc.shape[1]), row),
                  _const_spec(wm.shape, True), _const_spec(wa.shape, True), _const_spec(wb.shape, True),
                  _const_spec(wc.shape, True), _const_spec(wo.shape, True)],
        out_specs=pl.BlockSpec((tm, d), row),
        out_shape=jax.ShapeDtypeStruct((n, d), F32),
        compiler_params=_params(("arbitrary",)),
        name="merge",
    )(x2, g, ya, yb, yc, wm, wa, wb, wc, wo)


def _block_diag_tiles(w):
    nb, k, _ = w.shape
    per = MXU_DIM // k
    w4 = w.reshape(nb // per, per, k, k)
    eye = jnp.eye(per, dtype=w.dtype)
    return jnp.einsum('tpij,pq->tpiqj', w4, eye).reshape(nb // per, MXU_DIM, MXU_DIM)


def _row(v):
    return v.reshape(1, -1).astype(F32)


def _mixer(x, layer, mix_norm, w_in, rg_conv_w, rg_conv_b, rg_w_r, rg_b_r, rg_w_i, rg_b_i, rg_lambda,
           att_q_norm, att_k_norm, dn_conv_w, dn_a_log, dn_dt_bias, dn_out_norm, w_branch, w_out):
    bsz, t, d = x.shape
    rg_w = rg_conv_w.shape[1]
    att_w = att_q_norm.size
    dn_w = dn_out_norm.size
    dn_h = dn_out_norm.shape[0]
    att_off = 2 * rg_w
    dn_off = att_off + 3 * att_w
    merge_off = dn_off + 4 * dn_w + 2 * dn_h
    dn_cols = 4 * dn_w + 2 * LANES
    assert dn_off % dn_cols == 0 and dn_off + dn_cols <= w_in.shape[2] and 2 * dn_h <= LANES
    w_att = w_in[layer, :, att_off:dn_off]
    w_merge = w_in[layer, :, merge_off:merge_off + 3 * d]
    g = _row(mix_norm)

    lamc = _row(-RG_C * jax.nn.softplus(-rg_lambda.astype(F32)))
    ya = _rg_branch(x, g, w_in, layer, rg_conv_w, _row(rg_conv_b),
                    _block_diag_tiles(rg_w_r).astype(BF16), _row(rg_b_r),
                    _block_diag_tiles(rg_w_i).astype(BF16), _row(rg_b_i), lamc)

    n_att_heads = att_q_norm.shape[0]
    yb = _att_branch(x, g, w_att, _row(att_q_norm), _row(att_k_norm),
                     _head_mean_matrix(2 * LANES, ATT_HEAD_DIM), _att_maskbias(n_att_heads))

    pad = jnp.zeros((LANES - 2 * dn_h,), F32)
    aexp = _row(jnp.concatenate([jnp.zeros((dn_h,), F32), jnp.exp(dn_a_log.astype(F32)), pad]))
    dtb = _row(jnp.concatenate([jnp.zeros((dn_h,), F32), dn_dt_bias.astype(F32), pad]))
    yc = _dn_branch(x, g, w_in, layer, dn_off // dn_cols, dn_conv_w, aexp, dtb, _row(dn_out_norm))

    wa = w_branch[:rg_w]
    wb = w_branch[rg_w:rg_w + att_w]
    wc = w_branch[rg_w + att_w:]
    n = bsz * t
    out = _merge(x.reshape(n, d), g, ya.reshape(n, rg_w), yb.reshape(n, att_w), yc.reshape(n, dn_w),
                 w_merge, wa, wb, wc, w_out)
    return out.reshape(bsz, t, d)


def kernel(x, ffn1_norm, ffn1_w_gate, ffn1_w_up, ffn1_w_down, mix_norm, w_in, rg_conv_w, rg_conv_b, rg_w_r, rg_b_r, rg_w_i, rg_b_i, rg_lambda, att_q_norm, att_k_norm, dn_conv_w, dn_a_log, dn_dt_bias, dn_out_norm, w_branch, w_out, ffn2_norm, ffn2_w_gate, ffn2_w_up, ffn2_w_down):
    bsz, t, d = x.shape
    n = bsz * t
    n_layers = ffn1_norm.shape[0]
    ffn1_w = [w[0].astype(BF16) for w in (ffn1_w_gate, ffn1_w_up, ffn1_w_down)]
    w_in_all = w_in.astype(BF16)
    for l in range(n_layers):
        casts = [(w, l) for w in (w_branch, w_out, ffn2_w_gate, ffn2_w_up, ffn2_w_down)]
        x, (w_branch_b, w_out_b, *ffn2_w) = _ffn(x.reshape(n, d), _row(ffn1_norm[l]), *ffn1_w, casts)
        x = _mixer(x.reshape(bsz, t, d), l, mix_norm[l], w_in_all, rg_conv_w[l], rg_conv_b[l], rg_w_r[l], rg_b_r[l],
                   rg_w_i[l], rg_b_i[l], rg_lambda[l], att_q_norm[l], att_k_norm[l], dn_conv_w[l], dn_a_log[l],
                   dn_dt_bias[l], dn_out_norm[l], w_branch_b, w_out_b)
        casts = [(w, l + 1) for w in (ffn1_w_gate, ffn1_w_up, ffn1_w_down)] if l + 1 < n_layers else []
        x, ffn1_w = _ffn(x.reshape(n, d), _row(ffn2_norm[l]), *ffn2_w, casts)
    return x.reshape(bsz, t, d)
```

```python
import functools
import math

import jax
import jax.numpy as jnp
from jax import lax
from jax.experimental import pallas as pl
from jax.experimental.pallas import tpu as pltpu

F32 = jnp.float32
BF16 = jnp.bfloat16

EPS = 1e-6
NEG_INF = -1e30
RG_C = 8.0
RG_BLOCK_DIM = 64
CONV_WIDTH = 4
ATT_GROUPS = ((128, 1), (512, 4), (2048, 16))
ATT_HEADS_PER_GROUP = 4
ATT_HEAD_DIM = 64
ATT_SPAN = 128
DN_HEAD_DIM = 128
DN_CHUNK = 64

LANES = 128
SUBLANES = 8
MXU_DIM = 256
VMEM_LIMIT_BYTES = 60 * 1024 * 1024


def _rms(x, g):
    return x * lax.rsqrt(jnp.mean(x * x, axis=-1, keepdims=True) + EPS) * g


def _dot(a, b):
    return jnp.dot(a, b, preferred_element_type=F32)


def _dot_nt(a, b):
    return lax.dot_general(a, b, (((1,), (1,)), ((), ())), preferred_element_type=F32)


def _dot_tn(a, b):
    return lax.dot_general(a, b, (((0,), (0,)), ((), ())), preferred_element_type=F32)


def _softplus(z):
    return jnp.maximum(z, 0.0) + jnp.log(1.0 + jnp.exp(-jnp.abs(z)))


def _const_spec(shape, single_buffer=False):
    nd = len(shape)
    mode = pl.Buffered(1) if single_buffer else None
    return pl.BlockSpec(shape, lambda *_: (0,) * nd, pipeline_mode=mode)


def _col_block_spec(rows, cols, col_block, layer, single_buffer=False):
    mode = pl.Buffered(1) if single_buffer else None
    return pl.BlockSpec((None, rows, cols), lambda *_: (layer, 0, col_block), pipeline_mode=mode)


def _params(semantics):
    return pltpu.CompilerParams(dimension_semantics=semantics, vmem_limit_bytes=VMEM_LIMIT_BYTES)


BF16_SUBLANES = 16


def _cast_blocking(rows, n_steps):
    rpb = BF16_SUBLANES
    while rows % rpb or rows // rpb > n_steps:
        rpb += BF16_SUBLANES
        assert rpb <= rows
    return rpb, rows // rpb


def _ffn_kernel(x_ref, g_ref, wg_ref, wu_ref, wd_ref, *rest, f_chunk):
    n_cast = (len(rest) - 1) // 2
    o_ref = rest[n_cast]
    for src, dst in zip(rest[:n_cast], rest[n_cast + 1:]):
        dst[...] = src[...].astype(BF16)
    x = x_ref[...]
    u = _rms(x, g_ref[...]).astype(BF16)
    acc = jnp.zeros_like(x)
    for c in range(wg_ref.shape[1] // f_chunk):
        sl = slice(c * f_chunk, (c + 1) * f_chunk)
        hg = _dot(u, wg_ref[:, sl])
        hu = _dot(u, wu_ref[:, sl])
        h = (hg * jax.nn.sigmoid(hg) * hu).astype(BF16)
        acc = acc + _dot(h, wd_ref[sl, :])
    o_ref[...] = x + 0.5 * acc


def _ffn(x2, g, wg, wu, wd, casts=(), *, tm=1024, f_chunk=256):
    n, d = x2.shape
    f = wg.shape[1]
    n_steps = n // tm
    cast_in, cast_out, cast_shapes = [], [], []
    for stack, layer in casts:
        _, rows, cols = stack.shape
        rpb, nb = _cast_blocking(rows, n_steps)
        cast_in.append(pl.BlockSpec((None, rpb, cols), lambda i, layer=layer, nb=nb: (layer, jnp.minimum(i, nb - 1), 0)))
        cast_out.append(pl.BlockSpec((rpb, cols), lambda i, nb=nb: (jnp.minimum(i, nb - 1), 0)))
        cast_shapes.append(jax.ShapeDtypeStruct((rows, cols), BF16))
    outs = pl.pallas_call(
        functools.partial(_ffn_kernel, f_chunk=f_chunk),
        grid=(n_steps,),
        in_specs=[pl.BlockSpec((tm, d), lambda i: (i, 0)), _const_spec((1, d)),
                  _const_spec((d, f), True), _const_spec((d, f), True), _const_spec((f, d), True)] + cast_in,
        out_specs=[pl.BlockSpec((tm, d), lambda i: (i, 0))] + cast_out,
        out_shape=[jax.ShapeDtypeStruct((n, d), F32)] + cast_shapes,
        compiler_params=_params(("arbitrary",)),
        name="ffn",
    )(x2, g, wg, wu, wd, *[stack for stack, _ in casts])
    return outs[0], list(outs[1:])


def _shift_rows(cur, prev8, s):
    rows = lax.broadcasted_iota(jnp.int32, cur.shape, 0)
    return jnp.where(rows < s, pltpu.roll(prev8, s, 0), pltpu.roll(cur, s, 0))


def _causal_conv(p, tail8, w):
    kw = w.shape[0]
    acc = p * w[kw - 1:kw, :]
    top = p[0:SUBLANES] * w[kw - 1:kw, :]
    for s in range(1, kw):
        wk = w[kw - 1 - s:kw - s, :]
        acc = acc + pltpu.roll(p, s, 0) * wk
        top = top + _shift_rows(p[0:SUBLANES], tail8, s) * wk
    return jnp.concatenate([top, acc[SUBLANES:]], axis=0)


def _segment_perm(tt):
    r = jnp.arange(tt)
    src = (r % SUBLANES) * (tt // SUBLANES) + r // SUBLANES
    return (src[:, None] == jnp.arange(tt)[None, :]).astype(BF16)


def _rg_kernel(x_ref, g_ref, perm_ref, permt_ref, wx_ref, wgate_ref, cw_ref, cb_ref, wr_ref, br_ref, wi_ref,
               bi_ref, lamc_ref, o_ref, tail_ref, carry_ref, a_scr, h_scr):
    rows, width = o_ref.shape
    tt = perm_ref.shape[0]
    n_sub = rows // tt
    n_grp = tt // SUBLANES
    kw = cw_ref.shape[0]

    @pl.when(pl.program_id(1) == 0)
    def _():
        tail_ref[...] = jnp.zeros_like(tail_ref)
        carry_ref[...] = jnp.zeros_like(carry_ref)

    u = _rms(x_ref[...], g_ref[...]).astype(BF16)
    nc = width // MXU_DIM
    rows8 = lax.broadcasted_iota(jnp.int32, (SUBLANES, MXU_DIM), 0)
    up = {}
    tails = {c: tail_ref[:, c * MXU_DIM:(c + 1) * MXU_DIM] for c in range(nc)}
    carries = {c: carry_ref[:, c * MXU_DIM:(c + 1) * MXU_DIM] for c in range(nc)}

    def project(t, c):
        if t not in up:
            up[t] = _dot(perm_ref[...], u[t * tt:(t + 1) * tt]).astype(BF16)
        cs = slice(c * MXU_DIM, (c + 1) * MXU_DIM)
        return _dot(up[t], wx_ref[:, cs]), _dot(up[t], wgate_ref[:, cs])

    def conv(c, px):
        cs = slice(c * MXU_DIM, (c + 1) * MXU_DIM)
        tail = tails[c]
        head = [_shift_rows(px[(n_grp - k) * SUBLANES:(n_grp - k + 1) * SUBLANES],
                            tail[(kw - 1 - k) * SUBLANES:(kw - k) * SUBLANES], 1) for k in range(kw - 1, 0, -1)]
        ext = jnp.concatenate(head + [px], axis=0)
        tails[c] = px[tt - (kw - 1) * SUBLANES:tt]
        xa = cb_ref[:, cs] + ext[0:tt] * cw_ref[0:1, cs]
        for k in range(1, kw):
            xa = xa + ext[k * SUBLANES:k * SUBLANES + tt] * cw_ref[k:k + 1, cs]
        return xa

    def gates(c, xa):
        xab = xa.astype(BF16)
        return _dot(xab, wr_ref[c]), _dot(xab, wi_ref[c])

    def scan(t, c, xa, r_lin, i_lin, pg):
        cs = slice(c * MXU_DIM, (c + 1) * MXU_DIM)
        rt = slice(t * tt, (t + 1) * tt)
        a = jnp.exp(lamc_ref[:, cs] * jax.nn.sigmoid(r_lin + br_ref[:, cs]))
        a_scr[rt, cs] = a
        h_scr[rt, cs] = jnp.sqrt(1.0 - a * a) * (jax.nn.sigmoid(i_lin + bi_ref[:, cs]) * xa)
        hl = jnp.zeros((SUBLANES, MXU_DIM), F32)
        ap = jnp.ones((SUBLANES, MXU_DIM), F32)
        for j in range(n_grp):
            rs = slice(t * tt + j * SUBLANES, t * tt + (j + 1) * SUBLANES)
            aj = a_scr[rs, cs]
            hl = aj * hl + h_scr[rs, cs]
            ap = aj * ap
            h_scr[rs, cs] = hl
            a_scr[rs, cs] = ap
        for s in (1, 2, 4):
            m = rows8 >= s
            hl = jnp.where(m, ap * pltpu.roll(hl, s, 0) + hl, hl)
            ap = jnp.where(m, ap * pltpu.roll(ap, s, 0), ap)
        cin = carries[c]
        seg_end = hl + ap * cin
        carries[c] = jnp.broadcast_to(seg_end[SUBLANES - 1:SUBLANES, :], seg_end.shape)
        seg_in = jnp.where(rows8 < 1, cin, pltpu.roll(seg_end, 1, 0))
        h = (h_scr[rt, cs].reshape(n_grp, SUBLANES, MXU_DIM)
             + a_scr[rt, cs].reshape(n_grp, SUBLANES, MXU_DIM) * seg_in[None])
        return (h.reshape(tt, MXU_DIM) * jax.nn.gelu(pg)).astype(BF16)

    def emit(t, c, y):
        o_ref[t * tt:(t + 1) * tt, c * MXU_DIM:(c + 1) * MXU_DIM] = _dot(permt_ref[...], y).astype(BF16)

    items = [(t, c) for t in range(n_sub) for c in range(nc)]
    proj, xas, gts, ys = {}, {}, {}, {}
    for step in range(len(items) + 3):
        if step < len(items):
            proj[step] = project(*items[step])
        k = step - 1
        if 0 <= k < len(items):
            xas[k] = conv(items[k][1], proj[k][0])
            gts[k] = gates(items[k][1], xas[k])
        k = step - 2
        if 0 <= k < len(items):
            ys[k] = scan(*items[k], xas[k], gts[k][0], gts[k][1], proj[k][1])
        k = step - 3
        if 0 <= k < len(items):
            emit(*items[k], ys[k])
    for c in range(nc):
        tail_ref[:, c * MXU_DIM:(c + 1) * MXU_DIM] = tails[c]
        carry_ref[:, c * MXU_DIM:(c + 1) * MXU_DIM] = carries[c]


def _rg_branch(x, g, w_in, layer, cw, cb, wr_t, br, wi_t, bi, lamc, *, tt=256, n_sub=4):
    bsz, t, d = x.shape
    width = cw.shape[1]
    perm = _segment_perm(tt)
    rows = tt * n_sub
    return pl.pallas_call(
        _rg_kernel,
        grid=(bsz, t // rows),
        in_specs=[pl.BlockSpec((None, rows, d), lambda b, i: (b, i, 0)), _const_spec((1, d)),
                  _const_spec((tt, tt)), _const_spec((tt, tt)),
                  _col_block_spec(d, width, 0, layer), _col_block_spec(d, width, 1, layer),
                  _const_spec(cw.shape), _const_spec((1, width)),
                  _const_spec(wr_t.shape), _const_spec((1, width)),
                  _const_spec(wi_t.shape), _const_spec((1, width)), _const_spec((1, width))],
        out_specs=pl.BlockSpec((None, rows, width), lambda b, i: (b, i, 0)),
        out_shape=jax.ShapeDtypeStruct((bsz, t, width), BF16),
        scratch_shapes=[pltpu.VMEM(((cw.shape[0] - 1) * SUBLANES, width), F32), pltpu.VMEM((SUBLANES, width), F32),
                        pltpu.VMEM((rows, width), F32), pltpu.VMEM((rows, width), F32)],
        compiler_params=_params(("arbitrary", "arbitrary")),
        name="rg_branch",
    )(x, g, perm, perm.T, w_in, w_in, cw, cb, wr_t, br, wi_t, bi, lamc)


ATT_BLOCK_BATCH = 4


def _att_kernel(x_ref, g_ref, *rest, rt, n_w):
    w_refs = rest[:n_w]
    gq_ref, gk_ref, e4_ref, mb_ref, o_ref, qkv_scr, o_scr, lse_scr = rest[n_w:]
    t_total = o_ref.shape[0]
    n_slab = o_scr.shape[0]
    i = pl.program_id(1)
    row0 = pl.multiple_of(i * rt, rt)
    span = ATT_SPAN

    u = _rms(x_ref[...], g_ref[...]).astype(BF16)
    qkv = jnp.concatenate([_dot(u, w_ref[...]) for w_ref in w_refs], axis=1)
    width = n_slab * LANES
    e4 = e4_ref[...]
    for s in range(n_slab):
        cs = slice(s * LANES, (s + 1) * LANES)
        qk = jnp.concatenate([qkv[:, cs], qkv[:, width + s * LANES: width + (s + 1) * LANES]], axis=1)
        gain = jnp.concatenate([gq_ref[:, cs] * (ATT_HEAD_DIM ** -0.5), gk_ref[:, cs]], axis=1)
        ms = _dot((qk * qk).astype(BF16), e4)
        qk = qk * lax.rsqrt(ms + EPS) * gain
        qkv_scr[s, pl.ds(row0, rt), :] = qk[:, :LANES]
        qkv_scr[n_slab + s, pl.ds(row0, rt), :] = qk[:, LANES:]
        qkv_scr[2 * n_slab + s, pl.ds(row0, rt), :] = qkv[:, 2 * width + s * LANES: 2 * width + (s + 1) * LANES]

    @pl.when(i == pl.num_programs(1) - 1)
    def _():
        lane = lax.broadcasted_iota(jnp.int32, (span, LANES), 1)
        first_head = lane < ATT_HEAD_DIM

        for g, (window, dil) in enumerate(ATT_GROUPS):
            assert window // dil == span
            nblk = t_total // dil // span
            blocks = [(r, n) for n in range(nblk) for r in range(dil)]

            def rows(start, dil=dil):
                return pl.ds(start, span) if dil == 1 else pl.ds(start, span, stride=dil)

            for b0 in range(0, len(blocks), ATT_BLOCK_BATCH):
                probs = [(r + n * span * dil, n > 0, 2 * g + sl)
                         for r, n in blocks[b0:b0 + ATT_BLOCK_BATCH] for sl in range(2)]
                qs, kk, vv = [], [], []
                for start, has_prev, slab in probs:
                    q2 = qkv_scr[slab, rows(start), :]
                    qs.append(jnp.concatenate([jnp.where(first_head, q2, 0.0), jnp.where(first_head, 0.0, q2)],
                                              axis=0).astype(BF16))
                    kc = qkv_scr[n_slab + slab, rows(start), :]
                    vc = qkv_scr[2 * n_slab + slab, rows(start), :]
                    if has_prev:
                        prev = start - span * dil
                        kc = jnp.concatenate([qkv_scr[n_slab + slab, rows(prev), :], kc], axis=0)
                        vc = jnp.concatenate([qkv_scr[2 * n_slab + slab, rows(prev), :], vc], axis=0)
                    kk.append(kc.astype(BF16))
                    vv.append(vc.astype(BF16))
                sc = [_dot_nt(q_, k_) for q_, k_ in zip(qs, kk)]
                sc = [s_ + (mb_ref[slab] if has_prev else mb_ref[slab, :, span:])
                      for s_, (_, has_prev, slab) in zip(sc, probs)]
                mx = [jnp.max(s_, axis=-1, keepdims=True) for s_ in sc]
                pr = [jnp.exp(s_ - m_) for s_, m_ in zip(sc, mx)]
                den = [jnp.sum(p_, axis=-1, keepdims=True) for p_ in pr]
                pv = [_dot(p_.astype(BF16), v_) for p_, v_ in zip(pr, vv)]
                for (start, _, slab), pv_, m_, d_ in zip(probs, pv, mx, den):
                    on = pv_ * (1.0 / d_)
                    lse = jnp.broadcast_to(m_ + jnp.log(d_), on.shape)
                    o_scr[slab, rows(start), :] = jnp.where(first_head, on[:span], on[span:])
                    lse_scr[slab, rows(start), :] = jnp.where(first_head, lse[:span], lse[span:])

        n_grp = len(ATT_GROUPS)
        ct = 256

        def combine(j, carry):
            rws = pl.ds(pl.multiple_of(j * ct, ct), ct)
            for sl in range(2):
                ls = [lse_scr[2 * g + sl, rws, :] for g in range(n_grp)]
                mx = functools.reduce(jnp.maximum, ls)
                es = [jnp.exp(l - mx) for l in ls]
                inv = 1.0 / functools.reduce(lambda a, b: a + b, es)
                for g in range(n_grp):
                    slab = 2 * g + sl
                    o_ref[rws, slab * LANES:(slab + 1) * LANES] = (o_scr[slab, rws, :] * (es[g] * inv)).astype(BF16)
            return carry

        lax.fori_loop(0, t_total // ct, combine, 0)


def _att_branch(x, g, w_in, layer, w_off, gq, gk, e4, maskbias, *, rt=512):
    bsz, t, d = x.shape
    width = gq.shape[1]
    n_slab = width // LANES
    assert w_off % MXU_DIM == 0 and (3 * width) % MXU_DIM == 0
    n_w = 3 * width // MXU_DIM
    return pl.pallas_call(
        functools.partial(_att_kernel, rt=rt, n_w=n_w),
        grid=(bsz, t // rt),
        in_specs=[pl.BlockSpec((None, rt, d), lambda b, i: (b, i, 0)), _const_spec((1, d))]
                 + [_col_block_spec(d, MXU_DIM, w_off // MXU_DIM + j, layer, True) for j in range(n_w)]
                 + [_const_spec((1, width)), _const_spec((1, width)),
                    _const_spec(e4.shape), _const_spec(maskbias.shape, True)],
        out_specs=pl.BlockSpec((None, t, width), lambda b, i: (b, 0, 0)),
        out_shape=jax.ShapeDtypeStruct((bsz, t, width), BF16),
        scratch_shapes=[pltpu.VMEM((3 * n_slab, t, LANES), F32), pltpu.VMEM((n_slab, t, LANES), F32),
                        pltpu.VMEM((n_slab, t, LANES), F32)],
        compiler_params=_params(("arbitrary", "arbitrary")),
        name="att_branch",
    )(x, g, *([w_in] * n_w), gq, gk, e4, maskbias)


def _att_maskbias(n_heads):
    slopes = jnp.exp2(-8.0 * jnp.arange(1, n_heads + 1, dtype=F32) / n_heads)
    qi = jnp.arange(ATT_SPAN)[:, None]
    kj = jnp.arange(2 * ATT_SPAN)[None, :]
    delta = qi + ATT_SPAN - kj
    valid = (delta >= 0) & (delta <= ATT_SPAN)
    dil = jnp.repeat(jnp.array([d for _, d in ATT_GROUPS], F32), ATT_HEADS_PER_GROUP)
    bias = -(slopes * dil)[:, None, None] * delta.astype(F32)[None]
    return jnp.where(valid[None], bias, NEG_INF).reshape(n_heads // 2, 2 * ATT_SPAN, 2 * ATT_SPAN)


def _head_mean_matrix(n_lanes, head_dim):
    lane_head = jnp.arange(n_lanes) // head_dim
    return ((lane_head[:, None] == lane_head[None, :]).astype(F32) / head_dim).astype(BF16)


def _cumsum_rows(x):
    n = x.shape[0]
    rows = lax.broadcasted_iota(jnp.int32, x.shape, 0)
    s = 1
    while s < n:
        x = x + jnp.where(rows >= s, pltpu.roll(x, s, 0), 0.0)
        s *= 2
    return x


def _dn_kernel(*refs, tiles_per_seq):
    tail_ref, s_scr, *stage_scratch = refs[8:]
    step = pl.program_id(0)

    @pl.when(step == 0)
    def _():
        for ref in stage_scratch:
            ref[...] = jnp.zeros_like(ref)

    @pl.when(step % tiles_per_seq == 0)
    def _():
        tail_ref[...] = jnp.zeros_like(tail_ref)

    @pl.when((step % tiles_per_seq == 1) | (step == 0))
    def _():
        s_scr[...] = jnp.zeros_like(s_scr)

    for parity in range(2):
        pl.when(step % 2 == parity)(functools.partial(_dn_body, *refs, rd=1 - parity, wr=parity))


def _dn_body(x_ref, g_ref, w_ref, cw_ref, aexp_ref, dtb_ref, on_ref, o_ref, tail_ref, s_scr,
             kbq_s, k_s, rhs_s, kd_s, qeg_s, gam_s, gamt_s, z_s, *, rd, wr):
    tt, width = o_ref.shape
    n_heads = width // DN_HEAD_DIM
    c = DN_CHUNK

    u = _rms(x_ref[...], g_ref[...]).astype(BF16)
    parts = {}

    def project(*idx):
        for j in idx:
            parts[j] = _dot(u, w_ref[:, j * MXU_DIM:(j + 1) * MXU_DIM])

    assert 2 * c == LANES and n_heads % 4 == 0
    n_ch = tt // c
    n_pair = n_heads // 2
    row2 = lax.broadcasted_iota(jnp.int32, (c, LANES), 0)
    lane2 = lax.broadcasted_iota(jnp.int32, (c, LANES), 1)
    col2 = lane2 & (c - 1)
    incl = col2 <= row2
    strict = col2 < row2
    lo2 = lane2 < c
    lane2x = lax.broadcasted_iota(jnp.int32, (LANES, LANES), 1)
    lo2x = lane2x < c
    row4 = lax.broadcasted_iota(jnp.int32, (c, 2 * LANES), 0)
    lane4 = lax.broadcasted_iota(jnp.int32, (c, 2 * LANES), 1)
    eye4 = ((lane4 & (c - 1)) == row4).astype(F32)
    blk4 = lane4 // c
    zero_k = jnp.zeros((c, DN_HEAD_DIM), BF16)

    def block_diag(m):
        return jnp.concatenate([jnp.where(blk4 == i, m, 0.0) for i in range(4)], axis=0).astype(BF16)

    def cols(base, h):
        return slice(base * width + h * DN_HEAD_DIM, base * width + (h + 1) * DN_HEAD_DIM)

    def l2n(m):
        return m * lax.rsqrt(jnp.sum(m * m, axis=-1, keepdims=True) + EPS)

    probs = [(ch, h) for ch in range(n_ch) for h in range(n_heads)]
    pairs = [(ch, p) for ch in range(n_ch) for p in range(n_pair)]
    quads = [(ch, qd) for ch in range(n_ch) for qd in range(n_pair // 2)]
    pix = {ph: i for i, ph in enumerate(probs)}

    n_prob = len(probs)
    staged = {name: [None] * n_prob for name in ("kbq", "k", "rhs", "kd", "qeg")}
    common = {}
    ba_piece = 4 * n_pair

    def prep_common():
        ba = parts[ba_piece][:, :LANES]
        common["beta"] = jax.nn.sigmoid(ba)
        glog_all = -aexp_ref[...] * _softplus(ba + dtb_ref[...])
        common["gam"] = [_cumsum_rows(glog_all[ch * c:(ch + 1) * c]) for ch in range(n_ch)]
        common["gam_t"] = [jnp.concatenate([g_, g_], axis=0).T for g_ in common["gam"]]

    def prep_pair(p):
        def conv_silu(section):
            blk = parts[section * n_pair + p]
            cs = slice(section * width + p * MXU_DIM, section * width + (p + 1) * MXU_DIM)
            y = _causal_conv(blk, tail_ref[:, cs], cw_ref[:, cs])
            tail_ref[:, cs] = blk[tt - SUBLANES:tt]
            return y * jax.nn.sigmoid(y)

        qs, ks, vs = conv_silu(0), conv_silu(1), conv_silu(2)
        for ch in range(n_ch):
            rs = slice(ch * c, (ch + 1) * c)
            for sub in range(2):
                h = 2 * p + sub
                ls = slice(sub * DN_HEAD_DIM, (sub + 1) * DN_HEAD_DIM)
                q_ = l2n(qs[rs, ls]) * (DN_HEAD_DIM ** -0.5)
                k_ = l2n(ks[rs, ls])
                beta = common["beta"][rs, h:h + 1]
                gc = common["gam"][ch][:, n_heads + h:n_heads + h + 1]
                eg_ = jnp.exp(gc)
                kb_ = k_ * beta
                i = pix[(ch, h)]
                staged["rhs"][i] = jnp.concatenate([vs[rs, ls] * beta, kb_ * eg_], axis=1).astype(BF16)
                staged["kbq"][i] = jnp.concatenate([kb_, q_], axis=0).astype(BF16)
                staged["kd"][i] = k_ * jnp.exp(gc[c - 1:c, :] - gc)
                staged["qeg"][i] = (q_ * eg_).astype(BF16)
                staged["k"][i] = k_.astype(BF16)

    fill = iter([
        lambda: project(ba_piece, 0, n_pair, 2 * n_pair),
        lambda: (prep_common(), project(1, n_pair + 1, 2 * n_pair + 1)),
        lambda: prep_pair(0),
        lambda: project(2, n_pair + 2, 2 * n_pair + 2),
        lambda: prep_pair(1),
        lambda: project(3, n_pair + 3, 2 * n_pair + 3),
        lambda: prep_pair(2),
        lambda: project(3 * n_pair, 3 * n_pair + 1),
        lambda: prep_pair(3),
        lambda: project(3 * n_pair + 2),
        lambda: project(3 * n_pair + 3),
    ])
    assert n_pair == 4

    def interleave():
        action = next(fill, None)
        if action is not None:
            action()

    kbq = [kbq_s[rd, i] for i in range(n_prob)]
    k = [k_s[rd, i] for i in range(n_prob)]
    rhs = [rhs_s[rd, i] for i in range(n_prob)]
    kd = [kd_s[rd, i] for i in range(n_prob)]
    qeg = [qeg_s[rd, i] for i in range(n_prob)]
    gam_all = [gam_s[rd, ch] for ch in range(n_ch)]
    gam_t = [gamt_s[rd, ch] for ch in range(n_ch)]
    z = z_s[rd]
    gcol = [gam_all[ch][:, n_heads + h:n_heads + h + 1] for ch, h in probs]
    glast = [gc[c - 1:c, :] for gc in gcol]

    def both(lst, ch, p):
        return lst[pix[(ch, 2 * p)]], lst[pix[(ch, 2 * p + 1)]]

    interleave()
    kq = []
    for ch, p in pairs:
        (x0, x1), (k0, k1) = both(kbq, ch, p), both(k, ch, p)
        kq.append(_dot_nt(x0, jnp.concatenate([k0.astype(BF16), zero_k], axis=0))
                  + _dot_nt(x1, jnp.concatenate([zero_k, k1.astype(BF16)], axis=0)))
    interleave()
    decay = []
    for ch, p in pairs:
        g0, g1 = both(gcol, ch, p)
        gr = jnp.where(lo2[0:1], gam_t[ch][n_heads + 2 * p:n_heads + 2 * p + 1, :],
                       gam_t[ch][n_heads + 2 * p + 1:n_heads + 2 * p + 2, :])
        diff = jnp.where(lo2, g0, g1) - gr
        decay.append(jnp.where(incl, jnp.exp(jnp.where(incl, diff, 0.0)), 0.0))
    a2 = [jnp.where(strict, kq_[:c] * d_, 0.0) for kq_, d_ in zip(kq, decay)]
    qk2 = [jnp.where(incl, kq_[c:] * d_, 0.0) for kq_, d_ in zip(kq, decay)]

    a4 = [jnp.concatenate([a2[ch * n_pair + 2 * qd], a2[ch * n_pair + 2 * qd + 1]], axis=1) for ch, qd in quads]
    tinv = [eye4 - a_ for a_ in a4]
    pw = [_dot(a_.astype(BF16), block_diag(a_)) for a_ in a4]
    interleave()
    for _ in range(int(math.log2(c)) - 2):
        r = [_dot(jnp.concatenate([t_, p_], axis=0).astype(BF16), block_diag(p_)) for t_, p_ in zip(tinv, pw)]
        tinv = [t_ + r_[:c] for t_, r_ in zip(tinv, r)]
        pw = [r_[c:] for r_ in r]
        interleave()
    tinv = [t_ + _dot(t_.astype(BF16), block_diag(p_)) for t_, p_ in zip(tinv, pw)]
    interleave()

    def split_rows(m):
        return jnp.concatenate([jnp.where(lo2, m, 0.0), jnp.where(lo2, 0.0, m)], axis=0).astype(BF16)

    sol, lhs_o = [], []
    for i, (ch, p) in enumerate(pairs):
        t2 = tinv[i // 2][:, (i % 2) * LANES:(i % 2 + 1) * LANES]
        r0, r1 = both(rhs, ch, p)
        sol.append(_dot(split_rows(t2), jnp.concatenate([r0, r1], axis=0)))
        kd0, kd1 = both(kd, ch, p)
        kdt = jnp.concatenate([kd0, kd1], axis=0).T
        lhs_o.append(jnp.concatenate([split_rows(qk2[i]), jnp.where(lo2x, kdt, 0.0).astype(BF16),
                                      jnp.where(lo2x, 0.0, kdt).astype(BF16)], axis=0))
    wq = []
    for ch, h in probs:
        s_ = sol[ch * n_pair + h // 2][(h % 2) * c:(h % 2 + 1) * c]
        i = pix[(ch, h)]
        wq.append(jnp.concatenate([s_[:, DN_HEAD_DIM:].astype(BF16), qeg[i]], axis=0))
    interleave()

    for ch in range(n_ch):
        rs = slice(ch * c, (ch + 1) * c)
        state = [s_scr[h] for h in range(n_heads)]
        ws = [_dot(wq[pix[(ch, h)]], state[h].astype(BF16)) for h in range(n_heads)]
        interleave()
        un = [sol[ch * n_pair + h // 2][(h % 2) * c:(h % 2 + 1) * c, :DN_HEAD_DIM] - ws[h][:c] for h in range(n_heads)]
        ou = [_dot(lhs_o[ch * n_pair + p], jnp.concatenate([un[2 * p], un[2 * p + 1]], axis=0).astype(BF16))
              for p in range(n_pair)]
        interleave()
        for h in range(n_heads):
            hs = slice(h * DN_HEAD_DIM, (h + 1) * DN_HEAD_DIM)
            o2 = ou[h // 2]
            sub = h % 2
            s_scr[h] = (state[h] * jnp.exp(glast[pix[(ch, h)]])
                        + o2[2 * c + sub * DN_HEAD_DIM:2 * c + (sub + 1) * DN_HEAD_DIM])
            on = _rms(ws[h][c:] + o2[sub * c:(sub + 1) * c], on_ref[:, hs])
            zz = z[rs, hs]
            o_ref[rs, hs] = (on * (zz * jax.nn.sigmoid(zz))).astype(BF16)

    for action in fill:
        action()
    for ref, vals in ((kbq_s, staged["kbq"]), (k_s, staged["k"]), (rhs_s, staged["rhs"]), (kd_s, staged["kd"]),
                      (qeg_s, staged["qeg"]), (gam_s, common["gam"]), (gamt_s, common["gam_t"])):
        for i, v in enumerate(vals):
            ref[wr, i] = v
    z_s[wr] = jnp.concatenate([parts[3 * n_pair + j] for j in range(n_pair)], axis=1)


def _dn_branch(x, g, w_in, layer, w_block, cw, aexp, dtb, onorm, *, tt=256):
    bsz, t, d = x.shape
    width = onorm.shape[1]
    n_heads = width // DN_HEAD_DIM
    w_cols = 4 * width + 2 * LANES
    tiles_per_seq = t // tt
    n_tiles = bsz * tiles_per_seq
    n_prob = (tt // DN_CHUNK) * n_heads
    c = DN_CHUNK

    def x_map(s):
        j = jnp.minimum(s, n_tiles - 1)
        return j // tiles_per_seq, j % tiles_per_seq, 0

    def o_map(s):
        j = jnp.maximum(s - 1, 0)
        return j // tiles_per_seq, j % tiles_per_seq, 0

    assert tiles_per_seq > 1
    return pl.pallas_call(
        functools.partial(_dn_kernel, tiles_per_seq=tiles_per_seq),
        grid=(n_tiles + 1,),
        in_specs=[pl.BlockSpec((None, tt, d), x_map), _const_spec((1, d)),
                  _col_block_spec(d, w_cols, w_block, layer), _const_spec(cw.shape),
                  _const_spec((1, LANES)), _const_spec((1, LANES)), _const_spec((1, width))],
        out_specs=pl.BlockSpec((None, tt, width), o_map),
        out_shape=jax.ShapeDtypeStruct((bsz, t, width), BF16),
        scratch_shapes=[pltpu.VMEM((SUBLANES, 3 * width), F32),
                        pltpu.VMEM((n_heads, DN_HEAD_DIM, DN_HEAD_DIM), F32),
                        pltpu.VMEM((2, n_prob, 2 * c, DN_HEAD_DIM), BF16),
                        pltpu.VMEM((2, n_prob, c, DN_HEAD_DIM), BF16),
                        pltpu.VMEM((2, n_prob, c, 2 * DN_HEAD_DIM), BF16),
                        pltpu.VMEM((2, n_prob, c, DN_HEAD_DIM), F32),
                        pltpu.VMEM((2, n_prob, c, DN_HEAD_DIM), BF16),
                        pltpu.VMEM((2, tt // c, c, LANES), F32),
                        pltpu.VMEM((2, tt // c, LANES, LANES), F32),
                        pltpu.VMEM((2, tt, width), F32)],
        compiler_params=_params(("arbitrary",)),
        name="dn_branch",
    )(x, g, w_in, cw, aexp, dtb, onorm)


def _merge_kernel(x_ref, g_ref, ya_ref, yb_ref, yc_ref, wm_ref, wa_ref, wb_ref, wc_ref, wo_ref, o_ref):
    x = x_ref[...]
    d = x.shape[1]
    u = _rms(x, g_ref[...]).astype(BF16)
    y = jnp.zeros_like(x)
    for j, (yr, wr) in enumerate(((ya_ref, wa_ref), (yb_ref, wb_ref), (yc_ref, wc_ref))):
        gate = jax.nn.sigmoid(_dot(u, wm_ref[:, j * d:(j + 1) * d]))
        y = y + gate * _dot(yr[...], wr[...])
    o_ref[...] = x + _dot(y.astype(BF16), wo_ref[...])


def _merge(x2, g, ya, yb, yc, wm, wa, wb, wc, wo, *, tm=1024):
    n, d = x2.shape
    row = lambda i: (i, 0)
    return pl.pallas_call(
        _merge_kernel,
        grid=(n // tm,),
        in_specs=[pl.BlockSpec((tm, d), row), _const_spec((1, d)),
                  pl.BlockSpec((tm, ya.shape[1]), row), pl.BlockSpec((tm, yb.shape[1]), row),
                  pl.BlockSpec((tm, yc.shape[1]), row),
                  _const_spec(wm.shape, True), _const_spec(wa.shape, True), _const_spec(wb.shape, True),
                  _const_spec(wc.shape, True), _const_spec(wo.shape, True)],
        out_specs=pl.BlockSpec((tm, d), row),
        out_shape=jax.ShapeDtypeStruct((n, d), F32),
        compiler_params=_params(("arbitrary",)),
        name="merge",
    )(x2, g, ya, yb, yc, wm, wa, wb, wc, wo)


def _block_diag_tiles(w):
    nb, k, _ = w.shape
    per = MXU_DIM // k
    w4 = w.reshape(nb // per, per, k, k)
    eye = jnp.eye(per, dtype=w.dtype)
    return jnp.einsum('tpij,pq->tpiqj', w4, eye).reshape(nb // per, MXU_DIM, MXU_DIM)


def _row(v):
    return v.reshape(1, -1).astype(F32)


def _mixer(x, layer, mix_norm, w_in, rg_conv_w, rg_conv_b, rg_w_r, rg_b_r, rg_w_i, rg_b_i, rg_lambda,
           att_q_norm, att_k_norm, dn_conv_w, dn_a_log, dn_dt_bias, dn_out_norm, w_branch, w_out):
    bsz, t, d = x.shape
    rg_w = rg_conv_w.shape[1]
    att_w = att_q_norm.size
    dn_w = dn_out_norm.size
    dn_h = dn_out_norm.shape[0]
    att_off = 2 * rg_w
    dn_off = att_off + 3 * att_w
    merge_off = dn_off + 4 * dn_w + 2 * dn_h
    dn_cols = 4 * dn_w + 2 * LANES
    assert dn_off % dn_cols == 0 and dn_off + dn_cols <= w_in.shape[2] and 2 * dn_h <= LANES
    w_merge = w_in[layer, :, merge_off:merge_off + 3 * d]
    g = _row(mix_norm)

    lamc = _row(-RG_C * jax.nn.softplus(-rg_lambda.astype(F32)))
    ya = _rg_branch(x, g, w_in, layer, rg_conv_w, _row(rg_conv_b),
                    _block_diag_tiles(rg_w_r).astype(BF16), _row(rg_b_r),
                    _block_diag_tiles(rg_w_i).astype(BF16), _row(rg_b_i), lamc)

    n_att_heads = att_q_norm.shape[0]
    yb = _att_branch(x, g, w_in, layer, att_off, _row(att_q_norm), _row(att_k_norm),
                     _head_mean_matrix(2 * LANES, ATT_HEAD_DIM), _att_maskbias(n_att_heads))

    pad = jnp.zeros((LANES - 2 * dn_h,), F32)
    aexp = _row(jnp.concatenate([jnp.zeros((dn_h,), F32), jnp.exp(dn_a_log.astype(F32)), pad]))
    dtb = _row(jnp.concatenate([jnp.zeros((dn_h,), F32), dn_dt_bias.astype(F32), pad]))
    yc = _dn_branch(x, g, w_in, layer, dn_off // dn_cols, dn_conv_w, aexp, dtb, _row(dn_out_norm))

    wa = w_branch[:rg_w]
    wb = w_branch[rg_w:rg_w + att_w]
    wc = w_branch[rg_w + att_w:]
    n = bsz * t
    out = _merge(x.reshape(n, d), g, ya.reshape(n, rg_w), yb.reshape(n, att_w), yc.reshape(n, dn_w),
                 w_merge, wa, wb, wc, w_out)
    return out.reshape(bsz, t, d)


def kernel(x, ffn1_norm, ffn1_w_gate, ffn1_w_up, ffn1_w_down, mix_norm, w_in, rg_conv_w, rg_conv_b, rg_w_r, rg_b_r, rg_w_i, rg_b_i, rg_lambda, att_q_norm, att_k_norm, dn_conv_w, dn_a_log, dn_dt_bias, dn_out_norm, w_branch, w_out, ffn2_norm, ffn2_w_gate, ffn2_w_up, ffn2_w_down):
    bsz, t, d = x.shape
    n = bsz * t
    n_layers = ffn1_norm.shape[0]
    ffn1_w = [w[0].astype(BF16) for w in (ffn1_w_gate, ffn1_w_up, ffn1_w_down)]
    w_in_all = w_in.astype(BF16)
    for l in range(n_layers):
        casts = [(w, l) for w in (w_branch, w_out, ffn2_w_gate, ffn2_w_up, ffn2_w_down)]
        x, (w_branch_b, w_out_b, *ffn2_w) = _ffn(x.reshape(n, d), _row(ffn1_norm[l]), *ffn1_w, casts)
        x = _mixer(x.reshape(bsz, t, d), l, mix_norm[l], w_in_all, rg_conv_w[l], rg_conv_b[l], rg_w_r[l], rg_b_r[l],
                   rg_w_i[l], rg_b_i[l], rg_lambda[l], att_q_norm[l], att_k_norm[l], dn_conv_w[l], dn_a_log[l],
                   dn_dt_bias[l], dn_out_norm[l], w_branch_b, w_out_b)
        casts = [(w, l + 1) for w in (ffn1_w_gate, ffn1_w_up, ffn1_w_down)] if l + 1 < n_layers else []
        x, ffn1_w = _ffn(x.reshape(n, d), _row(ffn2_norm[l]), *ffn2_w, casts)
    return x.reshape(bsz, t, d)
```

```python
import functools
import math

import jax
import jax.numpy as jnp
from jax import lax
from jax.experimental import pallas as pl
from jax.experimental.pallas import tpu as pltpu

F32 = jnp.float32
BF16 = jnp.bfloat16

EPS = 1e-6
NEG_INF = -1e30
RG_C = 8.0
RG_BLOCK_DIM = 64
CONV_WIDTH = 4
ATT_GROUPS = ((128, 1), (512, 4), (2048, 16))
ATT_HEADS_PER_GROUP = 4
ATT_HEAD_DIM = 64
ATT_SPAN = 128
DN_HEAD_DIM = 128
DN_CHUNK = 64

LANES = 128
SUBLANES = 8
MXU_DIM = 256
VMEM_LIMIT_BYTES = 60 * 1024 * 1024


def _rms(x, g):
    return x * lax.rsqrt(jnp.mean(x * x, axis=-1, keepdims=True) + EPS) * g


def _dot(a, b):
    return jnp.dot(a, b, preferred_element_type=F32)


def _dot_nt(a, b):
    return lax.dot_general(a, b, (((1,), (1,)), ((), ())), preferred_element_type=F32)


def _dot_tn(a, b):
    return lax.dot_general(a, b, (((0,), (0,)), ((), ())), preferred_element_type=F32)


def _softplus(z):
    return jnp.maximum(z, 0.0) + jnp.log(1.0 + jnp.exp(-jnp.abs(z)))


def _const_spec(shape, single_buffer=False):
    nd = len(shape)
    mode = pl.Buffered(1) if single_buffer else None
    return pl.BlockSpec(shape, lambda *_: (0,) * nd, pipeline_mode=mode)


def _col_block_spec(rows, cols, col_block, layer, single_buffer=False):
    mode = pl.Buffered(1) if single_buffer else None
    return pl.BlockSpec((None, rows, cols), lambda *_: (layer, 0, col_block), pipeline_mode=mode)


def _params(semantics):
    return pltpu.CompilerParams(dimension_semantics=semantics, vmem_limit_bytes=VMEM_LIMIT_BYTES)


BF16_SUBLANES = 16


def _cast_blocking(rows, n_steps):
    rpb = BF16_SUBLANES
    while rows % rpb or rows // rpb > n_steps:
        rpb += BF16_SUBLANES
        assert rpb <= rows
    return rpb, rows // rpb


def _ffn_kernel(x_ref, g_ref, wg_ref, wu_ref, wd_ref, *rest, f_chunk):
    n_cast = (len(rest) - 1) // 2
    o_ref = rest[n_cast]
    for src, dst in zip(rest[:n_cast], rest[n_cast + 1:]):
        dst[...] = src[...].astype(BF16)
    x = x_ref[...]
    u = _rms(x, g_ref[...]).astype(BF16)
    acc = jnp.zeros_like(x)
    for c in range(wg_ref.shape[1] // f_chunk):
        sl = slice(c * f_chunk, (c + 1) * f_chunk)
        hg = _dot(u, wg_ref[:, sl])
        hu = _dot(u, wu_ref[:, sl])
        h = (hg * jax.nn.sigmoid(hg) * hu).astype(BF16)
        acc = acc + _dot(h, wd_ref[sl, :])
    o_ref[...] = x + 0.5 * acc


def _ffn(x2, g, wg, wu, wd, casts=(), *, tm=1024, f_chunk=256):
    n, d = x2.shape
    f = wg.shape[1]
    n_steps = n // tm
    cast_in, cast_out, cast_shapes = [], [], []
    for stack, layer in casts:
        _, rows, cols = stack.shape
        rpb, nb = _cast_blocking(rows, n_steps)
        cast_in.append(pl.BlockSpec((None, rpb, cols), lambda i, layer=layer, nb=nb: (layer, jnp.minimum(i, nb - 1), 0)))
        cast_out.append(pl.BlockSpec((rpb, cols), lambda i, nb=nb: (jnp.minimum(i, nb - 1), 0)))
        cast_shapes.append(jax.ShapeDtypeStruct((rows, cols), BF16))
    outs = pl.pallas_call(
        functools.partial(_ffn_kernel, f_chunk=f_chunk),
        grid=(n_steps,),
        in_specs=[pl.BlockSpec((tm, d), lambda i: (i, 0)), _const_spec((1, d)),
                  _const_spec((d, f), True), _const_spec((d, f), True), _const_spec((f, d), True)] + cast_in,
        out_specs=[pl.BlockSpec((tm, d), lambda i: (i, 0))] + cast_out,
        out_shape=[jax.ShapeDtypeStruct((n, d), F32)] + cast_shapes,
        compiler_params=_params(("arbitrary",)),
        name="ffn",
    )(x2, g, wg, wu, wd, *[stack for stack, _ in casts])
    return outs[0], list(outs[1:])


def _shift_rows(cur, prev8, s):
    rows = lax.broadcasted_iota(jnp.int32, cur.shape, 0)
    return jnp.where(rows < s, pltpu.roll(prev8, s, 0), pltpu.roll(cur, s, 0))


def _causal_conv(p, tail8, w):
    kw = w.shape[0]
    acc = p * w[kw - 1:kw, :]
    top = p[0:SUBLANES] * w[kw - 1:kw, :]
    for s in range(1, kw):
        wk = w[kw - 1 - s:kw - s, :]
        acc = acc + pltpu.roll(p, s, 0) * wk
        top = top + _shift_rows(p[0:SUBLANES], tail8, s) * wk
    return jnp.concatenate([top, acc[SUBLANES:]], axis=0)


def _segment_perm(tt):
    r = jnp.arange(tt)
    src = (r % SUBLANES) * (tt // SUBLANES) + r // SUBLANES
    return (src[:, None] == jnp.arange(tt)[None, :]).astype(BF16)


def _rg_kernel(x_ref, g_ref, perm_ref, permt_ref, wx_ref, wgate_ref, cw_ref, cb_ref, wr_ref, br_ref, wi_ref,
               bi_ref, lamc_ref, o_ref, tail_ref, carry_ref, a_scr, h_scr):
    rows, width = o_ref.shape
    tt = perm_ref.shape[0]
    n_sub = rows // tt
    n_grp = tt // SUBLANES
    kw = cw_ref.shape[0]

    @pl.when(pl.program_id(1) == 0)
    def _():
        tail_ref[...] = jnp.zeros_like(tail_ref)
        carry_ref[...] = jnp.zeros_like(carry_ref)

    u = _rms(x_ref[...], g_ref[...]).astype(BF16)
    nc = width // MXU_DIM
    rows8 = lax.broadcasted_iota(jnp.int32, (SUBLANES, MXU_DIM), 0)
    up = {}
    tails = {c: tail_ref[:, c * MXU_DIM:(c + 1) * MXU_DIM] for c in range(nc)}
    carries = {c: carry_ref[:, c * MXU_DIM:(c + 1) * MXU_DIM] for c in range(nc)}

    def project(t, c):
        if t not in up:
            up[t] = _dot(perm_ref[...], u[t * tt:(t + 1) * tt]).astype(BF16)
        cs = slice(c * MXU_DIM, (c + 1) * MXU_DIM)
        return _dot(up[t], wx_ref[:, cs]), _dot(up[t], wgate_ref[:, cs])

    def conv(c, px):
        cs = slice(c * MXU_DIM, (c + 1) * MXU_DIM)
        tail = tails[c]
        head = [_shift_rows(px[(n_grp - k) * SUBLANES:(n_grp - k + 1) * SUBLANES],
                            tail[(kw - 1 - k) * SUBLANES:(kw - k) * SUBLANES], 1) for k in range(kw - 1, 0, -1)]
        ext = jnp.concatenate(head + [px], axis=0)
        tails[c] = px[tt - (kw - 1) * SUBLANES:tt]
        xa = cb_ref[:, cs] + ext[0:tt] * cw_ref[0:1, cs]
        for k in range(1, kw):
            xa = xa + ext[k * SUBLANES:k * SUBLANES + tt] * cw_ref[k:k + 1, cs]
        return xa

    def gates(c, xa):
        xab = xa.astype(BF16)
        return _dot(xab, wr_ref[c]), _dot(xab, wi_ref[c])

    def scan(t, c, xa, r_lin, i_lin, pg):
        cs = slice(c * MXU_DIM, (c + 1) * MXU_DIM)
        rt = slice(t * tt, (t + 1) * tt)
        a = jnp.exp(lamc_ref[:, cs] * jax.nn.sigmoid(r_lin + br_ref[:, cs]))
        a_scr[rt, cs] = a
        h_scr[rt, cs] = jnp.sqrt(1.0 - a * a) * (jax.nn.sigmoid(i_lin + bi_ref[:, cs]) * xa)
        hl = jnp.zeros((SUBLANES, MXU_DIM), F32)
        ap = jnp.ones((SUBLANES, MXU_DIM), F32)
        for j in range(n_grp):
            rs = slice(t * tt + j * SUBLANES, t * tt + (j + 1) * SUBLANES)
            aj = a_scr[rs, cs]
            hl = aj * hl + h_scr[rs, cs]
            ap = aj * ap
            h_scr[rs, cs] = hl
            a_scr[rs, cs] = ap
        for s in (1, 2, 4):
            m = rows8 >= s
            hl = jnp.where(m, ap * pltpu.roll(hl, s, 0) + hl, hl)
            ap = jnp.where(m, ap * pltpu.roll(ap, s, 0), ap)
        cin = carries[c]
        seg_end = hl + ap * cin
        carries[c] = jnp.broadcast_to(seg_end[SUBLANES - 1:SUBLANES, :], seg_end.shape)
        seg_in = jnp.where(rows8 < 1, cin, pltpu.roll(seg_end, 1, 0))
        h = (h_scr[rt, cs].reshape(n_grp, SUBLANES, MXU_DIM)
             + a_scr[rt, cs].reshape(n_grp, SUBLANES, MXU_DIM) * seg_in[None])
        return (h.reshape(tt, MXU_DIM) * jax.nn.gelu(pg)).astype(BF16)

    def emit(t, c, y):
        o_ref[t * tt:(t + 1) * tt, c * MXU_DIM:(c + 1) * MXU_DIM] = _dot(permt_ref[...], y).astype(BF16)

    items = [(t, c) for t in range(n_sub) for c in range(nc)]
    proj, xas, gts, ys = {}, {}, {}, {}
    for step in range(len(items) + 3):
        if step < len(items):
            proj[step] = project(*items[step])
        k = step - 1
        if 0 <= k < len(items):
            xas[k] = conv(items[k][1], proj[k][0])
            gts[k] = gates(items[k][1], xas[k])
        k = step - 2
        if 0 <= k < len(items):
            ys[k] = scan(*items[k], xas[k], gts[k][0], gts[k][1], proj[k][1])
        k = step - 3
        if 0 <= k < len(items):
            emit(*items[k], ys[k])
    for c in range(nc):
        tail_ref[:, c * MXU_DIM:(c + 1) * MXU_DIM] = tails[c]
        carry_ref[:, c * MXU_DIM:(c + 1) * MXU_DIM] = carries[c]


def _rg_branch(x, g, w_in, layer, cw, cb, wr_t, br, wi_t, bi, lamc, *, tt=256, n_sub=8):
    bsz, t, d = x.shape
    width = cw.shape[1]
    perm = _segment_perm(tt)
    rows = tt * n_sub
    return pl.pallas_call(
        _rg_kernel,
        grid=(bsz, t // rows),
        in_specs=[pl.BlockSpec((None, rows, d), lambda b, i: (b, i, 0)), _const_spec((1, d)),
                  _const_spec((tt, tt)), _const_spec((tt, tt)),
                  _col_block_spec(d, width, 0, layer), _col_block_spec(d, width, 1, layer),
                  _const_spec(cw.shape), _const_spec((1, width)),
                  _const_spec(wr_t.shape), _const_spec((1, width)),
                  _const_spec(wi_t.shape), _const_spec((1, width)), _const_spec((1, width))],
        out_specs=pl.BlockSpec((None, rows, width), lambda b, i: (b, i, 0)),
        out_shape=jax.ShapeDtypeStruct((bsz, t, width), BF16),
        scratch_shapes=[pltpu.VMEM(((cw.shape[0] - 1) * SUBLANES, width), F32), pltpu.VMEM((SUBLANES, width), F32),
                        pltpu.VMEM((rows, width), F32), pltpu.VMEM((rows, width), F32)],
        compiler_params=_params(("arbitrary", "arbitrary")),
        name="rg_branch",
    )(x, g, perm, perm.T, w_in, w_in, cw, cb, wr_t, br, wi_t, bi, lamc)


ATT_BLOCK_BATCH = 4


def _att_kernel(x_ref, g_ref, *rest, rt, n_w):
    w_refs = rest[:n_w]
    gq_ref, gk_ref, e4_ref, mb_ref, o_ref, qkv_scr, o_scr, lse_scr = rest[n_w:]
    t_total = o_ref.shape[0]
    n_slab = o_scr.shape[0]
    i = pl.program_id(1)
    row0 = pl.multiple_of(i * rt, rt)
    span = ATT_SPAN

    u = _rms(x_ref[...], g_ref[...]).astype(BF16)
    qkv = jnp.concatenate([_dot(u, w_ref[...]) for w_ref in w_refs], axis=1)
    width = n_slab * LANES
    e4 = e4_ref[...]
    for s in range(n_slab):
        cs = slice(s * LANES, (s + 1) * LANES)
        qk = jnp.concatenate([qkv[:, cs], qkv[:, width + s * LANES: width + (s + 1) * LANES]], axis=1)
        gain = jnp.concatenate([gq_ref[:, cs] * (ATT_HEAD_DIM ** -0.5), gk_ref[:, cs]], axis=1)
        ms = _dot((qk * qk).astype(BF16), e4)
        qk = qk * lax.rsqrt(ms + EPS) * gain
        qkv_scr[s, pl.ds(row0, rt), :] = qk[:, :LANES]
        qkv_scr[n_slab + s, pl.ds(row0, rt), :] = qk[:, LANES:]
        qkv_scr[2 * n_slab + s, pl.ds(row0, rt), :] = qkv[:, 2 * width + s * LANES: 2 * width + (s + 1) * LANES]

    @pl.when(i == pl.num_programs(1) - 1)
    def _():
        lane = lax.broadcasted_iota(jnp.int32, (span, LANES), 1)
        first_head = lane < ATT_HEAD_DIM

        for g, (window, dil) in enumerate(ATT_GROUPS):
            assert window // dil == span
            nblk = t_total // dil // span
            blocks = [(r, n) for n in range(nblk) for r in range(dil)]

            def rows(start, dil=dil):
                return pl.ds(start, span) if dil == 1 else pl.ds(start, span, stride=dil)

            for b0 in range(0, len(blocks), ATT_BLOCK_BATCH):
                probs = [(r + n * span * dil, n > 0, 2 * g + sl)
                         for r, n in blocks[b0:b0 + ATT_BLOCK_BATCH] for sl in range(2)]
                qs, kk, vv = [], [], []
                for start, has_prev, slab in probs:
                    q2 = qkv_scr[slab, rows(start), :]
                    qs.append(jnp.concatenate([jnp.where(first_head, q2, 0.0), jnp.where(first_head, 0.0, q2)],
                                              axis=0).astype(BF16))
                    kc = qkv_scr[n_slab + slab, rows(start), :]
                    vc = qkv_scr[2 * n_slab + slab, rows(start), :]
                    if has_prev:
                        prev = start - span * dil
                        kc = jnp.concatenate([qkv_scr[n_slab + slab, rows(prev), :], kc], axis=0)
                        vc = jnp.concatenate([qkv_scr[2 * n_slab + slab, rows(prev), :], vc], axis=0)
                    kk.append(kc.astype(BF16))
                    vv.append(vc.astype(BF16))
                sc = [_dot_nt(q_, k_) for q_, k_ in zip(qs, kk)]
                sc = [s_ + (mb_ref[slab] if has_prev else mb_ref[slab, :, span:])
                      for s_, (_, has_prev, slab) in zip(sc, probs)]
                mx = [jnp.max(s_, axis=-1, keepdims=True) for s_ in sc]
                pr = [jnp.exp(s_ - m_) for s_, m_ in zip(sc, mx)]
                den = [jnp.sum(p_, axis=-1, keepdims=True) for p_ in pr]
                pv = [_dot(p_.astype(BF16), v_) for p_, v_ in zip(pr, vv)]
                for (start, _, slab), pv_, m_, d_ in zip(probs, pv, mx, den):
                    on = pv_ * (1.0 / d_)
                    lse = jnp.broadcast_to(m_ + jnp.log(d_), on.shape)
                    o_scr[slab, rows(start), :] = jnp.where(first_head, on[:span], on[span:])
                    lse_scr[slab, rows(start), :] = jnp.where(first_head, lse[:span], lse[span:])

        n_grp = len(ATT_GROUPS)
        ct = 256

        def combine(j, carry):
            rws = pl.ds(pl.multiple_of(j * ct, ct), ct)
            for sl in range(2):
                ls = [lse_scr[2 * g + sl, rws, :] for g in range(n_grp)]
                mx = functools.reduce(jnp.maximum, ls)
                es = [jnp.exp(l - mx) for l in ls]
                inv = 1.0 / functools.reduce(lambda a, b: a + b, es)
                for g in range(n_grp):
                    slab = 2 * g + sl
                    o_ref[rws, slab * LANES:(slab + 1) * LANES] = (o_scr[slab, rws, :] * (es[g] * inv)).astype(BF16)
            return carry

        lax.fori_loop(0, t_total // ct, combine, 0)


def _att_branch(x, g, w_in, layer, w_off, gq, gk, e4, maskbias, *, rt=512):
    bsz, t, d = x.shape
    width = gq.shape[1]
    n_slab = width // LANES
    assert w_off % MXU_DIM == 0 and (3 * width) % MXU_DIM == 0
    n_w = 3 * width // MXU_DIM
    return pl.pallas_call(
        functools.partial(_att_kernel, rt=rt, n_w=n_w),
        grid=(bsz, t // rt),
        in_specs=[pl.BlockSpec((None, rt, d), lambda b, i: (b, i, 0)), _const_spec((1, d))]
                 + [_col_block_spec(d, MXU_DIM, w_off // MXU_DIM + j, layer, True) for j in range(n_w)]
                 + [_const_spec((1, width)), _const_spec((1, width)),
                    _const_spec(e4.shape), _const_spec(maskbias.shape, True)],
        out_specs=pl.BlockSpec((None, t, width), lambda b, i: (b, 0, 0)),
        out_shape=jax.ShapeDtypeStruct((bsz, t, width), BF16),
        scratch_shapes=[pltpu.VMEM((3 * n_slab, t, LANES), F32), pltpu.VMEM((n_slab, t, LANES), F32),
                        pltpu.VMEM((n_slab, t, LANES), F32)],
        compiler_params=_params(("arbitrary", "arbitrary")),
        name="att_branch",
    )(x, g, *([w_in] * n_w), gq, gk, e4, maskbias)


def _att_maskbias(n_heads):
    slopes = jnp.exp2(-8.0 * jnp.arange(1, n_heads + 1, dtype=F32) / n_heads)
    qi = jnp.arange(ATT_SPAN)[:, None]
    kj = jnp.arange(2 * ATT_SPAN)[None, :]
    delta = qi + ATT_SPAN - kj
    valid = (delta >= 0) & (delta <= ATT_SPAN)
    dil = jnp.repeat(jnp.array([d for _, d in ATT_GROUPS], F32), ATT_HEADS_PER_GROUP)
    bias = -(slopes * dil)[:, None, None] * delta.astype(F32)[None]
    return jnp.where(valid[None], bias, NEG_INF).reshape(n_heads // 2, 2 * ATT_SPAN, 2 * ATT_SPAN)


def _head_mean_matrix(n_lanes, head_dim):
    lane_head = jnp.arange(n_lanes) // head_dim
    return ((lane_head[:, None] == lane_head[None, :]).astype(F32) / head_dim).astype(BF16)


def _cumsum_rows(x):
    n = x.shape[0]
    rows = lax.broadcasted_iota(jnp.int32, x.shape, 0)
    s = 1
    while s < n:
        x = x + jnp.where(rows >= s, pltpu.roll(x, s, 0), 0.0)
        s *= 2
    return x


def _dn_kernel(*refs, tiles_per_seq):
    tail_ref, s_scr, *stage_scratch = refs[8:]
    step = pl.program_id(0)

    @pl.when(step == 0)
    def _():
        for ref in stage_scratch:
            ref[...] = jnp.zeros_like(ref)

    @pl.when(step % tiles_per_seq == 0)
    def _():
        tail_ref[...] = jnp.zeros_like(tail_ref)

    @pl.when((step % tiles_per_seq == 1) | (step == 0))
    def _():
        s_scr[...] = jnp.zeros_like(s_scr)

    for parity in range(2):
        pl.when(step % 2 == parity)(functools.partial(_dn_body, *refs, rd=1 - parity, wr=parity))


def _dn_body(x_ref, g_ref, w_ref, cw_ref, aexp_ref, dtb_ref, on_ref, o_ref, tail_ref, s_scr,
             kbq_s, k_s, rhs_s, kd_s, qeg_s, gam_s, gamt_s, z_s, *, rd, wr):
    tt, width = o_ref.shape
    n_heads = width // DN_HEAD_DIM
    c = DN_CHUNK

    u = _rms(x_ref[...], g_ref[...]).astype(BF16)
    parts = {}

    def project(*idx):
        for j in idx:
            parts[j] = _dot(u, w_ref[:, j * MXU_DIM:(j + 1) * MXU_DIM])

    assert 2 * c == LANES and n_heads % 4 == 0
    n_ch = tt // c
    n_pair = n_heads // 2
    row2 = lax.broadcasted_iota(jnp.int32, (c, LANES), 0)
    lane2 = lax.broadcasted_iota(jnp.int32, (c, LANES), 1)
    col2 = lane2 & (c - 1)
    incl = col2 <= row2
    strict = col2 < row2
    lo2 = lane2 < c
    lane2x = lax.broadcasted_iota(jnp.int32, (LANES, LANES), 1)
    lo2x = lane2x < c
    row4 = lax.broadcasted_iota(jnp.int32, (c, 2 * LANES), 0)
    lane4 = lax.broadcasted_iota(jnp.int32, (c, 2 * LANES), 1)
    eye4 = ((lane4 & (c - 1)) == row4).astype(F32)
    blk4 = lane4 // c
    zero_k = jnp.zeros((c, DN_HEAD_DIM), BF16)

    def block_diag(m):
        return jnp.concatenate([jnp.where(blk4 == i, m, 0.0) for i in range(4)], axis=0).astype(BF16)

    def l2n(m):
        return m * lax.rsqrt(jnp.sum(m * m, axis=-1, keepdims=True) + EPS)

    probs = [(ch, h) for ch in range(n_ch) for h in range(n_heads)]
    pairs = [(ch, p) for ch in range(n_ch) for p in range(n_pair)]
    quads = [(ch, qd) for ch in range(n_ch) for qd in range(n_pair // 2)]
    pix = {ph: i for i, ph in enumerate(probs)}

    n_prob = len(probs)
    staged = {name: [None] * n_prob for name in ("kbq", "k", "rhs", "kd", "qeg")}
    common = {}
    ba_piece = 4 * n_pair

    def prep_common():
        ba = parts[ba_piece][:, :LANES]
        common["beta"] = jax.nn.sigmoid(ba)
        glog_all = -aexp_ref[...] * _softplus(ba + dtb_ref[...])
        common["gam"] = [_cumsum_rows(glog_all[ch * c:(ch + 1) * c]) for ch in range(n_ch)]
        common["gam_t"] = [jnp.concatenate([g_, g_], axis=0).T for g_ in common["gam"]]

    def prep_pair(p):
        def conv_silu(section):
            blk = parts[section * n_pair + p]
            cs = slice(section * width + p * MXU_DIM, section * width + (p + 1) * MXU_DIM)
            y = _causal_conv(blk, tail_ref[:, cs], cw_ref[:, cs])
            tail_ref[:, cs] = blk[tt - SUBLANES:tt]
            return y * jax.nn.sigmoid(y)

        qs, ks, vs = conv_silu(0), conv_silu(1), conv_silu(2)
        for ch in range(n_ch):
            rs = slice(ch * c, (ch + 1) * c)
            for sub in range(2):
                h = 2 * p + sub
                ls = slice(sub * DN_HEAD_DIM, (sub + 1) * DN_HEAD_DIM)
                q_ = l2n(qs[rs, ls]) * (DN_HEAD_DIM ** -0.5)
                k_ = l2n(ks[rs, ls])
                beta = common["beta"][rs, h:h + 1]
                gc = common["gam"][ch][:, n_heads + h:n_heads + h + 1]
                eg_ = jnp.exp(gc)
                kb_ = k_ * beta
                i = pix[(ch, h)]
                staged["rhs"][i] = jnp.concatenate([vs[rs, ls] * beta, kb_ * eg_], axis=1).astype(BF16)
                staged["kbq"][i] = jnp.concatenate([kb_, q_], axis=0).astype(BF16)
                staged["kd"][i] = k_ * jnp.exp(gc[c - 1:c, :] - gc)
                staged["qeg"][i] = (q_ * eg_).astype(BF16)
                staged["k"][i] = k_.astype(BF16)

    fill = iter([
        lambda: project(ba_piece, 0, n_pair, 2 * n_pair),
        lambda: (prep_common(), project(1, n_pair + 1, 2 * n_pair + 1)),
        lambda: prep_pair(0),
        lambda: project(2, n_pair + 2, 2 * n_pair + 2),
        lambda: prep_pair(1),
        lambda: project(3, n_pair + 3, 2 * n_pair + 3),
        lambda: prep_pair(2),
        lambda: project(3 * n_pair, 3 * n_pair + 1),
        lambda: prep_pair(3),
        lambda: project(3 * n_pair + 2),
        lambda: project(3 * n_pair + 3),
    ])
    assert n_pair == 4

    def interleave():
        action = next(fill, None)
        if action is not None:
            action()

    kbq = [kbq_s[rd, i] for i in range(n_prob)]
    k = [k_s[rd, i] for i in range(n_prob)]
    rhs = [rhs_s[rd, i] for i in range(n_prob)]
    kd = [kd_s[rd, i] for i in range(n_prob)]
    qeg = [qeg_s[rd, i] for i in range(n_prob)]
    gam_all = [gam_s[rd, ch] for ch in range(n_ch)]
    gam_t = [gamt_s[rd, ch] for ch in range(n_ch)]
    z = z_s[rd]
    gcol = [gam_all[ch][:, n_heads + h:n_heads + h + 1] for ch, h in probs]
    glast = [gc[c - 1:c, :] for gc in gcol]

    def both(lst, ch, p):
        return lst[pix[(ch, 2 * p)]], lst[pix[(ch, 2 * p + 1)]]

    interleave()
    kq = []
    for ch, p in pairs:
        (x0, x1), (k0, k1) = both(kbq, ch, p), both(k, ch, p)
        kq.append(_dot_nt(x0, jnp.concatenate([k0.astype(BF16), zero_k], axis=0))
                  + _dot_nt(x1, jnp.concatenate([zero_k, k1.astype(BF16)], axis=0)))
    interleave()
    decay = []
    for ch, p in pairs:
        g0, g1 = both(gcol, ch, p)
        gr = jnp.where(lo2[0:1], gam_t[ch][n_heads + 2 * p:n_heads + 2 * p + 1, :],
                       gam_t[ch][n_heads + 2 * p + 1:n_heads + 2 * p + 2, :])
        diff = jnp.where(lo2, g0, g1) - gr
        decay.append(jnp.where(incl, jnp.exp(jnp.where(incl, diff, 0.0)), 0.0))
    a2 = [jnp.where(strict, kq_[:c] * d_, 0.0) for kq_, d_ in zip(kq, decay)]
    qk2 = [jnp.where(incl, kq_[c:] * d_, 0.0) for kq_, d_ in zip(kq, decay)]

    a4 = [jnp.concatenate([a2[ch * n_pair + 2 * qd], a2[ch * n_pair + 2 * qd + 1]], axis=1) for ch, qd in quads]
    tinv = [eye4 - a_ for a_ in a4]
    pw = [_dot(a_.astype(BF16), block_diag(a_)) for a_ in a4]
    interleave()
    for _ in range(int(math.log2(c)) - 2):
        r = [_dot(jnp.concatenate([t_, p_], axis=0).astype(BF16), block_diag(p_)) for t_, p_ in zip(tinv, pw)]
        tinv = [t_ + r_[:c] for t_, r_ in zip(tinv, r)]
        pw = [r_[c:] for r_ in r]
        interleave()
    tinv = [t_ + _dot(t_.astype(BF16), block_diag(p_)) for t_, p_ in zip(tinv, pw)]
    interleave()

    def split_rows(m):
        return jnp.concatenate([jnp.where(lo2, m, 0.0), jnp.where(lo2, 0.0, m)], axis=0).astype(BF16)

    sol, lhs_o = [], []
    for i, (ch, p) in enumerate(pairs):
        t2 = tinv[i // 2][:, (i % 2) * LANES:(i % 2 + 1) * LANES]
        r0, r1 = both(rhs, ch, p)
        sol.append(_dot(split_rows(t2), jnp.concatenate([r0, r1], axis=0)))
        kd0, kd1 = both(kd, ch, p)
        kdt = jnp.concatenate([kd0, kd1], axis=0).T
        lhs_o.append(jnp.concatenate([split_rows(qk2[i]), jnp.where(lo2x, kdt, 0.0).astype(BF16),
                                      jnp.where(lo2x, 0.0, kdt).astype(BF16)], axis=0))
    wq = []
    for ch, h in probs:
        s_ = sol[ch * n_pair + h // 2][(h % 2) * c:(h % 2 + 1) * c]
        i = pix[(ch, h)]
        wq.append(jnp.concatenate([s_[:, DN_HEAD_DIM:].astype(BF16), qeg[i]], axis=0))
    interleave()

    for ch in range(n_ch):
        rs = slice(ch * c, (ch + 1) * c)
        state = [s_scr[h] for h in range(n_heads)]
        ws = [_dot(wq[pix[(ch, h)]], state[h].astype(BF16)) for h in range(n_heads)]
        interleave()
        un = [sol[ch * n_pair + h // 2][(h % 2) * c:(h % 2 + 1) * c, :DN_HEAD_DIM] - ws[h][:c] for h in range(n_heads)]
        ou = [_dot(lhs_o[ch * n_pair + p], jnp.concatenate([un[2 * p], un[2 * p + 1]], axis=0).astype(BF16))
              for p in range(n_pair)]
        interleave()
        for h in range(n_heads):
            hs = slice(h * DN_HEAD_DIM, (h + 1) * DN_HEAD_DIM)
            o2 = ou[h // 2]
            sub = h % 2
            s_scr[h] = (state[h] * jnp.exp(glast[pix[(ch, h)]])
                        + o2[2 * c + sub * DN_HEAD_DIM:2 * c + (sub + 1) * DN_HEAD_DIM])
            on = _rms(ws[h][c:] + o2[sub * c:(sub + 1) * c], on_ref[:, hs])
            zz = z[rs, hs]
            o_ref[rs, hs] = (on * (zz * jax.nn.sigmoid(zz))).astype(BF16)

    for action in fill:
        action()
    for ref, vals in ((kbq_s, staged["kbq"]), (k_s, staged["k"]), (rhs_s, staged["rhs"]), (kd_s, staged["kd"]),
                      (qeg_s, staged["qeg"]), (gam_s, common["gam"]), (gamt_s, common["gam_t"])):
        for i, v in enumerate(vals):
            ref[wr, i] = v
    z_s[wr] = jnp.concatenate([parts[3 * n_pair + j] for j in range(n_pair)], axis=1)


def _dn_branch(x, g, w_in, layer, w_block, cw, aexp, dtb, onorm, *, tt=256):
    bsz, t, d = x.shape
    width = onorm.shape[1]
    n_heads = width // DN_HEAD_DIM
    w_cols = 4 * width + 2 * LANES
    tiles_per_seq = t // tt
    n_tiles = bsz * tiles_per_seq
    n_prob = (tt // DN_CHUNK) * n_heads
    c = DN_CHUNK

    def x_map(s):
        j = jnp.minimum(s, n_tiles - 1)
        return j // tiles_per_seq, j % tiles_per_seq, 0

    def o_map(s):
        j = jnp.maximum(s - 1, 0)
        return j // tiles_per_seq, j % tiles_per_seq, 0

    assert tiles_per_seq > 1
    return pl.pallas_call(
        functools.partial(_dn_kernel, tiles_per_seq=tiles_per_seq),
        grid=(n_tiles + 1,),
        in_specs=[pl.BlockSpec((None, tt, d), x_map), _const_spec((1, d)),
                  _col_block_spec(d, w_cols, w_block, layer), _const_spec(cw.shape),
                  _const_spec((1, LANES)), _const_spec((1, LANES)), _const_spec((1, width))],
        out_specs=pl.BlockSpec((None, tt, width), o_map),
        out_shape=jax.ShapeDtypeStruct((bsz, t, width), BF16),
        scratch_shapes=[pltpu.VMEM((SUBLANES, 3 * width), F32),
                        pltpu.VMEM((n_heads, DN_HEAD_DIM, DN_HEAD_DIM), F32),
                        pltpu.VMEM((2, n_prob, 2 * c, DN_HEAD_DIM), BF16),
                        pltpu.VMEM((2, n_prob, c, DN_HEAD_DIM), BF16),
                        pltpu.VMEM((2, n_prob, c, 2 * DN_HEAD_DIM), BF16),
                        pltpu.VMEM((2, n_prob, c, DN_HEAD_DIM), F32),
                        pltpu.VMEM((2, n_prob, c, DN_HEAD_DIM), BF16),
                        pltpu.VMEM((2, tt // c, c, LANES), F32),
                        pltpu.VMEM((2, tt // c, LANES, LANES), F32),
                        pltpu.VMEM((2, tt, width), F32)],
        compiler_params=_params(("arbitrary",)),
        name="dn_branch",
    )(x, g, w_in, cw, aexp, dtb, onorm)


def _merge_kernel(x_ref, g_ref, ya_ref, yb_ref, yc_ref, wm_ref, wa_ref, wb_ref, wc_ref, wo_ref, o_ref):
    x = x_ref[...]
    d = x.shape[1]
    u = _rms(x, g_ref[...]).astype(BF16)
    y = jnp.zeros_like(x)
    for j, (yr, wr) in enumerate(((ya_ref, wa_ref), (yb_ref, wb_ref), (yc_ref, wc_ref))):
        gate = jax.nn.sigmoid(_dot(u, wm_ref[:, j * d:(j + 1) * d]))
        y = y + gate * _dot(yr[...], wr[...])
    o_ref[...] = x + _dot(y.astype(BF16), wo_ref[...])


def _merge(x2, g, ya, yb, yc, wm, wa, wb, wc, wo, *, tm=1024):
    n, d = x2.shape
    row = lambda i: (i, 0)
    return pl.pallas_call(
        _merge_kernel,
        grid=(n // tm,),
        in_specs=[pl.BlockSpec((tm, d), row), _const_spec((1, d)),
                  pl.BlockSpec((tm, ya.shape[1]), row), pl.BlockSpec((tm, yb.shape[1]), row),
                  pl.BlockSpec((tm, yc.shape[1]), row),
                  _const_spec(wm.shape, True), _const_spec(wa.shape, True), _const_spec(wb.shape, True),
                  _const_spec(wc.shape, True), _const_spec(wo.shape, True)],
        out_specs=pl.BlockSpec((tm, d), row),
        out_shape=jax.ShapeDtypeStruct((n, d), F32),
        compiler_params=_params(("arbitrary",)),
        name="merge",
    )(x2, g, ya, yb, yc, wm, wa, wb, wc, wo)


def _block_diag_tiles(w):
    nb, k, _ = w.shape
    per = MXU_DIM // k
    w4 = w.reshape(nb // per, per, k, k)
    eye = jnp.eye(per, dtype=w.dtype)
    return jnp.einsum('tpij,pq->tpiqj', w4, eye).reshape(nb // per, MXU_DIM, MXU_DIM)


def _row(v):
    return v.reshape(1, -1).astype(F32)


def _mixer(x, layer, mix_norm, w_in, rg_conv_w, rg_conv_b, rg_w_r, rg_b_r, rg_w_i, rg_b_i, rg_lambda,
           att_q_norm, att_k_norm, dn_conv_w, dn_a_log, dn_dt_bias, dn_out_norm, w_branch, w_out):
    bsz, t, d = x.shape
    rg_w = rg_conv_w.shape[1]
    att_w = att_q_norm.size
    dn_w = dn_out_norm.size
    dn_h = dn_out_norm.shape[0]
    att_off = 2 * rg_w
    dn_off = att_off + 3 * att_w
    merge_off = dn_off + 4 * dn_w + 2 * dn_h
    dn_cols = 4 * dn_w + 2 * LANES
    assert dn_off % dn_cols == 0 and dn_off + dn_cols <= w_in.shape[2] and 2 * dn_h <= LANES
    w_merge = w_in[layer, :, merge_off:merge_off + 3 * d]
    g = _row(mix_norm)

    lamc = _row(-RG_C * jax.nn.softplus(-rg_lambda.astype(F32)))
    ya = _rg_branch(x, g, w_in, layer, rg_conv_w, _row(rg_conv_b),
                    _block_diag_tiles(rg_w_r).astype(BF16), _row(rg_b_r),
                    _block_diag_tiles(rg_w_i).astype(BF16), _row(rg_b_i), lamc)

    n_att_heads = att_q_norm.shape[0]
    yb = _att_branch(x, g, w_in, layer, att_off, _row(att_q_norm), _row(att_k_norm),
                     _head_mean_matrix(2 * LANES, ATT_HEAD_DIM), _att_maskbias(n_att_heads))

    pad = jnp.zeros((LANES - 2 * dn_h,), F32)
    aexp = _row(jnp.concatenate([jnp.zeros((dn_h,), F32), jnp.exp(dn_a_log.astype(F32)), pad]))
    dtb = _row(jnp.concatenate([jnp.zeros((dn_h,), F32), dn_dt_bias.astype(F32), pad]))
    yc = _dn_branch(x, g, w_in, layer, dn_off // dn_cols, dn_conv_w, aexp, dtb, _row(dn_out_norm))

    wa = w_branch[:rg_w]
    wb = w_branch[rg_w:rg_w + att_w]
    wc = w_branch[rg_w + att_w:]
    n = bsz * t
    out = _merge(x.reshape(n, d), g, ya.reshape(n, rg_w), yb.reshape(n, att_w), yc.reshape(n, dn_w),
                 w_merge, wa, wb, wc, w_out)
    return out.reshape(bsz, t, d)


def kernel(x, ffn1_norm, ffn1_w_gate, ffn1_w_up, ffn1_w_down, mix_norm, w_in, rg_conv_w, rg_conv_b, rg_w_r, rg_b_r, rg_w_i, rg_b_i, rg_lambda, att_q_norm, att_k_norm, dn_conv_w, dn_a_log, dn_dt_bias, dn_out_norm, w_branch, w_out, ffn2_norm, ffn2_w_gate, ffn2_w_up, ffn2_w_down):
    bsz, t, d = x.shape
    n = bsz * t
    n_layers = ffn1_norm.shape[0]
    ffn1_w = [w[0].astype(BF16) for w in (ffn1_w_gate, ffn1_w_up, ffn1_w_down)]
    w_in_all = w_in.astype(BF16)
    for l in range(n_layers):
        casts = [(w, l) for w in (w_branch, w_out, ffn2_w_gate, ffn2_w_up, ffn2_w_down)]
        x, (w_branch_b, w_out_b, *ffn2_w) = _ffn(x.reshape(n, d), _row(ffn1_norm[l]), *ffn1_w, casts)
        x = _mixer(x.reshape(bsz, t, d), l, mix_norm[l], w_in_all, rg_conv_w[l], rg_conv_b[l], rg_w_r[l], rg_b_r[l],
                   rg_w_i[l], rg_b_i[l], rg_lambda[l], att_q_norm[l], att_k_norm[l], dn_conv_w[l], dn_a_log[l],
                   dn_dt_bias[l], dn_out_norm[l], w_branch_b, w_out_b)
        casts = [(w, l + 1) for w in (ffn1_w_gate, ffn1_w_up, ffn1_w_down)] if l + 1 < n_layers else []
        x, ffn1_w = _ffn(x.reshape(n, d), _row(ffn2_norm[l]), *ffn2_w, casts)
    return x.reshape(bsz, t, d)
```

```python
import functools
import math

import jax
import jax.numpy as jnp
from jax import lax
from jax.experimental import pallas as pl
from jax.experimental.pallas import tpu as pltpu

F32 = jnp.float32
BF16 = jnp.bfloat16

EPS = 1e-6
NEG_INF = -1e30
RG_C = 8.0
RG_BLOCK_DIM = 64
CONV_WIDTH = 4
ATT_GROUPS = ((128, 1), (512, 4), (2048, 16))
ATT_HEADS_PER_GROUP = 4
ATT_HEAD_DIM = 64
ATT_SPAN = 128
DN_HEAD_DIM = 128
DN_CHUNK = 64

LANES = 128
SUBLANES = 8
MXU_DIM = 256
VMEM_LIMIT_BYTES = 60 * 1024 * 1024


def _rms(x, g):
    return x * lax.rsqrt(jnp.mean(x * x, axis=-1, keepdims=True) + EPS) * g


def _dot(a, b):
    return jnp.dot(a, b, preferred_element_type=F32)


def _dot_nt(a, b):
    return lax.dot_general(a, b, (((1,), (1,)), ((), ())), preferred_element_type=F32)


def _dot_tn(a, b):
    return lax.dot_general(a, b, (((0,), (0,)), ((), ())), preferred_element_type=F32)


def _softplus(z):
    return jnp.maximum(z, 0.0) + jnp.log(1.0 + jnp.exp(-jnp.abs(z)))


def _const_spec(shape, single_buffer=False):
    nd = len(shape)
    mode = pl.Buffered(1) if single_buffer else None
    return pl.BlockSpec(shape, lambda *_: (0,) * nd, pipeline_mode=mode)


def _col_block_spec(rows, cols, col_block, layer, single_buffer=False):
    mode = pl.Buffered(1) if single_buffer else None
    return pl.BlockSpec((None, rows, cols), lambda *_: (layer, 0, col_block), pipeline_mode=mode)


def _params(semantics):
    return pltpu.CompilerParams(dimension_semantics=semantics, vmem_limit_bytes=VMEM_LIMIT_BYTES)


BF16_SUBLANES = 16


def _cast_blocking(rows, n_steps):
    rpb = BF16_SUBLANES
    while rows % rpb or rows // rpb > n_steps:
        rpb += BF16_SUBLANES
        assert rpb <= rows
    return rpb, rows // rpb


def _cast_specs(casts, n_steps):
    cast_in, cast_out, cast_shapes = [], [], []
    for stack, layer in casts:
        _, rows, cols = stack.shape
        rpb, nb = _cast_blocking(rows, n_steps)
        cast_in.append(pl.BlockSpec((None, rpb, cols), lambda i, layer=layer, nb=nb: (layer, jnp.minimum(i, nb - 1), 0)))
        cast_out.append(pl.BlockSpec((rpb, cols), lambda i, nb=nb: (jnp.minimum(i, nb - 1), 0)))
        cast_shapes.append(jax.ShapeDtypeStruct((rows, cols), BF16))
    return cast_in, cast_out, cast_shapes


def _ffn_kernel(x_ref, g_ref, wg_ref, wu_ref, wd_ref, *rest, f_chunk):
    n_cast = (len(rest) - 1) // 2
    o_ref = rest[n_cast]
    for src, dst in zip(rest[:n_cast], rest[n_cast + 1:]):
        dst[...] = src[...].astype(BF16)
    x = x_ref[...]
    u = _rms(x, g_ref[...]).astype(BF16)
    acc = jnp.zeros_like(x)
    for c in range(wg_ref.shape[1] // f_chunk):
        sl = slice(c * f_chunk, (c + 1) * f_chunk)
        hg = _dot(u, wg_ref[:, sl])
        hu = _dot(u, wu_ref[:, sl])
        h = (hg * jax.nn.sigmoid(hg) * hu).astype(BF16)
        acc = acc + _dot(h, wd_ref[sl, :])
    o_ref[...] = x + 0.5 * acc


def _ffn(x2, g, wg, wu, wd, casts=(), *, tm=1024, f_chunk=256):
    n, d = x2.shape
    f = wg.shape[1]
    n_steps = n // tm
    cast_in, cast_out, cast_shapes = _cast_specs(casts, n_steps)
    outs = pl.pallas_call(
        functools.partial(_ffn_kernel, f_chunk=f_chunk),
        grid=(n_steps,),
        in_specs=[pl.BlockSpec((tm, d), lambda i: (i, 0)), _const_spec((1, d)),
                  _const_spec((d, f), True), _const_spec((d, f), True), _const_spec((f, d), True)] + cast_in,
        out_specs=[pl.BlockSpec((tm, d), lambda i: (i, 0))] + cast_out,
        out_shape=[jax.ShapeDtypeStruct((n, d), F32)] + cast_shapes,
        compiler_params=_params(("arbitrary",)),
        name="ffn",
    )(x2, g, wg, wu, wd, *[stack for stack, _ in casts])
    return outs[0], list(outs[1:])


def _shift_rows(cur, prev8, s):
    rows = lax.broadcasted_iota(jnp.int32, cur.shape, 0)
    return jnp.where(rows < s, pltpu.roll(prev8, s, 0), pltpu.roll(cur, s, 0))


def _causal_conv(p, tail8, w):
    kw = w.shape[0]
    acc = p * w[kw - 1:kw, :]
    top = p[0:SUBLANES] * w[kw - 1:kw, :]
    for s in range(1, kw):
        wk = w[kw - 1 - s:kw - s, :]
        acc = acc + pltpu.roll(p, s, 0) * wk
        top = top + _shift_rows(p[0:SUBLANES], tail8, s) * wk
    return jnp.concatenate([top, acc[SUBLANES:]], axis=0)


def _segment_perm(tt):
    r = jnp.arange(tt)
    src = (r % SUBLANES) * (tt // SUBLANES) + r // SUBLANES
    return (src[:, None] == jnp.arange(tt)[None, :]).astype(BF16)


def _rg_kernel(x_ref, g_ref, perm_ref, permt_ref, wx_ref, wgate_ref, cw_ref, cb_ref, wr_ref, br_ref, wi_ref,
               bi_ref, lamc_ref, o_ref, tail_ref, carry_ref, a_scr, h_scr):
    rows, width = o_ref.shape
    tt = perm_ref.shape[0]
    n_sub = rows // tt
    n_grp = tt // SUBLANES
    kw = cw_ref.shape[0]

    @pl.when(pl.program_id(1) == 0)
    def _():
        tail_ref[...] = jnp.zeros_like(tail_ref)
        carry_ref[...] = jnp.zeros_like(carry_ref)

    u = _rms(x_ref[...], g_ref[...]).astype(BF16)
    nc = width // MXU_DIM
    rows8 = lax.broadcasted_iota(jnp.int32, (SUBLANES, MXU_DIM), 0)
    up = {}
    tails = {c: tail_ref[:, c * MXU_DIM:(c + 1) * MXU_DIM] for c in range(nc)}
    carries = {c: carry_ref[:, c * MXU_DIM:(c + 1) * MXU_DIM] for c in range(nc)}

    def project(t, c):
        if t not in up:
            up[t] = _dot(perm_ref[...], u[t * tt:(t + 1) * tt]).astype(BF16)
        cs = slice(c * MXU_DIM, (c + 1) * MXU_DIM)
        return _dot(up[t], wx_ref[:, cs]), _dot(up[t], wgate_ref[:, cs])

    def conv(c, px):
        cs = slice(c * MXU_DIM, (c + 1) * MXU_DIM)
        tail = tails[c]
        head = [_shift_rows(px[(n_grp - k) * SUBLANES:(n_grp - k + 1) * SUBLANES],
                            tail[(kw - 1 - k) * SUBLANES:(kw - k) * SUBLANES], 1) for k in range(kw - 1, 0, -1)]
        ext = jnp.concatenate(head + [px], axis=0)
        tails[c] = px[tt - (kw - 1) * SUBLANES:tt]
        xa = cb_ref[:, cs] + ext[0:tt] * cw_ref[0:1, cs]
        for k in range(1, kw):
            xa = xa + ext[k * SUBLANES:k * SUBLANES + tt] * cw_ref[k:k + 1, cs]
        return xa

    def gates(c, xa):
        xab = xa.astype(BF16)
        return _dot(xab, wr_ref[c]), _dot(xab, wi_ref[c])

    def scan(t, c, xa, r_lin, i_lin, pg):
        cs = slice(c * MXU_DIM, (c + 1) * MXU_DIM)
        rt = slice(t * tt, (t + 1) * tt)
        a = jnp.exp(lamc_ref[:, cs] * jax.nn.sigmoid(r_lin + br_ref[:, cs]))
        a_scr[rt, cs] = a
        h_scr[rt, cs] = jnp.sqrt(1.0 - a * a) * (jax.nn.sigmoid(i_lin + bi_ref[:, cs]) * xa)
        hl = jnp.zeros((SUBLANES, MXU_DIM), F32)
        ap = jnp.ones((SUBLANES, MXU_DIM), F32)
        for j in range(n_grp):
            rs = slice(t * tt + j * SUBLANES, t * tt + (j + 1) * SUBLANES)
            aj = a_scr[rs, cs]
            hl = aj * hl + h_scr[rs, cs]
            ap = aj * ap
            h_scr[rs, cs] = hl
            a_scr[rs, cs] = ap
        for s in (1, 2, 4):
            m = rows8 >= s
            hl = jnp.where(m, ap * pltpu.roll(hl, s, 0) + hl, hl)
            ap = jnp.where(m, ap * pltpu.roll(ap, s, 0), ap)
        cin = carries[c]
        seg_end = hl + ap * cin
        carries[c] = jnp.broadcast_to(seg_end[SUBLANES - 1:SUBLANES, :], seg_end.shape)
        seg_in = jnp.where(rows8 < 1, cin, pltpu.roll(seg_end, 1, 0))
        h = (h_scr[rt, cs].reshape(n_grp, SUBLANES, MXU_DIM)
             + a_scr[rt, cs].reshape(n_grp, SUBLANES, MXU_DIM) * seg_in[None])
        return (h.reshape(tt, MXU_DIM) * jax.nn.gelu(pg)).astype(BF16)

    def emit(t, c, y):
        o_ref[t * tt:(t + 1) * tt, c * MXU_DIM:(c + 1) * MXU_DIM] = _dot(permt_ref[...], y).astype(BF16)

    items = [(t, c) for t in range(n_sub) for c in range(nc)]
    proj, xas, gts, ys = {}, {}, {}, {}
    for step in range(len(items) + 3):
        if step < len(items):
            proj[step] = project(*items[step])
        k = step - 1
        if 0 <= k < len(items):
            xas[k] = conv(items[k][1], proj[k][0])
            gts[k] = gates(items[k][1], xas[k])
        k = step - 2
        if 0 <= k < len(items):
            ys[k] = scan(*items[k], xas[k], gts[k][0], gts[k][1], proj[k][1])
        k = step - 3
        if 0 <= k < len(items):
            emit(*items[k], ys[k])
    for c in range(nc):
        tail_ref[:, c * MXU_DIM:(c + 1) * MXU_DIM] = tails[c]
        carry_ref[:, c * MXU_DIM:(c + 1) * MXU_DIM] = carries[c]


def _rg_branch(x, g, w_in, layer, cw, cb, wr_t, br, wi_t, bi, lamc, *, tt=256, n_sub=4):
    bsz, t, d = x.shape
    width = cw.shape[1]
    perm = _segment_perm(tt)
    rows = tt * n_sub
    return pl.pallas_call(
        _rg_kernel,
        grid=(bsz, t // rows),
        in_specs=[pl.BlockSpec((None, rows, d), lambda b, i: (b, i, 0)), _const_spec((1, d)),
                  _const_spec((tt, tt)), _const_spec((tt, tt)),
                  _col_block_spec(d, width, 0, layer), _col_block_spec(d, width, 1, layer),
                  _const_spec(cw.shape), _const_spec((1, width)),
                  _const_spec(wr_t.shape), _const_spec((1, width)),
                  _const_spec(wi_t.shape), _const_spec((1, width)), _const_spec((1, width))],
        out_specs=pl.BlockSpec((None, rows, width), lambda b, i: (b, i, 0)),
        out_shape=jax.ShapeDtypeStruct((bsz, t, width), BF16),
        scratch_shapes=[pltpu.VMEM(((cw.shape[0] - 1) * SUBLANES, width), F32), pltpu.VMEM((SUBLANES, width), F32),
                        pltpu.VMEM((rows, width), F32), pltpu.VMEM((rows, width), F32)],
        compiler_params=_params(("arbitrary", "arbitrary")),
        name="rg_branch",
    )(x, g, perm, perm.T, w_in, w_in, cw, cb, wr_t, br, wi_t, bi, lamc)


ATT_BLOCK_BATCH = 4


def _att_kernel(x_ref, g_ref, *rest, rt, n_w):
    w_refs = rest[:n_w]
    gq_ref, gk_ref, e4_ref, mb_ref, o_ref, qkv_scr, o_scr, lse_scr = rest[n_w:]
    t_total = o_ref.shape[0]
    n_slab = o_scr.shape[0]
    i = pl.program_id(1)
    row0 = pl.multiple_of(i * rt, rt)
    span = ATT_SPAN

    u = _rms(x_ref[...], g_ref[...]).astype(BF16)
    qkv = jnp.concatenate([_dot(u, w_ref[...]) for w_ref in w_refs], axis=1)
    width = n_slab * LANES
    e4 = e4_ref[...]
    for s in range(n_slab):
        cs = slice(s * LANES, (s + 1) * LANES)
        qk = jnp.concatenate([qkv[:, cs], qkv[:, width + s * LANES: width + (s + 1) * LANES]], axis=1)
        gain = jnp.concatenate([gq_ref[:, cs] * (ATT_HEAD_DIM ** -0.5), gk_ref[:, cs]], axis=1)
        ms = _dot((qk * qk).astype(BF16), e4)
        qk = qk * lax.rsqrt(ms + EPS) * gain
        qkv_scr[s, pl.ds(row0, rt), :] = qk[:, :LANES]
        qkv_scr[n_slab + s, pl.ds(row0, rt), :] = qk[:, LANES:]
        qkv_scr[2 * n_slab + s, pl.ds(row0, rt), :] = qkv[:, 2 * width + s * LANES: 2 * width + (s + 1) * LANES]

    @pl.when(i == pl.num_programs(1) - 1)
    def _():
        lane = lax.broadcasted_iota(jnp.int32, (span, LANES), 1)
        first_head = lane < ATT_HEAD_DIM

        for g, (window, dil) in enumerate(ATT_GROUPS):
            assert window // dil == span
            nblk = t_total // dil // span
            blocks = [(r, n) for n in range(nblk) for r in range(dil)]

            def rows(start, dil=dil):
                return pl.ds(start, span) if dil == 1 else pl.ds(start, span, stride=dil)

            for b0 in range(0, len(blocks), ATT_BLOCK_BATCH):
                probs = [(r + n * span * dil, n > 0, 2 * g + sl)
                         for r, n in blocks[b0:b0 + ATT_BLOCK_BATCH] for sl in range(2)]
                qs, kk, vv = [], [], []
                for start, has_prev, slab in probs:
                    q2 = qkv_scr[slab, rows(start), :]
                    qs.append(jnp.concatenate([jnp.where(first_head, q2, 0.0), jnp.where(first_head, 0.0, q2)],
                                              axis=0).astype(BF16))
                    kc = qkv_scr[n_slab + slab, rows(start), :]
                    vc = qkv_scr[2 * n_slab + slab, rows(start), :]
                    if has_prev:
                        prev = start - span * dil
                        kc = jnp.concatenate([qkv_scr[n_slab + slab, rows(prev), :], kc], axis=0)
                        vc = jnp.concatenate([qkv_scr[2 * n_slab + slab, rows(prev), :], vc], axis=0)
                    kk.append(kc.astype(BF16))
                    vv.append(vc.astype(BF16))
                sc = [_dot_nt(q_, k_) for q_, k_ in zip(qs, kk)]
                sc = [s_ + (mb_ref[slab] if has_prev else mb_ref[slab, :, span:])
                      for s_, (_, has_prev, slab) in zip(sc, probs)]
                mx = [jnp.max(s_, axis=-1, keepdims=True) for s_ in sc]
                pr = [jnp.exp(s_ - m_) for s_, m_ in zip(sc, mx)]
                den = [jnp.sum(p_, axis=-1, keepdims=True) for p_ in pr]
                pv = [_dot(p_.astype(BF16), v_) for p_, v_ in zip(pr, vv)]
                for (start, _, slab), pv_, m_, d_ in zip(probs, pv, mx, den):
                    on = pv_ * (1.0 / d_)
                    lse = jnp.broadcast_to(m_ + jnp.log(d_), on.shape)
                    o_scr[slab, rows(start), :] = jnp.where(first_head, on[:span], on[span:])
                    lse_scr[slab, rows(start), :] = jnp.where(first_head, lse[:span], lse[span:])

        n_grp = len(ATT_GROUPS)
        ct = 256

        def combine(j, carry):
            rws = pl.ds(pl.multiple_of(j * ct, ct), ct)
            for sl in range(2):
                ls = [lse_scr[2 * g + sl, rws, :] for g in range(n_grp)]
                mx = functools.reduce(jnp.maximum, ls)
                es = [jnp.exp(l - mx) for l in ls]
                inv = 1.0 / functools.reduce(lambda a, b: a + b, es)
                for g in range(n_grp):
                    slab = 2 * g + sl
                    o_ref[rws, slab * LANES:(slab + 1) * LANES] = (o_scr[slab, rws, :] * (es[g] * inv)).astype(BF16)
            return carry

        lax.fori_loop(0, t_total // ct, combine, 0)


def _att_branch(x, g, w_in, layer, w_off, gq, gk, e4, maskbias, *, rt=512):
    bsz, t, d = x.shape
    width = gq.shape[1]
    n_slab = width // LANES
    assert w_off % MXU_DIM == 0 and (3 * width) % MXU_DIM == 0
    n_w = 3 * width // MXU_DIM
    return pl.pallas_call(
        functools.partial(_att_kernel, rt=rt, n_w=n_w),
        grid=(bsz, t // rt),
        in_specs=[pl.BlockSpec((None, rt, d), lambda b, i: (b, i, 0)), _const_spec((1, d))]
                 + [_col_block_spec(d, MXU_DIM, w_off // MXU_DIM + j, layer, True) for j in range(n_w)]
                 + [_const_spec((1, width)), _const_spec((1, width)),
                    _const_spec(e4.shape), _const_spec(maskbias.shape, True)],
        out_specs=pl.BlockSpec((None, t, width), lambda b, i: (b, 0, 0)),
        out_shape=jax.ShapeDtypeStruct((bsz, t, width), BF16),
        scratch_shapes=[pltpu.VMEM((3 * n_slab, t, LANES), F32), pltpu.VMEM((n_slab, t, LANES), F32),
                        pltpu.VMEM((n_slab, t, LANES), F32)],
        compiler_params=_params(("arbitrary", "arbitrary")),
        name="att_branch",
    )(x, g, *([w_in] * n_w), gq, gk, e4, maskbias)


def _att_maskbias(n_heads):
    slopes = jnp.exp2(-8.0 * jnp.arange(1, n_heads + 1, dtype=F32) / n_heads)
    qi = jnp.arange(ATT_SPAN)[:, None]
    kj = jnp.arange(2 * ATT_SPAN)[None, :]
    delta = qi + ATT_SPAN - kj
    valid = (delta >= 0) & (delta <= ATT_SPAN)
    dil = jnp.repeat(jnp.array([d for _, d in ATT_GROUPS], F32), ATT_HEADS_PER_GROUP)
    bias = -(slopes * dil)[:, None, None] * delta.astype(F32)[None]
    return jnp.where(valid[None], bias, NEG_INF).reshape(n_heads // 2, 2 * ATT_SPAN, 2 * ATT_SPAN)


def _head_mean_matrix(n_lanes, head_dim):
    lane_head = jnp.arange(n_lanes) // head_dim
    return ((lane_head[:, None] == lane_head[None, :]).astype(F32) / head_dim).astype(BF16)


def _cumsum_rows(x):
    n = x.shape[0]
    rows = lax.broadcasted_iota(jnp.int32, x.shape, 0)
    s = 1
    while s < n:
        x = x + jnp.where(rows >= s, pltpu.roll(x, s, 0), 0.0)
        s *= 2
    return x


def _dn_kernel(*refs, tiles_per_seq):
    tail_ref, s_scr, *stage_scratch = refs[8:]
    step = pl.program_id(0)

    @pl.when(step == 0)
    def _():
        for ref in stage_scratch:
            ref[...] = jnp.zeros_like(ref)

    @pl.when(step % tiles_per_seq == 0)
    def _():
        tail_ref[...] = jnp.zeros_like(tail_ref)

    @pl.when((step % tiles_per_seq == 1) | (step == 0))
    def _():
        s_scr[...] = jnp.zeros_like(s_scr)

    for parity in range(2):
        pl.when(step % 2 == parity)(functools.partial(_dn_body, *refs, rd=1 - parity, wr=parity))


def _dn_body(x_ref, g_ref, w_ref, cw_ref, aexp_ref, dtb_ref, on_ref, o_ref, tail_ref, s_scr,
             kbq_s, k_s, rhs_s, kd_s, qeg_s, gam_s, gamt_s, z_s, *, rd, wr):
    tt, width = o_ref.shape
    n_heads = width // DN_HEAD_DIM
    c = DN_CHUNK

    u = _rms(x_ref[...], g_ref[...]).astype(BF16)
    parts = {}

    def project(*idx):
        for j in idx:
            parts[j] = _dot(u, w_ref[:, j * MXU_DIM:(j + 1) * MXU_DIM])

    assert 2 * c == LANES and n_heads % 4 == 0
    n_ch = tt // c
    n_pair = n_heads // 2
    row2 = lax.broadcasted_iota(jnp.int32, (c, LANES), 0)
    lane2 = lax.broadcasted_iota(jnp.int32, (c, LANES), 1)
    col2 = lane2 & (c - 1)
    incl = col2 <= row2
    strict = col2 < row2
    lo2 = lane2 < c
    lane2x = lax.broadcasted_iota(jnp.int32, (LANES, LANES), 1)
    lo2x = lane2x < c
    row4 = lax.broadcasted_iota(jnp.int32, (c, 2 * LANES), 0)
    lane4 = lax.broadcasted_iota(jnp.int32, (c, 2 * LANES), 1)
    eye4 = ((lane4 & (c - 1)) == row4).astype(F32)
    blk4 = lane4 // c
    zero_k = jnp.zeros((c, DN_HEAD_DIM), BF16)

    def block_diag(m):
        return jnp.concatenate([jnp.where(blk4 == i, m, 0.0) for i in range(4)], axis=0).astype(BF16)

    def l2n(m):
        return m * lax.rsqrt(jnp.sum(m * m, axis=-1, keepdims=True) + EPS)

    probs = [(ch, h) for ch in range(n_ch) for h in range(n_heads)]
    pairs = [(ch, p) for ch in range(n_ch) for p in range(n_pair)]
    quads = [(ch, qd) for ch in range(n_ch) for qd in range(n_pair // 2)]
    pix = {ph: i for i, ph in enumerate(probs)}

    n_prob = len(probs)
    staged = {name: [None] * n_prob for name in ("kbq", "k", "rhs", "kd", "qeg")}
    common = {}
    ba_piece = 4 * n_pair

    def prep_common():
        ba = parts[ba_piece][:, :LANES]
        common["beta"] = jax.nn.sigmoid(ba)
        glog_all = -aexp_ref[...] * _softplus(ba + dtb_ref[...])
        common["gam"] = [_cumsum_rows(glog_all[ch * c:(ch + 1) * c]) for ch in range(n_ch)]
        common["gam_t"] = [jnp.concatenate([g_, g_], axis=0).T for g_ in common["gam"]]

    def prep_pair(p):
        def conv_silu(section):
            blk = parts[section * n_pair + p]
            cs = slice(section * width + p * MXU_DIM, section * width + (p + 1) * MXU_DIM)
            y = _causal_conv(blk, tail_ref[:, cs], cw_ref[:, cs])
            tail_ref[:, cs] = blk[tt - SUBLANES:tt]
            return y * jax.nn.sigmoid(y)

        qs, ks, vs = conv_silu(0), conv_silu(1), conv_silu(2)
        for ch in range(n_ch):
            rs = slice(ch * c, (ch + 1) * c)
            for sub in range(2):
                h = 2 * p + sub
                ls = slice(sub * DN_HEAD_DIM, (sub + 1) * DN_HEAD_DIM)
                q_ = l2n(qs[rs, ls]) * (DN_HEAD_DIM ** -0.5)
                k_ = l2n(ks[rs, ls])
                beta = common["beta"][rs, h:h + 1]
                gc = common["gam"][ch][:, n_heads + h:n_heads + h + 1]
                eg_ = jnp.exp(gc)
                kb_ = k_ * beta
                i = pix[(ch, h)]
                staged["rhs"][i] = jnp.concatenate([vs[rs, ls] * beta, kb_ * eg_], axis=1).astype(BF16)
                staged["kbq"][i] = jnp.concatenate([kb_, q_], axis=0).astype(BF16)
                staged["kd"][i] = k_ * jnp.exp(gc[c - 1:c, :] - gc)
                staged["qeg"][i] = (q_ * eg_).astype(BF16)
                staged["k"][i] = k_.astype(BF16)

    fill = iter([
        lambda: project(ba_piece, 0, n_pair, 2 * n_pair),
        lambda: (prep_common(), project(1, n_pair + 1, 2 * n_pair + 1)),
        lambda: prep_pair(0),
        lambda: project(2, n_pair + 2, 2 * n_pair + 2),
        lambda: prep_pair(1),
        lambda: project(3, n_pair + 3, 2 * n_pair + 3),
        lambda: prep_pair(2),
        lambda: project(3 * n_pair, 3 * n_pair + 1),
        lambda: prep_pair(3),
        lambda: project(3 * n_pair + 2),
        lambda: project(3 * n_pair + 3),
    ])
    assert n_pair == 4

    def interleave():
        action = next(fill, None)
        if action is not None:
            action()

    kbq = [kbq_s[rd, i] for i in range(n_prob)]
    k = [k_s[rd, i] for i in range(n_prob)]
    rhs = [rhs_s[rd, i] for i in range(n_prob)]
    kd = [kd_s[rd, i] for i in range(n_prob)]
    qeg = [qeg_s[rd, i] for i in range(n_prob)]
    gam_all = [gam_s[rd, ch] for ch in range(n_ch)]
    gam_t = [gamt_s[rd, ch] for ch in range(n_ch)]
    z = z_s[rd]
    gcol = [gam_all[ch][:, n_heads + h:n_heads + h + 1] for ch, h in probs]
    glast = [gc[c - 1:c, :] for gc in gcol]

    def both(lst, ch, p):
        return lst[pix[(ch, 2 * p)]], lst[pix[(ch, 2 * p + 1)]]

    interleave()
    kq = []
    for ch, p in pairs:
        (x0, x1), (k0, k1) = both(kbq, ch, p), both(k, ch, p)
        kq.append(_dot_nt(x0, jnp.concatenate([k0.astype(BF16), zero_k], axis=0))
                  + _dot_nt(x1, jnp.concatenate([zero_k, k1.astype(BF16)], axis=0)))
    interleave()
    decay = []
    for ch, p in pairs:
        g0, g1 = both(gcol, ch, p)
        gr = jnp.where(lo2[0:1], gam_t[ch][n_heads + 2 * p:n_heads + 2 * p + 1, :],
                       gam_t[ch][n_heads + 2 * p + 1:n_heads + 2 * p + 2, :])
        diff = jnp.where(lo2, g0, g1) - gr
        decay.append(jnp.where(incl, jnp.exp(jnp.where(incl, diff, 0.0)), 0.0))
    a2 = [jnp.where(strict, kq_[:c] * d_, 0.0) for kq_, d_ in zip(kq, decay)]
    qk2 = [jnp.where(incl, kq_[c:] * d_, 0.0) for kq_, d_ in zip(kq, decay)]

    a4 = [jnp.concatenate([a2[ch * n_pair + 2 * qd], a2[ch * n_pair + 2 * qd + 1]], axis=1) for ch, qd in quads]
    tinv = [eye4 - a_ for a_ in a4]
    pw = [_dot(a_.astype(BF16), block_diag(a_)) for a_ in a4]
    interleave()
    for _ in range(int(math.log2(c)) - 2):
        r = [_dot(jnp.concatenate([t_, p_], axis=0).astype(BF16), block_diag(p_)) for t_, p_ in zip(tinv, pw)]
        tinv = [t_ + r_[:c] for t_, r_ in zip(tinv, r)]
        pw = [r_[c:] for r_ in r]
        interleave()
    tinv = [t_ + _dot(t_.astype(BF16), block_diag(p_)) for t_, p_ in zip(tinv, pw)]
    interleave()

    def split_rows(m):
        return jnp.concatenate([jnp.where(lo2, m, 0.0), jnp.where(lo2, 0.0, m)], axis=0).astype(BF16)

    sol, lhs_o = [], []
    for i, (ch, p) in enumerate(pairs):
        t2 = tinv[i // 2][:, (i % 2) * LANES:(i % 2 + 1) * LANES]
        r0, r1 = both(rhs, ch, p)
        sol.append(_dot(split_rows(t2), jnp.concatenate([r0, r1], axis=0)))
        kd0, kd1 = both(kd, ch, p)
        kdt = jnp.concatenate([kd0, kd1], axis=0).T
        lhs_o.append(jnp.concatenate([split_rows(qk2[i]), jnp.where(lo2x, kdt, 0.0).astype(BF16),
                                      jnp.where(lo2x, 0.0, kdt).astype(BF16)], axis=0))
    wq = []
    for ch, h in probs:
        s_ = sol[ch * n_pair + h // 2][(h % 2) * c:(h % 2 + 1) * c]
        i = pix[(ch, h)]
        wq.append(jnp.concatenate([s_[:, DN_HEAD_DIM:].astype(BF16), qeg[i]], axis=0))
    interleave()

    for ch in range(n_ch):
        rs = slice(ch * c, (ch + 1) * c)
        state = [s_scr[h] for h in range(n_heads)]
        ws = [_dot(wq[pix[(ch, h)]], state[h].astype(BF16)) for h in range(n_heads)]
        interleave()
        un = [sol[ch * n_pair + h // 2][(h % 2) * c:(h % 2 + 1) * c, :DN_HEAD_DIM] - ws[h][:c] for h in range(n_heads)]
        ou = [_dot(lhs_o[ch * n_pair + p], jnp.concatenate([un[2 * p], un[2 * p + 1]], axis=0).astype(BF16))
              for p in range(n_pair)]
        interleave()
        for h in range(n_heads):
            hs = slice(h * DN_HEAD_DIM, (h + 1) * DN_HEAD_DIM)
            o2 = ou[h // 2]
            sub = h % 2
            s_scr[h] = (state[h] * jnp.exp(glast[pix[(ch, h)]])
                        + o2[2 * c + sub * DN_HEAD_DIM:2 * c + (sub + 1) * DN_HEAD_DIM])
            on = _rms(ws[h][c:] + o2[sub * c:(sub + 1) * c], on_ref[:, hs])
            zz = z[rs, hs]
            o_ref[rs, hs] = (on * (zz * jax.nn.sigmoid(zz))).astype(BF16)

    for action in fill:
        action()
    for ref, vals in ((kbq_s, staged["kbq"]), (k_s, staged["k"]), (rhs_s, staged["rhs"]), (kd_s, staged["kd"]),
                      (qeg_s, staged["qeg"]), (gam_s, common["gam"]), (gamt_s, common["gam_t"])):
        for i, v in enumerate(vals):
            ref[wr, i] = v
    z_s[wr] = jnp.concatenate([parts[3 * n_pair + j] for j in range(n_pair)], axis=1)


def _dn_branch(x, g, w_in, layer, w_block, cw, aexp, dtb, onorm, *, tt=256):
    bsz, t, d = x.shape
    width = onorm.shape[1]
    n_heads = width // DN_HEAD_DIM
    w_cols = 4 * width + 2 * LANES
    tiles_per_seq = t // tt
    n_tiles = bsz * tiles_per_seq
    n_prob = (tt // DN_CHUNK) * n_heads
    c = DN_CHUNK

    def x_map(s):
        j = jnp.minimum(s, n_tiles - 1)
        return j // tiles_per_seq, j % tiles_per_seq, 0

    def o_map(s):
        j = jnp.maximum(s - 1, 0)
        return j // tiles_per_seq, j % tiles_per_seq, 0

    assert tiles_per_seq > 1
    return pl.pallas_call(
        functools.partial(_dn_kernel, tiles_per_seq=tiles_per_seq),
        grid=(n_tiles + 1,),
        in_specs=[pl.BlockSpec((None, tt, d), x_map), _const_spec((1, d)),
                  _col_block_spec(d, w_cols, w_block, layer), _const_spec(cw.shape),
                  _const_spec((1, LANES)), _const_spec((1, LANES)), _const_spec((1, width))],
        out_specs=pl.BlockSpec((None, tt, width), o_map),
        out_shape=jax.ShapeDtypeStruct((bsz, t, width), BF16),
        scratch_shapes=[pltpu.VMEM((SUBLANES, 3 * width), F32),
                        pltpu.VMEM((n_heads, DN_HEAD_DIM, DN_HEAD_DIM), F32),
                        pltpu.VMEM((2, n_prob, 2 * c, DN_HEAD_DIM), BF16),
                        pltpu.VMEM((2, n_prob, c, DN_HEAD_DIM), BF16),
                        pltpu.VMEM((2, n_prob, c, 2 * DN_HEAD_DIM), BF16),
                        pltpu.VMEM((2, n_prob, c, DN_HEAD_DIM), F32),
                        pltpu.VMEM((2, n_prob, c, DN_HEAD_DIM), BF16),
                        pltpu.VMEM((2, tt // c, c, LANES), F32),
                        pltpu.VMEM((2, tt // c, LANES, LANES), F32),
                        pltpu.VMEM((2, tt, width), F32)],
        compiler_params=_params(("arbitrary",)),
        name="dn_branch",
    )(x, g, w_in, cw, aexp, dtb, onorm)


def _merge_ffn_kernel(x_ref, gm_ref, ya_ref, yb_ref, yc_ref, wm_ref, wa_ref, wb_ref, wc_ref, wo_ref,
                      gf_ref, wg_ref, wu_ref, wd_ref, *rest, f_chunk):
    n_cast = (len(rest) - 1) // 2
    o_ref = rest[n_cast]
    for src, dst in zip(rest[:n_cast], rest[n_cast + 1:]):
        dst[...] = src[...].astype(BF16)
    x = x_ref[...]
    d = x.shape[1]
    u = _rms(x, gm_ref[...]).astype(BF16)
    y = jnp.zeros_like(x)
    for j, (yr, wr) in enumerate(((ya_ref, wa_ref), (yb_ref, wb_ref), (yc_ref, wc_ref))):
        gate = jax.nn.sigmoid(_dot(u, wm_ref[:, j * d:(j + 1) * d]))
        y = y + gate * _dot(yr[...], wr[...])
    x = x + _dot(y.astype(BF16), wo_ref[...])
    u = _rms(x, gf_ref[...]).astype(BF16)
    acc = jnp.zeros_like(x)
    for c in range(wg_ref.shape[1] // f_chunk):
        sl = slice(c * f_chunk, (c + 1) * f_chunk)
        hg = _dot(u, wg_ref[:, sl])
        hu = _dot(u, wu_ref[:, sl])
        h = (hg * jax.nn.sigmoid(hg) * hu).astype(BF16)
        acc = acc + _dot(h, wd_ref[sl, :])
    o_ref[...] = x + 0.5 * acc


def _merge_ffn(x2, gm, ya, yb, yc, wm, wa, wb, wc, wo, gf, wg, wu, wd, casts=(), *, tm=512, f_chunk=256):
    n, d = x2.shape
    n_steps = n // tm
    row = lambda i: (i, 0)
    cast_in, cast_out, cast_shapes = _cast_specs(casts, n_steps)
    resident = [_const_spec(w.shape, True) for w in (wm, wa, wb, wc, wo)]
    outs = pl.pallas_call(
        functools.partial(_merge_ffn_kernel, f_chunk=f_chunk),
        grid=(n_steps,),
        in_specs=[pl.BlockSpec((tm, d), row), _const_spec((1, d)),
                  pl.BlockSpec((tm, ya.shape[1]), row), pl.BlockSpec((tm, yb.shape[1]), row),
                  pl.BlockSpec((tm, yc.shape[1]), row)] + resident
                 + [_const_spec((1, d))] + [_const_spec(w.shape, True) for w in (wg, wu, wd)] + cast_in,
        out_specs=[pl.BlockSpec((tm, d), row)] + cast_out,
        out_shape=[jax.ShapeDtypeStruct((n, d), F32)] + cast_shapes,
        compiler_params=_params(("arbitrary",)),
        name="merge_ffn",
    )(x2, gm, ya, yb, yc, wm, wa, wb, wc, wo, gf, wg, wu, wd, *[stack for stack, _ in casts])
    return outs[0], list(outs[1:])


def _block_diag_tiles(w):
    nb, k, _ = w.shape
    per = MXU_DIM // k
    w4 = w.reshape(nb // per, per, k, k)
    eye = jnp.eye(per, dtype=w.dtype)
    return jnp.einsum('tpij,pq->tpiqj', w4, eye).reshape(nb // per, MXU_DIM, MXU_DIM)


def _row(v):
    return v.reshape(1, -1).astype(F32)


def _mixer_ffn(x, layer, mix_norm, w_in, rg_conv_w, rg_conv_b, rg_w_r, rg_b_r, rg_w_i, rg_b_i, rg_lambda,
               att_q_norm, att_k_norm, dn_conv_w, dn_a_log, dn_dt_bias, dn_out_norm, w_branch, w_out,
               ffn_norm, ffn_w, casts):
    bsz, t, d = x.shape
    rg_w = rg_conv_w.shape[1]
    att_w = att_q_norm.size
    dn_w = dn_out_norm.size
    dn_h = dn_out_norm.shape[0]
    att_off = 2 * rg_w
    dn_off = att_off + 3 * att_w
    merge_off = dn_off + 4 * dn_w + 2 * dn_h
    dn_cols = 4 * dn_w + 2 * LANES
    assert dn_off % dn_cols == 0 and dn_off + dn_cols <= w_in.shape[2] and 2 * dn_h <= LANES
    w_merge = w_in[layer, :, merge_off:merge_off + 3 * d]
    g = _row(mix_norm)

    lamc = _row(-RG_C * jax.nn.softplus(-rg_lambda.astype(F32)))
    ya = _rg_branch(x, g, w_in, layer, rg_conv_w, _row(rg_conv_b),
                    _block_diag_tiles(rg_w_r).astype(BF16), _row(rg_b_r),
                    _block_diag_tiles(rg_w_i).astype(BF16), _row(rg_b_i), lamc)

    n_att_heads = att_q_norm.shape[0]
    yb = _att_branch(x, g, w_in, layer, att_off, _row(att_q_norm), _row(att_k_norm),
                     _head_mean_matrix(2 * LANES, ATT_HEAD_DIM), _att_maskbias(n_att_heads))

    pad = jnp.zeros((LANES - 2 * dn_h,), F32)
    aexp = _row(jnp.concatenate([jnp.zeros((dn_h,), F32), jnp.exp(dn_a_log.astype(F32)), pad]))
    dtb = _row(jnp.concatenate([jnp.zeros((dn_h,), F32), dn_dt_bias.astype(F32), pad]))
    yc = _dn_branch(x, g, w_in, layer, dn_off // dn_cols, dn_conv_w, aexp, dtb, _row(dn_out_norm))

    wa = w_branch[:rg_w]
    wb = w_branch[rg_w:rg_w + att_w]
    wc = w_branch[rg_w + att_w:]
    n = bsz * t
    out, cast = _merge_ffn(x.reshape(n, d), g, ya.reshape(n, rg_w), yb.reshape(n, att_w), yc.reshape(n, dn_w),
                           w_merge, wa, wb, wc, w_out, _row(ffn_norm), *ffn_w, casts)
    return out.reshape(bsz, t, d), cast


def kernel(x, ffn1_norm, ffn1_w_gate, ffn1_w_up, ffn1_w_down, mix_norm, w_in, rg_conv_w, rg_conv_b, rg_w_r, rg_b_r, rg_w_i, rg_b_i, rg_lambda, att_q_norm, att_k_norm, dn_conv_w, dn_a_log, dn_dt_bias, dn_out_norm, w_branch, w_out, ffn2_norm, ffn2_w_gate, ffn2_w_up, ffn2_w_down):
    bsz, t, d = x.shape
    n = bsz * t
    n_layers = ffn1_norm.shape[0]
    ffn1_w = [w[0].astype(BF16) for w in (ffn1_w_gate, ffn1_w_up, ffn1_w_down)]
    w_in_all = w_in.astype(BF16)
    for l in range(n_layers):
        casts = [(w, l) for w in (w_branch, w_out, ffn2_w_gate, ffn2_w_up, ffn2_w_down)]
        x, (w_branch_b, w_out_b, *ffn2_w) = _ffn(x.reshape(n, d), _row(ffn1_norm[l]), *ffn1_w, casts)
        casts = [(w, l + 1) for w in (ffn1_w_gate, ffn1_w_up, ffn1_w_down)] if l + 1 < n_layers else []
        x, ffn1_w = _mixer_ffn(x.reshape(bsz, t, d), l, mix_norm[l], w_in_all, rg_conv_w[l], rg_conv_b[l], rg_w_r[l],
                               rg_b_r[l], rg_w_i[l], rg_b_i[l], rg_lambda[l], att_q_norm[l], att_k_norm[l],
                               dn_conv_w[l], dn_a_log[l], dn_dt_bias[l], dn_out_norm[l], w_branch_b, w_out_b,
                               ffn2_norm[l], ffn2_w, casts)
    return x
```

```python
import functools
import math

import jax
import jax.numpy as jnp
from jax import lax
from jax.experimental import pallas as pl
from jax.experimental.pallas import tpu as pltpu

F32 = jnp.float32
BF16 = jnp.bfloat16

EPS = 1e-6
NEG_INF = -1e30
RG_C = 8.0
RG_BLOCK_DIM = 64
CONV_WIDTH = 4
ATT_GROUPS = ((128, 1), (512, 4), (2048, 16))
ATT_HEADS_PER_GROUP = 4
ATT_HEAD_DIM = 64
ATT_SPAN = 128
DN_HEAD_DIM = 128
DN_CHUNK = 64

LANES = 128
SUBLANES = 8
MXU_DIM = 256
VMEM_LIMIT_BYTES = 60 * 1024 * 1024


def _rms(x, g):
    return x * lax.rsqrt(jnp.mean(x * x, axis=-1, keepdims=True) + EPS) * g


def _dot(a, b):
    return jnp.dot(a, b, preferred_element_type=F32)


def _dot_nt(a, b):
    return lax.dot_general(a, b, (((1,), (1,)), ((), ())), preferred_element_type=F32)


def _dot_tn(a, b):
    return lax.dot_general(a, b, (((0,), (0,)), ((), ())), preferred_element_type=F32)


def _softplus(z):
    return jnp.maximum(z, 0.0) + jnp.log(1.0 + jnp.exp(-jnp.abs(z)))


def _const_spec(shape, single_buffer=False):
    nd = len(shape)
    mode = pl.Buffered(1) if single_buffer else None
    return pl.BlockSpec(shape, lambda *_: (0,) * nd, pipeline_mode=mode)


def _col_block_spec(rows, cols, col_block, layer, single_buffer=False):
    mode = pl.Buffered(1) if single_buffer else None
    return pl.BlockSpec((None, rows, cols), lambda *_: (layer, 0, col_block), pipeline_mode=mode)


def _params(semantics):
    return pltpu.CompilerParams(dimension_semantics=semantics, vmem_limit_bytes=VMEM_LIMIT_BYTES)


BF16_SUBLANES = 16


def _cast_blocking(rows, n_steps):
    rpb = BF16_SUBLANES
    while rows % rpb or rows // rpb > n_steps:
        rpb += BF16_SUBLANES
        assert rpb <= rows
    return rpb, rows // rpb


def _cast_specs(casts, n_steps):
    cast_in, cast_out, cast_shapes = [], [], []
    for stack, layer in casts:
        _, rows, cols = stack.shape
        rpb, nb = _cast_blocking(rows, n_steps)
        cast_in.append(pl.BlockSpec((None, rpb, cols), lambda i, layer=layer, nb=nb: (layer, jnp.minimum(i, nb - 1), 0)))
        cast_out.append(pl.BlockSpec((rpb, cols), lambda i, nb=nb: (jnp.minimum(i, nb - 1), 0)))
        cast_shapes.append(jax.ShapeDtypeStruct((rows, cols), BF16))
    return cast_in, cast_out, cast_shapes


def _ffn_kernel(x_ref, g_ref, wg_ref, wu_ref, wd_ref, *rest, f_chunk):
    n_cast = (len(rest) - 1) // 2
    o_ref = rest[n_cast]
    for src, dst in zip(rest[:n_cast], rest[n_cast + 1:]):
        dst[...] = src[...].astype(BF16)
    x = x_ref[...]
    u = _rms(x, g_ref[...]).astype(BF16)
    acc = jnp.zeros_like(x)
    for c in range(wg_ref.shape[1] // f_chunk):
        sl = slice(c * f_chunk, (c + 1) * f_chunk)
        hg = _dot(u, wg_ref[:, sl])
        hu = _dot(u, wu_ref[:, sl])
        h = (hg * jax.nn.sigmoid(hg) * hu).astype(BF16)
        acc = acc + _dot(h, wd_ref[sl, :])
    o_ref[...] = x + 0.5 * acc


def _ffn(x2, g, wg, wu, wd, casts=(), *, tm=1024, f_chunk=256):
    n, d = x2.shape
    f = wg.shape[1]
    n_steps = n // tm
    cast_in, cast_out, cast_shapes = _cast_specs(casts, n_steps)
    outs = pl.pallas_call(
        functools.partial(_ffn_kernel, f_chunk=f_chunk),
        grid=(n_steps,),
        in_specs=[pl.BlockSpec((tm, d), lambda i: (i, 0)), _const_spec((1, d)),
                  _const_spec((d, f), True), _const_spec((d, f), True), _const_spec((f, d), True)] + cast_in,
        out_specs=[pl.BlockSpec((tm, d), lambda i: (i, 0))] + cast_out,
        out_shape=[jax.ShapeDtypeStruct((n, d), F32)] + cast_shapes,
        compiler_params=_params(("arbitrary",)),
        name="ffn",
    )(x2, g, wg, wu, wd, *[stack for stack, _ in casts])
    return outs[0], list(outs[1:])


def _shift_rows(cur, prev8, s):
    rows = lax.broadcasted_iota(jnp.int32, cur.shape, 0)
    return jnp.where(rows < s, pltpu.roll(prev8, s, 0), pltpu.roll(cur, s, 0))


def _causal_conv(p, tail8, w):
    kw = w.shape[0]
    acc = p * w[kw - 1:kw, :]
    top = p[0:SUBLANES] * w[kw - 1:kw, :]
    for s in range(1, kw):
        wk = w[kw - 1 - s:kw - s, :]
        acc = acc + pltpu.roll(p, s, 0) * wk
        top = top + _shift_rows(p[0:SUBLANES], tail8, s) * wk
    return jnp.concatenate([top, acc[SUBLANES:]], axis=0)


def _segment_perm(tt):
    r = jnp.arange(tt)
    src = (r % SUBLANES) * (tt // SUBLANES) + r // SUBLANES
    return (src[:, None] == jnp.arange(tt)[None, :]).astype(BF16)


def _rg_kernel(x_ref, g_ref, perm_ref, permt_ref, wx_ref, wgate_ref, cw_ref, cb_ref, wr_ref, br_ref, wi_ref,
               bi_ref, lamc_ref, o_ref, tail_ref, carry_ref):
    rows, width = o_ref.shape
    tt = perm_ref.shape[0]
    n_sub = rows // tt
    n_grp = tt // SUBLANES
    kw = cw_ref.shape[0]

    @pl.when(pl.program_id(1) == 0)
    def _():
        tail_ref[...] = jnp.zeros_like(tail_ref)
        carry_ref[...] = jnp.zeros_like(carry_ref)

    u = _rms(x_ref[...], g_ref[...]).astype(BF16)
    nc = width // MXU_DIM
    rows8 = lax.broadcasted_iota(jnp.int32, (SUBLANES, MXU_DIM), 0)
    up = {}
    tails = {c: tail_ref[:, c * MXU_DIM:(c + 1) * MXU_DIM] for c in range(nc)}
    carries = {c: carry_ref[:, c * MXU_DIM:(c + 1) * MXU_DIM] for c in range(nc)}

    def project(t, c):
        if t not in up:
            up[t] = _dot(perm_ref[...], u[t * tt:(t + 1) * tt]).astype(BF16)
        cs = slice(c * MXU_DIM, (c + 1) * MXU_DIM)
        return _dot(up[t], wx_ref[:, cs]), _dot(up[t], wgate_ref[:, cs])

    def conv(c, px):
        cs = slice(c * MXU_DIM, (c + 1) * MXU_DIM)
        tail = tails[c]
        head = [_shift_rows(px[(n_grp - k) * SUBLANES:(n_grp - k + 1) * SUBLANES],
                            tail[(kw - 1 - k) * SUBLANES:(kw - k) * SUBLANES], 1) for k in range(kw - 1, 0, -1)]
        ext = jnp.concatenate(head + [px], axis=0)
        tails[c] = px[tt - (kw - 1) * SUBLANES:tt]
        xa = cb_ref[:, cs] + ext[0:tt] * cw_ref[0:1, cs]
        for k in range(1, kw):
            xa = xa + ext[k * SUBLANES:k * SUBLANES + tt] * cw_ref[k:k + 1, cs]
        return xa

    def gates(c, xa):
        xab = xa.astype(BF16)
        return _dot(xab, wr_ref[c]), _dot(xab, wi_ref[c])

    def scan(c, xa, r_lin, i_lin, pg):
        cs = slice(c * MXU_DIM, (c + 1) * MXU_DIM)
        a = jnp.exp(lamc_ref[:, cs] * jax.nn.sigmoid(r_lin + br_ref[:, cs]))
        b = jnp.sqrt(1.0 - a * a) * (jax.nn.sigmoid(i_lin + bi_ref[:, cs]) * xa)
        hl = jnp.zeros((SUBLANES, MXU_DIM), F32)
        ap = jnp.ones((SUBLANES, MXU_DIM), F32)
        hs, aps = [], []
        for j in range(n_grp):
            rs = slice(j * SUBLANES, (j + 1) * SUBLANES)
            hl = a[rs] * hl + b[rs]
            ap = a[rs] * ap
            hs.append(hl)
            aps.append(ap)
        for s in (1, 2, 4):
            m = rows8 >= s
            hl = jnp.where(m, ap * pltpu.roll(hl, s, 0) + hl, hl)
            ap = jnp.where(m, ap * pltpu.roll(ap, s, 0), ap)
        cin = carries[c]
        seg_end = hl + ap * cin
        carries[c] = jnp.broadcast_to(seg_end[SUBLANES - 1:SUBLANES, :], seg_end.shape)
        seg_in = jnp.where(rows8 < 1, cin, pltpu.roll(seg_end, 1, 0))
        h = jnp.concatenate([h_ + a_ * seg_in for h_, a_ in zip(hs, aps)], axis=0)
        return (h * jax.nn.gelu(pg)).astype(BF16)

    def emit(t, c, y):
        o_ref[t * tt:(t + 1) * tt, c * MXU_DIM:(c + 1) * MXU_DIM] = _dot(permt_ref[...], y).astype(BF16)

    items = [(t, c) for t in range(n_sub) for c in range(nc)]
    proj, xas, gts, ys = {}, {}, {}, {}
    for step in range(len(items) + 3):
        if step < len(items):
            proj[step] = project(*items[step])
        k = step - 1
        if 0 <= k < len(items):
            xas[k] = conv(items[k][1], proj[k][0])
            gts[k] = gates(items[k][1], xas[k])
        k = step - 2
        if 0 <= k < len(items):
            ys[k] = scan(items[k][1], xas[k], gts[k][0], gts[k][1], proj[k][1])
        k = step - 3
        if 0 <= k < len(items):
            emit(*items[k], ys[k])
    for c in range(nc):
        tail_ref[:, c * MXU_DIM:(c + 1) * MXU_DIM] = tails[c]
        carry_ref[:, c * MXU_DIM:(c + 1) * MXU_DIM] = carries[c]


def _rg_branch(x, g, w_in, layer, cw, cb, wr_t, br, wi_t, bi, lamc, *, tt=256, n_sub=4):
    bsz, t, d = x.shape
    width = cw.shape[1]
    perm = _segment_perm(tt)
    rows = tt * n_sub
    return pl.pallas_call(
        _rg_kernel,
        grid=(bsz, t // rows),
        in_specs=[pl.BlockSpec((None, rows, d), lambda b, i: (b, i, 0)), _const_spec((1, d)),
                  _const_spec((tt, tt)), _const_spec((tt, tt)),
                  _col_block_spec(d, width, 0, layer), _col_block_spec(d, width, 1, layer),
                  _const_spec(cw.shape), _const_spec((1, width)),
                  _const_spec(wr_t.shape), _const_spec((1, width)),
                  _const_spec(wi_t.shape), _const_spec((1, width)), _const_spec((1, width))],
        out_specs=pl.BlockSpec((None, rows, width), lambda b, i: (b, i, 0)),
        out_shape=jax.ShapeDtypeStruct((bsz, t, width), BF16),
        scratch_shapes=[pltpu.VMEM(((cw.shape[0] - 1) * SUBLANES, width), F32), pltpu.VMEM((SUBLANES, width), F32)],
        compiler_params=_params(("arbitrary", "arbitrary")),
        name="rg_branch",
    )(x, g, perm, perm.T, w_in, w_in, cw, cb, wr_t, br, wi_t, bi, lamc)


ATT_BLOCK_BATCH = 4


def _att_kernel(x_ref, g_ref, *rest, rt, n_w):
    w_refs = rest[:n_w]
    gq_ref, gk_ref, e4_ref, mb_ref, o_ref, qkv_scr, o_scr, lse_scr = rest[n_w:]
    t_total = o_ref.shape[0]
    n_slab = o_scr.shape[0]
    i = pl.program_id(1)
    row0 = pl.multiple_of(i * rt, rt)
    span = ATT_SPAN

    u = _rms(x_ref[...], g_ref[...]).astype(BF16)
    qkv = jnp.concatenate([_dot(u, w_ref[...]) for w_ref in w_refs], axis=1)
    width = n_slab * LANES
    e4 = e4_ref[...]
    for s in range(n_slab):
        cs = slice(s * LANES, (s + 1) * LANES)
        qk = jnp.concatenate([qkv[:, cs], qkv[:, width + s * LANES: width + (s + 1) * LANES]], axis=1)
        gain = jnp.concatenate([gq_ref[:, cs] * (ATT_HEAD_DIM ** -0.5), gk_ref[:, cs]], axis=1)
        ms = _dot((qk * qk).astype(BF16), e4)
        qk = qk * lax.rsqrt(ms + EPS) * gain
        qkv_scr[s, pl.ds(row0, rt), :] = qk[:, :LANES]
        qkv_scr[n_slab + s, pl.ds(row0, rt), :] = qk[:, LANES:]
        qkv_scr[2 * n_slab + s, pl.ds(row0, rt), :] = qkv[:, 2 * width + s * LANES: 2 * width + (s + 1) * LANES]

    @pl.when(i == pl.num_programs(1) - 1)
    def _():
        lane = lax.broadcasted_iota(jnp.int32, (span, LANES), 1)
        first_head = lane < ATT_HEAD_DIM

        for g, (window, dil) in enumerate(ATT_GROUPS):
            assert window // dil == span
            nblk = t_total // dil // span
            blocks = [(r, n) for n in range(nblk) for r in range(dil)]

            def rows(start, dil=dil):
                return pl.ds(start, span) if dil == 1 else pl.ds(start, span, stride=dil)

            for b0 in range(0, len(blocks), ATT_BLOCK_BATCH):
                probs = [(r + n * span * dil, n > 0, 2 * g + sl)
                         for r, n in blocks[b0:b0 + ATT_BLOCK_BATCH] for sl in range(2)]
                qs, kk, vv = [], [], []
                for start, has_prev, slab in probs:
                    q2 = qkv_scr[slab, rows(start), :]
                    qs.append(jnp.concatenate([jnp.where(first_head, q2, 0.0), jnp.where(first_head, 0.0, q2)],
                                              axis=0).astype(BF16))
                    kc = qkv_scr[n_slab + slab, rows(start), :]
                    vc = qkv_scr[2 * n_slab + slab, rows(start), :]
                    if has_prev:
                        prev = start - span * dil
                        kc = jnp.concatenate([qkv_scr[n_slab + slab, rows(prev), :], kc], axis=0)
                        vc = jnp.concatenate([qkv_scr[2 * n_slab + slab, rows(prev), :], vc], axis=0)
                    kk.append(kc.astype(BF16))
                    vv.append(vc.astype(BF16))
                sc = [_dot_nt(q_, k_) for q_, k_ in zip(qs, kk)]
                sc = [s_ + (mb_ref[slab] if has_prev else mb_ref[slab, :, span:])
                      for s_, (_, has_prev, slab) in zip(sc, probs)]
                mx = [jnp.max(s_, axis=-1, keepdims=True) for s_ in sc]
                pr = [jnp.exp(s_ - m_) for s_, m_ in zip(sc, mx)]
                den = [jnp.sum(p_, axis=-1, keepdims=True) for p_ in pr]
                pv = [_dot(p_.astype(BF16), v_) for p_, v_ in zip(pr, vv)]
                for (start, _, slab), pv_, m_, d_ in zip(probs, pv, mx, den):
                    on = pv_ * (1.0 / d_)
                    lse = jnp.broadcast_to(m_ + jnp.log(d_), on.shape)
                    o_scr[slab, rows(start), :] = jnp.where(first_head, on[:span], on[span:])
                    lse_scr[slab, rows(start), :] = jnp.where(first_head, lse[:span], lse[span:])

        n_grp = len(ATT_GROUPS)
        ct = 256

        def combine(j, carry):
            rws = pl.ds(pl.multiple_of(j * ct, ct), ct)
            for sl in range(2):
                ls = [lse_scr[2 * g + sl, rws, :] for g in range(n_grp)]
                mx = functools.reduce(jnp.maximum, ls)
                es = [jnp.exp(l - mx) for l in ls]
                inv = 1.0 / functools.reduce(lambda a, b: a + b, es)
                for g in range(n_grp):
                    slab = 2 * g + sl
                    o_ref[rws, slab * LANES:(slab + 1) * LANES] = (o_scr[slab, rws, :] * (es[g] * inv)).astype(BF16)
            return carry

        lax.fori_loop(0, t_total // ct, combine, 0)


def _att_branch(x, g, w_in, layer, w_off, gq, gk, e4, maskbias, *, rt=512):
    bsz, t, d = x.shape
    width = gq.shape[1]
    n_slab = width // LANES
    assert w_off % MXU_DIM == 0 and (3 * width) % MXU_DIM == 0
    n_w = 3 * width // MXU_DIM
    return pl.pallas_call(
        functools.partial(_att_kernel, rt=rt, n_w=n_w),
        grid=(bsz, t // rt),
        in_specs=[pl.BlockSpec((None, rt, d), lambda b, i: (b, i, 0)), _const_spec((1, d))]
                 + [_col_block_spec(d, MXU_DIM, w_off // MXU_DIM + j, layer, True) for j in range(n_w)]
                 + [_const_spec((1, width)), _const_spec((1, width)),
                    _const_spec(e4.shape), _const_spec(maskbias.shape, True)],
        out_specs=pl.BlockSpec((None, t, width), lambda b, i: (b, 0, 0)),
        out_shape=jax.ShapeDtypeStruct((bsz, t, width), BF16),
        scratch_shapes=[pltpu.VMEM((3 * n_slab, t, LANES), F32), pltpu.VMEM((n_slab, t, LANES), F32),
                        pltpu.VMEM((n_slab, t, LANES), F32)],
        compiler_params=_params(("arbitrary", "arbitrary")),
        name="att_branch",
    )(x, g, *([w_in] * n_w), gq, gk, e4, maskbias)


def _att_maskbias(n_heads):
    slopes = jnp.exp2(-8.0 * jnp.arange(1, n_heads + 1, dtype=F32) / n_heads)
    qi = jnp.arange(ATT_SPAN)[:, None]
    kj = jnp.arange(2 * ATT_SPAN)[None, :]
    delta = qi + ATT_SPAN - kj
    valid = (delta >= 0) & (delta <= ATT_SPAN)
    dil = jnp.repeat(jnp.array([d for _, d in ATT_GROUPS], F32), ATT_HEADS_PER_GROUP)
    bias = -(slopes * dil)[:, None, None] * delta.astype(F32)[None]
    return jnp.where(valid[None], bias, NEG_INF).reshape(n_heads // 2, 2 * ATT_SPAN, 2 * ATT_SPAN)


def _head_mean_matrix(n_lanes, head_dim):
    lane_head = jnp.arange(n_lanes) // head_dim
    return ((lane_head[:, None] == lane_head[None, :]).astype(F32) / head_dim).astype(BF16)


def _cumsum_rows(x):
    n = x.shape[0]
    rows = lax.broadcasted_iota(jnp.int32, x.shape, 0)
    s = 1
    while s < n:
        x = x + jnp.where(rows >= s, pltpu.roll(x, s, 0), 0.0)
        s *= 2
    return x


def _dn_kernel(*refs, tiles_per_seq):
    tail_ref, s_scr, *stage_scratch = refs[8:]
    step = pl.program_id(0)

    @pl.when(step == 0)
    def _():
        for ref in stage_scratch:
            ref[...] = jnp.zeros_like(ref)

    @pl.when(step % tiles_per_seq == 0)
    def _():
        tail_ref[...] = jnp.zeros_like(tail_ref)

    @pl.when((step % tiles_per_seq == 1) | (step == 0))
    def _():
        s_scr[...] = jnp.zeros_like(s_scr)

    for parity in range(2):
        pl.when(step % 2 == parity)(functools.partial(_dn_body, *refs, rd=1 - parity, wr=parity))


def _dn_body(x_ref, g_ref, w_ref, cw_ref, aexp_ref, dtb_ref, on_ref, o_ref, tail_ref, s_scr,
             kbq_s, k_s, rhs_s, kd_s, qeg_s, gam_s, gamt_s, z_s, *, rd, wr):
    tt, width = o_ref.shape
    n_heads = width // DN_HEAD_DIM
    c = DN_CHUNK

    u = _rms(x_ref[...], g_ref[...]).astype(BF16)
    parts = {}

    def project(*idx):
        for j in idx:
            parts[j] = _dot(u, w_ref[:, j * MXU_DIM:(j + 1) * MXU_DIM])

    assert 2 * c == LANES and n_heads % 4 == 0
    n_ch = tt // c
    n_pair = n_heads // 2
    row2 = lax.broadcasted_iota(jnp.int32, (c, LANES), 0)
    lane2 = lax.broadcasted_iota(jnp.int32, (c, LANES), 1)
    col2 = lane2 & (c - 1)
    incl = col2 <= row2
    strict = col2 < row2
    lo2 = lane2 < c
    lane2x = lax.broadcasted_iota(jnp.int32, (LANES, LANES), 1)
    lo2x = lane2x < c
    row4 = lax.broadcasted_iota(jnp.int32, (c, 2 * LANES), 0)
    lane4 = lax.broadcasted_iota(jnp.int32, (c, 2 * LANES), 1)
    eye4 = ((lane4 & (c - 1)) == row4).astype(F32)
    blk4 = lane4 // c
    zero_k = jnp.zeros((c, DN_HEAD_DIM), BF16)

    def block_diag(m):
        return jnp.concatenate([jnp.where(blk4 == i, m, 0.0) for i in range(4)], axis=0).astype(BF16)

    def l2n(m):
        return m * lax.rsqrt(jnp.sum(m * m, axis=-1, keepdims=True) + EPS)

    probs = [(ch, h) for ch in range(n_ch) for h in range(n_heads)]
    pairs = [(ch, p) for ch in range(n_ch) for p in range(n_pair)]
    quads = [(ch, qd) for ch in range(n_ch) for qd in range(n_pair // 2)]
    pix = {ph: i for i, ph in enumerate(probs)}

    n_prob = len(probs)
    staged = {name: [None] * n_prob for name in ("kbq", "k", "rhs", "kd", "qeg")}
    common = {}
    ba_piece = 4 * n_pair

    def prep_common():
        ba = parts[ba_piece][:, :LANES]
        common["beta"] = jax.nn.sigmoid(ba)
        glog_all = -aexp_ref[...] * _softplus(ba + dtb_ref[...])
        common["gam"] = [_cumsum_rows(glog_all[ch * c:(ch + 1) * c]) for ch in range(n_ch)]
        common["gam_t"] = [jnp.concatenate([g_, g_], axis=0).T for g_ in common["gam"]]

    def prep_pair(p):
        def conv_silu(section):
            blk = parts[section * n_pair + p]
            cs = slice(section * width + p * MXU_DIM, section * width + (p + 1) * MXU_DIM)
            y = _causal_conv(blk, tail_ref[:, cs], cw_ref[:, cs])
            tail_ref[:, cs] = blk[tt - SUBLANES:tt]
            return y * jax.nn.sigmoid(y)

        qs, ks, vs = conv_silu(0), conv_silu(1), conv_silu(2)
        for ch in range(n_ch):
            rs = slice(ch * c, (ch + 1) * c)
            for sub in range(2):
                h = 2 * p + sub
                ls = slice(sub * DN_HEAD_DIM, (sub + 1) * DN_HEAD_DIM)
                q_ = l2n(qs[rs, ls]) * (DN_HEAD_DIM ** -0.5)
                k_ = l2n(ks[rs, ls])
                beta = common["beta"][rs, h:h + 1]
                gc = common["gam"][ch][:, n_heads + h:n_heads + h + 1]
                eg_ = jnp.exp(gc)
                kb_ = k_ * beta
                i = pix[(ch, h)]
                staged["rhs"][i] = jnp.concatenate([vs[rs, ls] * beta, kb_ * eg_], axis=1).astype(BF16)
                staged["kbq"][i] = jnp.concatenate([kb_, q_], axis=0).astype(BF16)
                staged["kd"][i] = k_ * jnp.exp(gc[c - 1:c, :] - gc)
                staged["qeg"][i] = (q_ * eg_).astype(BF16)
                staged["k"][i] = k_.astype(BF16)

    fill = iter([
        lambda: project(ba_piece, 0, n_pair, 2 * n_pair),
        lambda: (prep_common(), project(1, n_pair + 1, 2 * n_pair + 1)),
        lambda: prep_pair(0),
        lambda: project(2, n_pair + 2, 2 * n_pair + 2),
        lambda: prep_pair(1),
        lambda: project(3, n_pair + 3, 2 * n_pair + 3),
        lambda: prep_pair(2),
        lambda: project(3 * n_pair, 3 * n_pair + 1),
        lambda: prep_pair(3),
        lambda: project(3 * n_pair + 2),
        lambda: project(3 * n_pair + 3),
    ])
    assert n_pair == 4

    def interleave():
        action = next(fill, None)
        if action is not None:
            action()

    kbq = [kbq_s[rd, i] for i in range(n_prob)]
    k = [k_s[rd, i] for i in range(n_prob)]
    rhs = [rhs_s[rd, i] for i in range(n_prob)]
    kd = [kd_s[rd, i] for i in range(n_prob)]
    qeg = [qeg_s[rd, i] for i in range(n_prob)]
    gam_all = [gam_s[rd, ch] for ch in range(n_ch)]
    gam_t = [gamt_s[rd, ch] for ch in range(n_ch)]
    z = z_s[rd]
    gcol = [gam_all[ch][:, n_heads + h:n_heads + h + 1] for ch, h in probs]
    glast = [gc[c - 1:c, :] for gc in gcol]

    def both(lst, ch, p):
        return lst[pix[(ch, 2 * p)]], lst[pix[(ch, 2 * p + 1)]]

    interleave()
    kq = []
    for ch, p in pairs:
        (x0, x1), (k0, k1) = both(kbq, ch, p), both(k, ch, p)
        kq.append(_dot_nt(x0, jnp.concatenate([k0.astype(BF16), zero_k], axis=0))
                  + _dot_nt(x1, jnp.concatenate([zero_k, k1.astype(BF16)], axis=0)))
    interleave()
    decay = []
    for ch, p in pairs:
        g0, g1 = both(gcol, ch, p)
        gr = jnp.where(lo2[0:1], gam_t[ch][n_heads + 2 * p:n_heads + 2 * p + 1, :],
                       gam_t[ch][n_heads + 2 * p + 1:n_heads + 2 * p + 2, :])
        diff = jnp.where(lo2, g0, g1) - gr
        decay.append(jnp.where(incl, jnp.exp(jnp.where(incl, diff, 0.0)), 0.0))
    a2 = [jnp.where(strict, kq_[:c] * d_, 0.0) for kq_, d_ in zip(kq, decay)]
    qk2 = [jnp.where(incl, kq_[c:] * d_, 0.0) for kq_, d_ in zip(kq, decay)]

    a4 = [jnp.concatenate([a2[ch * n_pair + 2 * qd], a2[ch * n_pair + 2 * qd + 1]], axis=1) for ch, qd in quads]
    tinv = [eye4 - a_ for a_ in a4]
    pw = [_dot(a_.astype(BF16), block_diag(a_)) for a_ in a4]
    interleave()
    for _ in range(int(math.log2(c)) - 2):
        r = [_dot(jnp.concatenate([t_, p_], axis=0).astype(BF16), block_diag(p_)) for t_, p_ in zip(tinv, pw)]
        tinv = [t_ + r_[:c] for t_, r_ in zip(tinv, r)]
        pw = [r_[c:] for r_ in r]
        interleave()
    tinv = [t_ + _dot(t_.astype(BF16), block_diag(p_)) for t_, p_ in zip(tinv, pw)]
    interleave()

    def split_rows(m):
        return jnp.concatenate([jnp.where(lo2, m, 0.0), jnp.where(lo2, 0.0, m)], axis=0).astype(BF16)

    sol, lhs_o = [], []
    for i, (ch, p) in enumerate(pairs):
        t2 = tinv[i // 2][:, (i % 2) * LANES:(i % 2 + 1) * LANES]
        r0, r1 = both(rhs, ch, p)
        sol.append(_dot(split_rows(t2), jnp.concatenate([r0, r1], axis=0)))
        kd0, kd1 = both(kd, ch, p)
        kdt = jnp.concatenate([kd0, kd1], axis=0).T
        lhs_o.append(jnp.concatenate([split_rows(qk2[i]), jnp.where(lo2x, kdt, 0.0).astype(BF16),
                                      jnp.where(lo2x, 0.0, kdt).astype(BF16)], axis=0))
    wq = []
    for ch, h in probs:
        s_ = sol[ch * n_pair + h // 2][(h % 2) * c:(h % 2 + 1) * c]
        i = pix[(ch, h)]
        wq.append(jnp.concatenate([s_[:, DN_HEAD_DIM:].astype(BF16), qeg[i]], axis=0))
    interleave()

    for ch in range(n_ch):
        rs = slice(ch * c, (ch + 1) * c)
        state = [s_scr[h] for h in range(n_heads)]
        ws = [_dot(wq[pix[(ch, h)]], state[h].astype(BF16)) for h in range(n_heads)]
        interleave()
        un = [sol[ch * n_pair + h // 2][(h % 2) * c:(h % 2 + 1) * c, :DN_HEAD_DIM] - ws[h][:c] for h in range(n_heads)]
        ou = [_dot(lhs_o[ch * n_pair + p], jnp.concatenate([un[2 * p], un[2 * p + 1]], axis=0).astype(BF16))
              for p in range(n_pair)]
        interleave()
        for h in range(n_heads):
            hs = slice(h * DN_HEAD_DIM, (h + 1) * DN_HEAD_DIM)
            o2 = ou[h // 2]
            sub = h % 2
            s_scr[h] = (state[h] * jnp.exp(glast[pix[(ch, h)]])
                        + o2[2 * c + sub * DN_HEAD_DIM:2 * c + (sub + 1) * DN_HEAD_DIM])
            on = _rms(ws[h][c:] + o2[sub * c:(sub + 1) * c], on_ref[:, hs])
            zz = z[rs, hs]
            o_ref[rs, hs] = (on * (zz * jax.nn.sigmoid(zz))).astype(BF16)

    for action in fill:
        action()
    for ref, vals in ((kbq_s, staged["kbq"]), (k_s, staged["k"]), (rhs_s, staged["rhs"]), (kd_s, staged["kd"]),
                      (qeg_s, staged["qeg"]), (gam_s, common["gam"]), (gamt_s, common["gam_t"])):
        for i, v in enumerate(vals):
            ref[wr, i] = v
    z_s[wr] = jnp.concatenate([parts[3 * n_pair + j] for j in range(n_pair)], axis=1)


def _dn_branch(x, g, w_in, layer, w_block, cw, aexp, dtb, onorm, *, tt=256):
    bsz, t, d = x.shape
    width = onorm.shape[1]
    n_heads = width // DN_HEAD_DIM
    w_cols = 4 * width + 2 * LANES
    tiles_per_seq = t // tt
    n_tiles = bsz * tiles_per_seq
    n_prob = (tt // DN_CHUNK) * n_heads
    c = DN_CHUNK

    def x_map(s):
        j = jnp.minimum(s, n_tiles - 1)
        return j // tiles_per_seq, j % tiles_per_seq, 0

    def o_map(s):
        j = jnp.maximum(s - 1, 0)
        return j // tiles_per_seq, j % tiles_per_seq, 0

    assert tiles_per_seq > 1
    return pl.pallas_call(
        functools.partial(_dn_kernel, tiles_per_seq=tiles_per_seq),
        grid=(n_tiles + 1,),
        in_specs=[pl.BlockSpec((None, tt, d), x_map), _const_spec((1, d)),
                  _col_block_spec(d, w_cols, w_block, layer), _const_spec(cw.shape),
                  _const_spec((1, LANES)), _const_spec((1, LANES)), _const_spec((1, width))],
        out_specs=pl.BlockSpec((None, tt, width), o_map),
        out_shape=jax.ShapeDtypeStruct((bsz, t, width), BF16),
        scratch_shapes=[pltpu.VMEM((SUBLANES, 3 * width), F32),
                        pltpu.VMEM((n_heads, DN_HEAD_DIM, DN_HEAD_DIM), F32),
                        pltpu.VMEM((2, n_prob, 2 * c, DN_HEAD_DIM), BF16),
                        pltpu.VMEM((2, n_prob, c, DN_HEAD_DIM), BF16),
                        pltpu.VMEM((2, n_prob, c, 2 * DN_HEAD_DIM), BF16),
                        pltpu.VMEM((2, n_prob, c, DN_HEAD_DIM), F32),
                        pltpu.VMEM((2, n_prob, c, DN_HEAD_DIM), BF16),
                        pltpu.VMEM((2, tt // c, c, LANES), F32),
                        pltpu.VMEM((2, tt // c, LANES, LANES), F32),
                        pltpu.VMEM((2, tt, width), F32)],
        compiler_params=_params(("arbitrary",)),
        name="dn_branch",
    )(x, g, w_in, cw, aexp, dtb, onorm)


def _merge_ffn_kernel(x_ref, gm_ref, ya_ref, yb_ref, yc_ref, wm_ref, wa_ref, wb_ref, wc_ref, wo_ref,
                      gf_ref, wg_ref, wu_ref, wd_ref, *rest, f_chunk):
    n_cast = (len(rest) - 1) // 2
    o_ref = rest[n_cast]
    for src, dst in zip(rest[:n_cast], rest[n_cast + 1:]):
        dst[...] = src[...].astype(BF16)
    x = x_ref[...]
    d = x.shape[1]
    u = _rms(x, gm_ref[...]).astype(BF16)
    y = jnp.zeros_like(x)
    for j, (yr, wr) in enumerate(((ya_ref, wa_ref), (yb_ref, wb_ref), (yc_ref, wc_ref))):
        gate = jax.nn.sigmoid(_dot(u, wm_ref[:, j * d:(j + 1) * d]))
        y = y + gate * _dot(yr[...], wr[...])
    x = x + _dot(y.astype(BF16), wo_ref[...])
    u = _rms(x, gf_ref[...]).astype(BF16)
    acc = jnp.zeros_like(x)
    for c in range(wg_ref.shape[1] // f_chunk):
        sl = slice(c * f_chunk, (c + 1) * f_chunk)
        hg = _dot(u, wg_ref[:, sl])
        hu = _dot(u, wu_ref[:, sl])
        h = (hg * jax.nn.sigmoid(hg) * hu).astype(BF16)
        acc = acc + _dot(h, wd_ref[sl, :])
    o_ref[...] = x + 0.5 * acc


def _merge_ffn(x2, gm, ya, yb, yc, wm, wa, wb, wc, wo, gf, wg, wu, wd, casts=(), *, tm=512, f_chunk=256):
    n, d = x2.shape
    n_steps = n // tm
    row = lambda i: (i, 0)
    cast_in, cast_out, cast_shapes = _cast_specs(casts, n_steps)
    resident = [_const_spec(w.shape, True) for w in (wm, wa, wb, wc, wo)]
    outs = pl.pallas_call(
        functools.partial(_merge_ffn_kernel, f_chunk=f_chunk),
        grid=(n_steps,),
        in_specs=[pl.BlockSpec((tm, d), row), _const_spec((1, d)),
                  pl.BlockSpec((tm, ya.shape[1]), row), pl.BlockSpec((tm, yb.shape[1]), row),
                  pl.BlockSpec((tm, yc.shape[1]), row)] + resident
                 + [_const_spec((1, d))] + [_const_spec(w.shape, True) for w in (wg, wu, wd)] + cast_in,
        out_specs=[pl.BlockSpec((tm, d), row)] + cast_out,
        out_shape=[jax.ShapeDtypeStruct((n, d), F32)] + cast_shapes,
        compiler_params=_params(("arbitrary",)),
        name="merge_ffn",
    )(x2, gm, ya, yb, yc, wm, wa, wb, wc, wo, gf, wg, wu, wd, *[stack for stack, _ in casts])
    return outs[0], list(outs[1:])


def _block_diag_tiles(w):
    nb, k, _ = w.shape
    per = MXU_DIM // k
    w4 = w.reshape(nb // per, per, k, k)
    eye = jnp.eye(per, dtype=w.dtype)
    return jnp.einsum('tpij,pq->tpiqj', w4, eye).reshape(nb // per, MXU_DIM, MXU_DIM)


def _row(v):
    return v.reshape(1, -1).astype(F32)


def _mixer_ffn(x, layer, mix_norm, w_in, rg_conv_w, rg_conv_b, rg_w_r, rg_b_r, rg_w_i, rg_b_i, rg_lambda,
               att_q_norm, att_k_norm, dn_conv_w, dn_a_log, dn_dt_bias, dn_out_norm, w_branch, w_out,
               ffn_norm, ffn_w, casts):
    bsz, t, d = x.shape
    rg_w = rg_conv_w.shape[1]
    att_w = att_q_norm.size
    dn_w = dn_out_norm.size
    dn_h = dn_out_norm.shape[0]
    att_off = 2 * rg_w
    dn_off = att_off + 3 * att_w
    merge_off = dn_off + 4 * dn_w + 2 * dn_h
    dn_cols = 4 * dn_w + 2 * LANES
    assert dn_off % dn_cols == 0 and dn_off + dn_cols <= w_in.shape[2] and 2 * dn_h <= LANES
    w_merge = w_in[layer, :, merge_off:merge_off + 3 * d]
    g = _row(mix_norm)

    lamc = _row(-RG_C * jax.nn.softplus(-rg_lambda.astype(F32)))
    ya = _rg_branch(x, g, w_in, layer, rg_conv_w, _row(rg_conv_b),
                    _block_diag_tiles(rg_w_r).astype(BF16), _row(rg_b_r),
                    _block_diag_tiles(rg_w_i).astype(BF16), _row(rg_b_i), lamc)

    n_att_heads = att_q_norm.shape[0]
    yb = _att_branch(x, g, w_in, layer, att_off, _row(att_q_norm), _row(att_k_norm),
                     _head_mean_matrix(2 * LANES, ATT_HEAD_DIM), _att_maskbias(n_att_heads))

    pad = jnp.zeros((LANES - 2 * dn_h,), F32)
    aexp = _row(jnp.concatenate([jnp.zeros((dn_h,), F32), jnp.exp(dn_a_log.astype(F32)), pad]))
    dtb = _row(jnp.concatenate([jnp.zeros((dn_h,), F32), dn_dt_bias.astype(F32), pad]))
    yc = _dn_branch(x, g, w_in, layer, dn_off // dn_cols, dn_conv_w, aexp, dtb, _row(dn_out_norm))

    wa = w_branch[:rg_w]
    wb = w_branch[rg_w:rg_w + att_w]
    wc = w_branch[rg_w + att_w:]
    n = bsz * t
    out, cast = _merge_ffn(x.reshape(n, d), g, ya.reshape(n, rg_w), yb.reshape(n, att_w), yc.reshape(n, dn_w),
                           w_merge, wa, wb, wc, w_out, _row(ffn_norm), *ffn_w, casts)
    return out.reshape(bsz, t, d), cast


def kernel(x, ffn1_norm, ffn1_w_gate, ffn1_w_up, ffn1_w_down, mix_norm, w_in, rg_conv_w, rg_conv_b, rg_w_r, rg_b_r, rg_w_i, rg_b_i, rg_lambda, att_q_norm, att_k_norm, dn_conv_w, dn_a_log, dn_dt_bias, dn_out_norm, w_branch, w_out, ffn2_norm, ffn2_w_gate, ffn2_w_up, ffn2_w_down):
    bsz, t, d = x.shape
    n = bsz * t
    n_layers = ffn1_norm.shape[0]
    ffn1_w = [w[0].astype(BF16) for w in (ffn1_w_gate, ffn1_w_up, ffn1_w_down)]
    w_in_all = w_in.astype(BF16)
    for l in range(n_layers):
        casts = [(w, l) for w in (w_branch, w_out, ffn2_w_gate, ffn2_w_up, ffn2_w_down)]
        x, (w_branch_b, w_out_b, *ffn2_w) = _ffn(x.reshape(n, d), _row(ffn1_norm[l]), *ffn1_w, casts)
        casts = [(w, l + 1) for w in (ffn1_w_gate, ffn1_w_up, ffn1_w_down)] if l + 1 < n_layers else []
        x, ffn1_w = _mixer_ffn(x.reshape(bsz, t, d), l, mix_norm[l], w_in_all, rg_conv_w[l], rg_conv_b[l], rg_w_r[l],
                               rg_b_r[l], rg_w_i[l], rg_b_i[l], rg_lambda[l], att_q_norm[l], att_k_norm[l],
                               dn_conv_w[l], dn_a_log[l], dn_dt_bias[l], dn_out_norm[l], w_branch_b, w_out_b,
                               ffn2_norm[l], ffn2_w, casts)
    return x
```

```python
import functools
import math

import jax
import jax.numpy as jnp
from jax import lax
from jax.experimental import pallas as pl
from jax.experimental.pallas import tpu as pltpu

F32 = jnp.float32
BF16 = jnp.bfloat16

EPS = 1e-6
NEG_INF = -1e30
RG_C = 8.0
RG_BLOCK_DIM = 64
CONV_WIDTH = 4
ATT_GROUPS = ((128, 1), (512, 4), (2048, 16))
ATT_HEADS_PER_GROUP = 4
ATT_HEAD_DIM = 64
ATT_SPAN = 128
DN_HEAD_DIM = 128
DN_CHUNK = 64

LANES = 128
SUBLANES = 8
MXU_DIM = 256
VMEM_LIMIT_BYTES = 60 * 1024 * 1024


def _rms(x, g):
    return x * lax.rsqrt(jnp.mean(x * x, axis=-1, keepdims=True) + EPS) * g


def _dot(a, b):
    return jnp.dot(a, b, preferred_element_type=F32)


def _dot_nt(a, b):
    return lax.dot_general(a, b, (((1,), (1,)), ((), ())), preferred_element_type=F32)


def _dot_tn(a, b):
    return lax.dot_general(a, b, (((0,), (0,)), ((), ())), preferred_element_type=F32)


def _softplus(z):
    return jnp.maximum(z, 0.0) + jnp.log(1.0 + jnp.exp(-jnp.abs(z)))


def _const_spec(shape, single_buffer=False):
    nd = len(shape)
    mode = pl.Buffered(1) if single_buffer else None
    return pl.BlockSpec(shape, lambda *_: (0,) * nd, pipeline_mode=mode)


def _col_block_spec(rows, cols, col_block, layer, single_buffer=False):
    mode = pl.Buffered(1) if single_buffer else None
    return pl.BlockSpec((None, rows, cols), lambda *_: (layer, 0, col_block), pipeline_mode=mode)


def _params(semantics):
    return pltpu.CompilerParams(dimension_semantics=semantics, vmem_limit_bytes=VMEM_LIMIT_BYTES)


BF16_SUBLANES = 16


def _cast_blocking(rows, n_steps):
    rpb = BF16_SUBLANES
    while rows % rpb or rows // rpb > n_steps:
        rpb += BF16_SUBLANES
        assert rpb <= rows
    return rpb, rows // rpb


def _cast_specs(casts, n_steps):
    cast_in, cast_out, cast_shapes = [], [], []
    for stack, layer in casts:
        _, rows, cols = stack.shape
        rpb, nb = _cast_blocking(rows, n_steps)
        cast_in.append(pl.BlockSpec((None, rpb, cols), lambda i, layer=layer, nb=nb: (layer, jnp.minimum(i, nb - 1), 0)))
        cast_out.append(pl.BlockSpec((rpb, cols), lambda i, nb=nb: (jnp.minimum(i, nb - 1), 0)))
        cast_shapes.append(jax.ShapeDtypeStruct((rows, cols), BF16))
    return cast_in, cast_out, cast_shapes


def _ffn_kernel(x_ref, g_ref, wg_ref, wu_ref, wd_ref, *rest, f_chunk):
    n_cast = (len(rest) - 1) // 2
    o_ref = rest[n_cast]
    for src, dst in zip(rest[:n_cast], rest[n_cast + 1:]):
        dst[...] = src[...].astype(BF16)
    x = x_ref[...]
    u = _rms(x, g_ref[...]).astype(BF16)
    acc = jnp.zeros_like(x)
    for c in range(wg_ref.shape[1] // f_chunk):
        sl = slice(c * f_chunk, (c + 1) * f_chunk)
        hg = _dot(u, wg_ref[:, sl])
        hu = _dot(u, wu_ref[:, sl])
        h = (hg * jax.nn.sigmoid(hg) * hu).astype(BF16)
        acc = acc + _dot(h, wd_ref[sl, :])
    o_ref[...] = x + 0.5 * acc


def _ffn(x2, g, wg, wu, wd, casts=(), *, tm=1024, f_chunk=256):
    n, d = x2.shape
    f = wg.shape[1]
    n_steps = n // tm
    cast_in, cast_out, cast_shapes = _cast_specs(casts, n_steps)
    outs = pl.pallas_call(
        functools.partial(_ffn_kernel, f_chunk=f_chunk),
        grid=(n_steps,),
        in_specs=[pl.BlockSpec((tm, d), lambda i: (i, 0)), _const_spec((1, d)),
                  _const_spec((d, f), True), _const_spec((d, f), True), _const_spec((f, d), True)] + cast_in,
        out_specs=[pl.BlockSpec((tm, d), lambda i: (i, 0))] + cast_out,
        out_shape=[jax.ShapeDtypeStruct((n, d), F32)] + cast_shapes,
        compiler_params=_params(("arbitrary",)),
        name="ffn",
    )(x2, g, wg, wu, wd, *[stack for stack, _ in casts])
    return outs[0], list(outs[1:])


def _shift_rows(cur, prev8, s):
    rows = lax.broadcasted_iota(jnp.int32, cur.shape, 0)
    return jnp.where(rows < s, pltpu.roll(prev8, s, 0), pltpu.roll(cur, s, 0))


def _causal_conv(p, tail8, w):
    kw = w.shape[0]
    acc = p * w[kw - 1:kw, :]
    top = p[0:SUBLANES] * w[kw - 1:kw, :]
    for s in range(1, kw):
        wk = w[kw - 1 - s:kw - s, :]
        acc = acc + pltpu.roll(p, s, 0) * wk
        top = top + _shift_rows(p[0:SUBLANES], tail8, s) * wk
    return jnp.concatenate([top, acc[SUBLANES:]], axis=0)


def _segment_perm(tt):
    r = jnp.arange(tt)
    src = (r % SUBLANES) * (tt // SUBLANES) + r // SUBLANES
    return (src[:, None] == jnp.arange(tt)[None, :]).astype(BF16)


def _rg_kernel(x_ref, g_ref, perm_ref, permt_ref, wx_ref, wgate_ref, cw_ref, cb_ref, wr_ref, br_ref, wi_ref,
               bi_ref, lamc_ref, o_ref, tail_ref, carry_ref):
    rows, width = o_ref.shape
    tt = perm_ref.shape[0]
    n_sub = rows // tt
    n_grp = tt // SUBLANES
    kw = cw_ref.shape[0]

    @pl.when(pl.program_id(1) == 0)
    def _():
        tail_ref[...] = jnp.zeros_like(tail_ref)
        carry_ref[...] = jnp.zeros_like(carry_ref)

    u = _rms(x_ref[...], g_ref[...]).astype(BF16)
    nc = width // MXU_DIM
    rows8 = lax.broadcasted_iota(jnp.int32, (SUBLANES, MXU_DIM), 0)
    up = {}
    tails = {c: tail_ref[:, c * MXU_DIM:(c + 1) * MXU_DIM] for c in range(nc)}
    carries = {c: carry_ref[:, c * MXU_DIM:(c + 1) * MXU_DIM] for c in range(nc)}

    def project(t, c):
        if t not in up:
            up[t] = _dot(perm_ref[...], u[t * tt:(t + 1) * tt]).astype(BF16)
        cs = slice(c * MXU_DIM, (c + 1) * MXU_DIM)
        return _dot(up[t], wx_ref[:, cs]), _dot(up[t], wgate_ref[:, cs])

    def conv(c, px):
        cs = slice(c * MXU_DIM, (c + 1) * MXU_DIM)
        tail = tails[c]
        head = [_shift_rows(px[(n_grp - k) * SUBLANES:(n_grp - k + 1) * SUBLANES],
                            tail[(kw - 1 - k) * SUBLANES:(kw - k) * SUBLANES], 1) for k in range(kw - 1, 0, -1)]
        ext = jnp.concatenate(head + [px], axis=0)
        tails[c] = px[tt - (kw - 1) * SUBLANES:tt]
        xa = cb_ref[:, cs] + ext[0:tt] * cw_ref[0:1, cs]
        for k in range(1, kw):
            xa = xa + ext[k * SUBLANES:k * SUBLANES + tt] * cw_ref[k:k + 1, cs]
        return xa

    def gates(c, xa):
        xab = xa.astype(BF16)
        return _dot(xab, wr_ref[c]), _dot(xab, wi_ref[c])

    def scan(c, xa, r_lin, i_lin, pg):
        cs = slice(c * MXU_DIM, (c + 1) * MXU_DIM)
        a = jnp.exp(lamc_ref[:, cs] * jax.nn.sigmoid(r_lin + br_ref[:, cs]))
        b = jnp.sqrt(1.0 - a * a) * (jax.nn.sigmoid(i_lin + bi_ref[:, cs]) * xa)
        hl = jnp.zeros((SUBLANES, MXU_DIM), F32)
        ap = jnp.ones((SUBLANES, MXU_DIM), F32)
        hs, aps = [], []
        for j in range(n_grp):
            rs = slice(j * SUBLANES, (j + 1) * SUBLANES)
            hl = a[rs] * hl + b[rs]
            ap = a[rs] * ap
            hs.append(hl)
            aps.append(ap)
        for s in (1, 2, 4):
            m = rows8 >= s
            hl = jnp.where(m, ap * pltpu.roll(hl, s, 0) + hl, hl)
            ap = jnp.where(m, ap * pltpu.roll(ap, s, 0), ap)
        cin = carries[c]
        seg_end = hl + ap * cin
        carries[c] = jnp.broadcast_to(seg_end[SUBLANES - 1:SUBLANES, :], seg_end.shape)
        seg_in = jnp.where(rows8 < 1, cin, pltpu.roll(seg_end, 1, 0))
        h = jnp.concatenate([h_ + a_ * seg_in for h_, a_ in zip(hs, aps)], axis=0)
        return (h * jax.nn.gelu(pg)).astype(BF16)

    def emit(t, c, y):
        o_ref[t * tt:(t + 1) * tt, c * MXU_DIM:(c + 1) * MXU_DIM] = _dot(permt_ref[...], y).astype(BF16)

    items = [(t, c) for t in range(n_sub) for c in range(nc)]
    proj, xas, gts, ys = {}, {}, {}, {}
    for step in range(len(items) + 3):
        if step < len(items):
            proj[step] = project(*items[step])
        k = step - 1
        if 0 <= k < len(items):
            xas[k] = conv(items[k][1], proj[k][0])
            gts[k] = gates(items[k][1], xas[k])
        k = step - 2
        if 0 <= k < len(items):
            ys[k] = scan(items[k][1], xas[k], gts[k][0], gts[k][1], proj[k][1])
        k = step - 3
        if 0 <= k < len(items):
            emit(*items[k], ys[k])
    for c in range(nc):
        tail_ref[:, c * MXU_DIM:(c + 1) * MXU_DIM] = tails[c]
        carry_ref[:, c * MXU_DIM:(c + 1) * MXU_DIM] = carries[c]


def _rg_branch(x, g, w_in, layer, cw, cb, wr_t, br, wi_t, bi, lamc, *, tt=256, n_sub=4):
    bsz, t, d = x.shape
    width = cw.shape[1]
    perm = _segment_perm(tt)
    rows = tt * n_sub
    return pl.pallas_call(
        _rg_kernel,
        grid=(bsz, t // rows),
        in_specs=[pl.BlockSpec((None, rows, d), lambda b, i: (b, i, 0)), _const_spec((1, d)),
                  _const_spec((tt, tt)), _const_spec((tt, tt)),
                  _col_block_spec(d, width, 0, layer), _col_block_spec(d, width, 1, layer),
                  _const_spec(cw.shape), _const_spec((1, width)),
                  _const_spec(wr_t.shape), _const_spec((1, width)),
                  _const_spec(wi_t.shape), _const_spec((1, width)), _const_spec((1, width))],
        out_specs=pl.BlockSpec((None, rows, width), lambda b, i: (b, i, 0)),
        out_shape=jax.ShapeDtypeStruct((bsz, t, width), BF16),
        scratch_shapes=[pltpu.VMEM(((cw.shape[0] - 1) * SUBLANES, width), F32), pltpu.VMEM((SUBLANES, width), F32)],
        compiler_params=_params(("arbitrary", "arbitrary")),
        name="rg_branch",
    )(x, g, perm, perm.T, w_in, w_in, cw, cb, wr_t, br, wi_t, bi, lamc)


ATT_BLOCK_BATCH = 4


def _att_kernel(x_ref, g_ref, *rest, rt, n_w):
    w_refs = rest[:n_w]
    gq_ref, gk_ref, e4_ref, mb_ref, o_ref, qkv_scr, o_scr, lse_scr = rest[n_w:]
    t_total = o_ref.shape[0]
    n_slab = o_scr.shape[0]
    i = pl.program_id(1)
    row0 = pl.multiple_of(i * rt, rt)
    span = ATT_SPAN

    u = _rms(x_ref[...], g_ref[...]).astype(BF16)
    qkv = jnp.concatenate([_dot(u, w_ref[...]) for w_ref in w_refs], axis=1)
    width = n_slab * LANES
    e4 = e4_ref[...]
    for s in range(n_slab):
        cs = slice(s * LANES, (s + 1) * LANES)
        qk = jnp.concatenate([qkv[:, cs], qkv[:, width + s * LANES: width + (s + 1) * LANES]], axis=1)
        gain = jnp.concatenate([gq_ref[:, cs] * (ATT_HEAD_DIM ** -0.5), gk_ref[:, cs]], axis=1)
        ms = _dot((qk * qk).astype(BF16), e4)
        qk = qk * lax.rsqrt(ms + EPS) * gain
        qkv_scr[s, pl.ds(row0, rt), :] = qk[:, :LANES]
        qkv_scr[n_slab + s, pl.ds(row0, rt), :] = qk[:, LANES:]
        qkv_scr[2 * n_slab + s, pl.ds(row0, rt), :] = qkv[:, 2 * width + s * LANES: 2 * width + (s + 1) * LANES]

    @pl.when(i == pl.num_programs(1) - 1)
    def _():
        lane = lax.broadcasted_iota(jnp.int32, (span, LANES), 1)
        first_head = lane < ATT_HEAD_DIM

        for g, (window, dil) in enumerate(ATT_GROUPS):
            assert window // dil == span
            nblk = t_total // dil // span
            blocks = [(r, n) for n in range(nblk) for r in range(dil)]

            def rows(start, dil=dil):
                return pl.ds(start, span) if dil == 1 else pl.ds(start, span, stride=dil)

            for b0 in range(0, len(blocks), ATT_BLOCK_BATCH):
                probs = [(r + n * span * dil, n > 0, 2 * g + sl)
                         for r, n in blocks[b0:b0 + ATT_BLOCK_BATCH] for sl in range(2)]
                qs, kk, vv = [], [], []
                for start, has_prev, slab in probs:
                    q2 = qkv_scr[slab, rows(start), :]
                    qs.append(jnp.concatenate([jnp.where(first_head, q2, 0.0), jnp.where(first_head, 0.0, q2)],
                                              axis=0).astype(BF16))
                    kc = qkv_scr[n_slab + slab, rows(start), :]
                    vc = qkv_scr[2 * n_slab + slab, rows(start), :]
                    if has_prev:
                        prev = start - span * dil
                        kc = jnp.concatenate([qkv_scr[n_slab + slab, rows(prev), :], kc], axis=0)
                        vc = jnp.concatenate([qkv_scr[2 * n_slab + slab, rows(prev), :], vc], axis=0)
                    kk.append(kc.astype(BF16))
                    vv.append(vc.astype(BF16))
                sc = [_dot_nt(q_, k_) for q_, k_ in zip(qs, kk)]
                sc = [s_ + (mb_ref[slab] if has_prev else mb_ref[slab, :, span:])
                      for s_, (_, has_prev, slab) in zip(sc, probs)]
                mx = [jnp.max(s_, axis=-1, keepdims=True) for s_ in sc]
                pr = [jnp.exp(s_ - m_) for s_, m_ in zip(sc, mx)]
                den = [jnp.sum(p_, axis=-1, keepdims=True) for p_ in pr]
                pv = [_dot(p_.astype(BF16), v_) for p_, v_ in zip(pr, vv)]
                for (start, _, slab), pv_, m_, d_ in zip(probs, pv, mx, den):
                    on = pv_ * (1.0 / d_)
                    lse = jnp.broadcast_to(m_ + jnp.log(d_), on.shape)
                    o_scr[slab, rows(start), :] = jnp.where(first_head, on[:span], on[span:])
                    lse_scr[slab, rows(start), :] = jnp.where(first_head, lse[:span], lse[span:])

        n_grp = len(ATT_GROUPS)
        ct = 256

        def combine(j, carry):
            rws = pl.ds(pl.multiple_of(j * ct, ct), ct)
            for sl in range(2):
                ls = [lse_scr[2 * g + sl, rws, :] for g in range(n_grp)]
                mx = functools.reduce(jnp.maximum, ls)
                es = [jnp.exp(l - mx) for l in ls]
                inv = 1.0 / functools.reduce(lambda a, b: a + b, es)
                for g in range(n_grp):
                    slab = 2 * g + sl
                    o_ref[rws, slab * LANES:(slab + 1) * LANES] = (o_scr[slab, rws, :] * (es[g] * inv)).astype(BF16)
            return carry

        lax.fori_loop(0, t_total // ct, combine, 0)


def _att_branch(x, g, w_in, layer, w_off, gq, gk, e4, maskbias, *, rt=512):
    bsz, t, d = x.shape
    width = gq.shape[1]
    n_slab = width // LANES
    assert w_off % MXU_DIM == 0 and (3 * width) % MXU_DIM == 0
    n_w = 3 * width // MXU_DIM
    return pl.pallas_call(
        functools.partial(_att_kernel, rt=rt, n_w=n_w),
        grid=(bsz, t // rt),
        in_specs=[pl.BlockSpec((None, rt, d), lambda b, i: (b, i, 0)), _const_spec((1, d))]
                 + [_col_block_spec(d, MXU_DIM, w_off // MXU_DIM + j, layer, True) for j in range(n_w)]
                 + [_const_spec((1, width)), _const_spec((1, width)),
                    _const_spec(e4.shape), _const_spec(maskbias.shape, True)],
        out_specs=pl.BlockSpec((None, t, width), lambda b, i: (b, 0, 0)),
        out_shape=jax.ShapeDtypeStruct((bsz, t, width), BF16),
        scratch_shapes=[pltpu.VMEM((3 * n_slab, t, LANES), F32), pltpu.VMEM((n_slab, t, LANES), F32),
                        pltpu.VMEM((n_slab, t, LANES), F32)],
        compiler_params=_params(("arbitrary", "arbitrary")),
        name="att_branch",
    )(x, g, *([w_in] * n_w), gq, gk, e4, maskbias)


def _att_maskbias(n_heads):
    slopes = jnp.exp2(-8.0 * jnp.arange(1, n_heads + 1, dtype=F32) / n_heads)
    qi = jnp.arange(ATT_SPAN)[:, None]
    kj = jnp.arange(2 * ATT_SPAN)[None, :]
    delta = qi + ATT_SPAN - kj
    valid = (delta >= 0) & (delta <= ATT_SPAN)
    dil = jnp.repeat(jnp.array([d for _, d in ATT_GROUPS], F32), ATT_HEADS_PER_GROUP)
    bias = -(slopes * dil)[:, None, None] * delta.astype(F32)[None]
    return jnp.where(valid[None], bias, NEG_INF).reshape(n_heads // 2, 2 * ATT_SPAN, 2 * ATT_SPAN)


def _head_mean_matrix(n_lanes, head_dim):
    lane_head = jnp.arange(n_lanes) // head_dim
    return ((lane_head[:, None] == lane_head[None, :]).astype(F32) / head_dim).astype(BF16)


def _cumsum_rows(x):
    n = x.shape[0]
    rows = lax.broadcasted_iota(jnp.int32, x.shape, 0)
    s = 1
    while s < n:
        x = x + jnp.where(rows >= s, pltpu.roll(x, s, 0), 0.0)
        s *= 2
    return x


def _dn_kernel(*refs, tiles_per_seq):
    tail_ref, s_scr, *stage_scratch = refs[8:]
    step = pl.program_id(0)

    @pl.when(step == 0)
    def _():
        for ref in stage_scratch:
            ref[...] = jnp.zeros_like(ref)

    @pl.when(step % tiles_per_seq == 0)
    def _():
        tail_ref[...] = jnp.zeros_like(tail_ref)

    @pl.when((step % tiles_per_seq == 1) | (step == 0))
    def _():
        s_scr[...] = jnp.zeros_like(s_scr)

    for parity in range(2):
        pl.when(step % 2 == parity)(functools.partial(_dn_body, *refs, rd=1 - parity, wr=parity))


def _dn_body(x_ref, g_ref, w_ref, cw_ref, aexp_ref, dtb_ref, on_ref, o_ref, tail_ref, s_scr,
             kbq_s, k_s, rhs_s, kd_s, qeg_s, gam_s, gamt_s, z_s, *, rd, wr):
    tt, width = o_ref.shape
    n_heads = width // DN_HEAD_DIM
    c = DN_CHUNK

    u = _rms(x_ref[...], g_ref[...]).astype(BF16)
    parts = {}

    def project(*idx):
        for j in idx:
            parts[j] = _dot(u, w_ref[:, j * MXU_DIM:(j + 1) * MXU_DIM])

    assert 2 * c == LANES and n_heads % 4 == 0
    n_ch = tt // c
    n_pair = n_heads // 2
    row2 = lax.broadcasted_iota(jnp.int32, (c, LANES), 0)
    lane2 = lax.broadcasted_iota(jnp.int32, (c, LANES), 1)
    col2 = lane2 & (c - 1)
    incl = col2 <= row2
    strict = col2 < row2
    lo2 = lane2 < c
    lane2x = lax.broadcasted_iota(jnp.int32, (LANES, LANES), 1)
    lo2x = lane2x < c
    row4 = lax.broadcasted_iota(jnp.int32, (c, 2 * LANES), 0)
    lane4 = lax.broadcasted_iota(jnp.int32, (c, 2 * LANES), 1)
    eye4 = ((lane4 & (c - 1)) == row4).astype(F32)
    blk4 = lane4 // c

    def block_diag(m):
        return jnp.concatenate([jnp.where(blk4 == i, m, 0.0) for i in range(4)], axis=0).astype(BF16)

    def l2n(m):
        return m * lax.rsqrt(jnp.sum(m * m, axis=-1, keepdims=True) + EPS)

    probs = [(ch, h) for ch in range(n_ch) for h in range(n_heads)]
    pairs = [(ch, p) for ch in range(n_ch) for p in range(n_pair)]
    quads = [(ch, qd) for ch in range(n_ch) for qd in range(n_pair // 2)]
    pix = {ph: i for i, ph in enumerate(probs)}

    n_prob = len(probs)
    staged = {name: [None] * n_prob for name in ("kbq", "k", "rhs", "kd", "qeg")}
    common = {}
    ba_piece = 4 * n_pair

    def prep_common():
        ba = parts[ba_piece][:, :LANES]
        common["beta"] = jax.nn.sigmoid(ba)
        glog_all = -aexp_ref[...] * _softplus(ba + dtb_ref[...])
        common["gam"] = [_cumsum_rows(glog_all[ch * c:(ch + 1) * c]) for ch in range(n_ch)]
        common["gam_t"] = [jnp.concatenate([g_, g_], axis=0).T for g_ in common["gam"]]

    def prep_pair(p):
        def conv_silu(section):
            blk = parts[section * n_pair + p]
            cs = slice(section * width + p * MXU_DIM, section * width + (p + 1) * MXU_DIM)
            y = _causal_conv(blk, tail_ref[:, cs], cw_ref[:, cs])
            tail_ref[:, cs] = blk[tt - SUBLANES:tt]
            return y * jax.nn.sigmoid(y)

        qs, ks, vs = conv_silu(0), conv_silu(1), conv_silu(2)
        for ch in range(n_ch):
            rs = slice(ch * c, (ch + 1) * c)
            for sub in range(2):
                h = 2 * p + sub
                ls = slice(sub * DN_HEAD_DIM, (sub + 1) * DN_HEAD_DIM)
                q_ = l2n(qs[rs, ls]) * (DN_HEAD_DIM ** -0.5)
                k_ = l2n(ks[rs, ls])
                beta = common["beta"][rs, h:h + 1]
                gc = common["gam"][ch][:, n_heads + h:n_heads + h + 1]
                eg_ = jnp.exp(gc)
                kb_ = k_ * beta
                i = pix[(ch, h)]
                staged["rhs"][i] = jnp.concatenate([vs[rs, ls] * beta, kb_ * eg_], axis=1).astype(BF16)
                staged["kbq"][i] = jnp.concatenate([kb_, q_], axis=0).astype(BF16)
                staged["kd"][i] = k_ * jnp.exp(gc[c - 1:c, :] - gc)
                staged["qeg"][i] = (q_ * eg_).astype(BF16)
                staged["k"][i] = k_.astype(BF16)

    fill = iter([
        lambda: project(ba_piece, 0, n_pair, 2 * n_pair),
        lambda: (prep_common(), project(1, n_pair + 1, 2 * n_pair + 1)),
        lambda: prep_pair(0),
        lambda: project(2, n_pair + 2, 2 * n_pair + 2),
        lambda: prep_pair(1),
        lambda: project(3, n_pair + 3, 2 * n_pair + 3),
        lambda: prep_pair(2),
        lambda: project(3 * n_pair, 3 * n_pair + 1),
        lambda: prep_pair(3),
        lambda: project(3 * n_pair + 2),
        lambda: project(3 * n_pair + 3),
    ])
    assert n_pair == 4

    def interleave():
        action = next(fill, None)
        if action is not None:
            action()

    kbq = [kbq_s[rd, i] for i in range(n_prob)]
    k = [k_s[rd, i] for i in range(n_prob)]
    rhs = [rhs_s[rd, i] for i in range(n_prob)]
    kd = [kd_s[rd, i] for i in range(n_prob)]
    qeg = [qeg_s[rd, i] for i in range(n_prob)]
    gam_all = [gam_s[rd, ch] for ch in range(n_ch)]
    gam_t = [gamt_s[rd, ch] for ch in range(n_ch)]
    z = z_s[rd]
    gcol = [gam_all[ch][:, n_heads + h:n_heads + h + 1] for ch, h in probs]
    glast = [gc[c - 1:c, :] for gc in gcol]

    def both(lst, ch, p):
        return lst[pix[(ch, 2 * p)]], lst[pix[(ch, 2 * p + 1)]]

    interleave()
    kq = []
    for ch, p in pairs:
        (x0, x1), (k0, k1) = both(kbq, ch, p), both(k, ch, p)
        full = _dot_nt(jnp.concatenate([x0, x1], axis=0), jnp.concatenate([k0, k1], axis=0).astype(BF16))
        kq.append(jnp.where(lo2x, full[:2 * c], full[2 * c:]))
    interleave()
    decay = []
    for ch, p in pairs:
        g0, g1 = both(gcol, ch, p)
        gr = jnp.where(lo2[0:1], gam_t[ch][n_heads + 2 * p:n_heads + 2 * p + 1, :],
                       gam_t[ch][n_heads + 2 * p + 1:n_heads + 2 * p + 2, :])
        diff = jnp.where(lo2, g0, g1) - gr
        decay.append(jnp.where(incl, jnp.exp(jnp.where(incl, diff, 0.0)), 0.0))
    a2 = [jnp.where(strict, kq_[:c] * d_, 0.0) for kq_, d_ in zip(kq, decay)]
    qk2 = [jnp.where(incl, kq_[c:] * d_, 0.0) for kq_, d_ in zip(kq, decay)]

    a4 = [jnp.concatenate([a2[ch * n_pair + 2 * qd], a2[ch * n_pair + 2 * qd + 1]], axis=1) for ch, qd in quads]
    tinv = [eye4 - a_ for a_ in a4]
    pw = [_dot(a_.astype(BF16), block_diag(a_)) for a_ in a4]
    interleave()
    for _ in range(int(math.log2(c)) - 2):
        r = [_dot(jnp.concatenate([t_, p_], axis=0).astype(BF16), block_diag(p_)) for t_, p_ in zip(tinv, pw)]
        tinv = [t_ + r_[:c] for t_, r_ in zip(tinv, r)]
        pw = [r_[c:] for r_ in r]
        interleave()
    tinv = [t_ + _dot(t_.astype(BF16), block_diag(p_)) for t_, p_ in zip(tinv, pw)]
    interleave()

    def split_rows(m):
        return jnp.concatenate([jnp.where(lo2, m, 0.0), jnp.where(lo2, 0.0, m)], axis=0).astype(BF16)

    sol, lhs_o = [], []
    for i, (ch, p) in enumerate(pairs):
        t2 = tinv[i // 2][:, (i % 2) * LANES:(i % 2 + 1) * LANES]
        r0, r1 = both(rhs, ch, p)
        sol.append(_dot(split_rows(t2), jnp.concatenate([r0, r1], axis=0)))
        kd0, kd1 = both(kd, ch, p)
        kdt = jnp.concatenate([kd0, kd1], axis=0).T
        lhs_o.append(jnp.concatenate([split_rows(qk2[i]), jnp.where(lo2x, kdt, 0.0).astype(BF16),
                                      jnp.where(lo2x, 0.0, kdt).astype(BF16)], axis=0))
    wq = []
    for ch, h in probs:
        s_ = sol[ch * n_pair + h // 2][(h % 2) * c:(h % 2 + 1) * c]
        i = pix[(ch, h)]
        wq.append(jnp.concatenate([s_[:, DN_HEAD_DIM:].astype(BF16), qeg[i]], axis=0))
    interleave()

    for ch in range(n_ch):
        rs = slice(ch * c, (ch + 1) * c)
        state = [s_scr[h] for h in range(n_heads)]
        ws = [_dot(wq[pix[(ch, h)]], state[h].astype(BF16)) for h in range(n_heads)]
        interleave()
        un = [sol[ch * n_pair + h // 2][(h % 2) * c:(h % 2 + 1) * c, :DN_HEAD_DIM] - ws[h][:c] for h in range(n_heads)]
        ou = [_dot(lhs_o[ch * n_pair + p], jnp.concatenate([un[2 * p], un[2 * p + 1]], axis=0).astype(BF16))
              for p in range(n_pair)]
        interleave()
        for h in range(n_heads):
            hs = slice(h * DN_HEAD_DIM, (h + 1) * DN_HEAD_DIM)
            o2 = ou[h // 2]
            sub = h % 2
            s_scr[h] = (state[h] * jnp.exp(glast[pix[(ch, h)]])
                        + o2[2 * c + sub * DN_HEAD_DIM:2 * c + (sub + 1) * DN_HEAD_DIM])
            on = _rms(ws[h][c:] + o2[sub * c:(sub + 1) * c], on_ref[:, hs])
            zz = z[rs, hs]
            o_ref[rs, hs] = (on * (zz * jax.nn.sigmoid(zz))).astype(BF16)

    for action in fill:
        action()
    for ref, vals in ((kbq_s, staged["kbq"]), (k_s, staged["k"]), (rhs_s, staged["rhs"]), (kd_s, staged["kd"]),
                      (qeg_s, staged["qeg"]), (gam_s, common["gam"]), (gamt_s, common["gam_t"])):
        for i, v in enumerate(vals):
            ref[wr, i] = v
    z_s[wr] = jnp.concatenate([parts[3 * n_pair + j] for j in range(n_pair)], axis=1)


def _dn_branch(x, g, w_in, layer, w_block, cw, aexp, dtb, onorm, *, tt=256):
    bsz, t, d = x.shape
    width = onorm.shape[1]
    n_heads = width // DN_HEAD_DIM
    w_cols = 4 * width + 2 * LANES
    tiles_per_seq = t // tt
    n_tiles = bsz * tiles_per_seq
    n_prob = (tt // DN_CHUNK) * n_heads
    c = DN_CHUNK

    def x_map(s):
        j = jnp.minimum(s, n_tiles - 1)
        return j // tiles_per_seq, j % tiles_per_seq, 0

    def o_map(s):
        j = jnp.maximum(s - 1, 0)
        return j // tiles_per_seq, j % tiles_per_seq, 0

    assert tiles_per_seq > 1
    return pl.pallas_call(
        functools.partial(_dn_kernel, tiles_per_seq=tiles_per_seq),
        grid=(n_tiles + 1,),
        in_specs=[pl.BlockSpec((None, tt, d), x_map), _const_spec((1, d)),
                  _col_block_spec(d, w_cols, w_block, layer), _const_spec(cw.shape),
                  _const_spec((1, LANES)), _const_spec((1, LANES)), _const_spec((1, width))],
        out_specs=pl.BlockSpec((None, tt, width), o_map),
        out_shape=jax.ShapeDtypeStruct((bsz, t, width), BF16),
        scratch_shapes=[pltpu.VMEM((SUBLANES, 3 * width), F32),
                        pltpu.VMEM((n_heads, DN_HEAD_DIM, DN_HEAD_DIM), F32),
                        pltpu.VMEM((2, n_prob, 2 * c, DN_HEAD_DIM), BF16),
                        pltpu.VMEM((2, n_prob, c, DN_HEAD_DIM), BF16),
                        pltpu.VMEM((2, n_prob, c, 2 * DN_HEAD_DIM), BF16),
                        pltpu.VMEM((2, n_prob, c, DN_HEAD_DIM), F32),
                        pltpu.VMEM((2, n_prob, c, DN_HEAD_DIM), BF16),
                        pltpu.VMEM((2, tt // c, c, LANES), F32),
                        pltpu.VMEM((2, tt // c, LANES, LANES), F32),
                        pltpu.VMEM((2, tt, width), F32)],
        compiler_params=_params(("arbitrary",)),
        name="dn_branch",
    )(x, g, w_in, cw, aexp, dtb, onorm)


def _merge_ffn_kernel(x_ref, gm_ref, ya_ref, yb_ref, yc_ref, wm_ref, wa_ref, wb_ref, wc_ref, wo_ref,
                      gf_ref, wg_ref, wu_ref, wd_ref, *rest, f_chunk):
    n_cast = (len(rest) - 1) // 2
    o_ref = rest[n_cast]
    for src, dst in zip(rest[:n_cast], rest[n_cast + 1:]):
        dst[...] = src[...].astype(BF16)
    x = x_ref[...]
    d = x.shape[1]
    u = _rms(x, gm_ref[...]).astype(BF16)
    y = jnp.zeros_like(x)
    for j, (yr, wr) in enumerate(((ya_ref, wa_ref), (yb_ref, wb_ref), (yc_ref, wc_ref))):
        gate = jax.nn.sigmoid(_dot(u, wm_ref[:, j * d:(j + 1) * d]))
        y = y + gate * _dot(yr[...], wr[...])
    x = x + _dot(y.astype(BF16), wo_ref[...])
    u = _rms(x, gf_ref[...]).astype(BF16)
    acc = jnp.zeros_like(x)
    for c in range(wg_ref.shape[1] // f_chunk):
        sl = slice(c * f_chunk, (c + 1) * f_chunk)
        hg = _dot(u, wg_ref[:, sl])
        hu = _dot(u, wu_ref[:, sl])
        h = (hg * jax.nn.sigmoid(hg) * hu).astype(BF16)
        acc = acc + _dot(h, wd_ref[sl, :])
    o_ref[...] = x + 0.5 * acc


def _merge_ffn(x2, gm, ya, yb, yc, wm, wa, wb, wc, wo, gf, wg, wu, wd, casts=(), *, tm=512, f_chunk=256):
    n, d = x2.shape
    n_steps = n // tm
    row = lambda i: (i, 0)
    cast_in, cast_out, cast_shapes = _cast_specs(casts, n_steps)
    resident = [_const_spec(w.shape, True) for w in (wm, wa, wb, wc, wo)]
    outs = pl.pallas_call(
        functools.partial(_merge_ffn_kernel, f_chunk=f_chunk),
        grid=(n_steps,),
        in_specs=[pl.BlockSpec((tm, d), row), _const_spec((1, d)),
                  pl.BlockSpec((tm, ya.shape[1]), row), pl.BlockSpec((tm, yb.shape[1]), row),
                  pl.BlockSpec((tm, yc.shape[1]), row)] + resident
                 + [_const_spec((1, d))] + [_const_spec(w.shape, True) for w in (wg, wu, wd)] + cast_in,
        out_specs=[pl.BlockSpec((tm, d), row)] + cast_out,
        out_shape=[jax.ShapeDtypeStruct((n, d), F32)] + cast_shapes,
        compiler_params=_params(("arbitrary",)),
        name="merge_ffn",
    )(x2, gm, ya, yb, yc, wm, wa, wb, wc, wo, gf, wg, wu, wd, *[stack for stack, _ in casts])
    return outs[0], list(outs[1:])


def _block_diag_tiles(w):
    nb, k, _ = w.shape
    per = MXU_DIM // k
    w4 = w.reshape(nb // per, per, k, k)
    eye = jnp.eye(per, dtype=w.dtype)
    return jnp.einsum('tpij,pq->tpiqj', w4, eye).reshape(nb // per, MXU_DIM, MXU_DIM)


def _row(v):
    return v.reshape(1, -1).astype(F32)


def _mixer_ffn(x, layer, mix_norm, w_in, rg_conv_w, rg_conv_b, rg_w_r, rg_b_r, rg_w_i, rg_b_i, rg_lambda,
               att_q_norm, att_k_norm, dn_conv_w, dn_a_log, dn_dt_bias, dn_out_norm, w_branch, w_out,
               ffn_norm, ffn_w, casts):
    bsz, t, d = x.shape
    rg_w = rg_conv_w.shape[1]
    att_w = att_q_norm.size
    dn_w = dn_out_norm.size
    dn_h = dn_out_norm.shape[0]
    att_off = 2 * rg_w
    dn_off = att_off + 3 * att_w
    merge_off = dn_off + 4 * dn_w + 2 * dn_h
    dn_cols = 4 * dn_w + 2 * LANES
    assert dn_off % dn_cols == 0 and dn_off + dn_cols <= w_in.shape[2] and 2 * dn_h <= LANES
    w_merge = w_in[layer, :, merge_off:merge_off + 3 * d]
    g = _row(mix_norm)

    lamc = _row(-RG_C * jax.nn.softplus(-rg_lambda.astype(F32)))
    ya = _rg_branch(x, g, w_in, layer, rg_conv_w, _row(rg_conv_b),
                    _block_diag_tiles(rg_w_r).astype(BF16), _row(rg_b_r),
                    _block_diag_tiles(rg_w_i).astype(BF16), _row(rg_b_i), lamc)

    n_att_heads = att_q_norm.shape[0]
    yb = _att_branch(x, g, w_in, layer, att_off, _row(att_q_norm), _row(att_k_norm),
                     _head_mean_matrix(2 * LANES, ATT_HEAD_DIM), _att_maskbias(n_att_heads))

    pad = jnp.zeros((LANES - 2 * dn_h,), F32)
    aexp = _row(jnp.concatenate([jnp.zeros((dn_h,), F32), jnp.exp(dn_a_log.astype(F32)), pad]))
    dtb = _row(jnp.concatenate([jnp.zeros((dn_h,), F32), dn_dt_bias.astype(F32), pad]))
    yc = _dn_branch(x, g, w_in, layer, dn_off // dn_cols, dn_conv_w, aexp, dtb, _row(dn_out_norm))

    wa = w_branch[:rg_w]
    wb = w_branch[rg_w:rg_w + att_w]
    wc = w_branch[rg_w + att_w:]
    n = bsz * t
    out, cast = _merge_ffn(x.reshape(n, d), g, ya.reshape(n, rg_w), yb.reshape(n, att_w), yc.reshape(n, dn_w),
                           w_merge, wa, wb, wc, w_out, _row(ffn_norm), *ffn_w, casts)
    return out.reshape(bsz, t, d), cast


def kernel(x, ffn1_norm, ffn1_w_gate, ffn1_w_up, ffn1_w_down, mix_norm, w_in, rg_conv_w, rg_conv_b, rg_w_r, rg_b_r, rg_w_i, rg_b_i, rg_lambda, att_q_norm, att_k_norm, dn_conv_w, dn_a_log, dn_dt_bias, dn_out_norm, w_branch, w_out, ffn2_norm, ffn2_w_gate, ffn2_w_up, ffn2_w_down):
    bsz, t, d = x.shape
    n = bsz * t
    n_layers = ffn1_norm.shape[0]
    ffn1_w = [w[0].astype(BF16) for w in (ffn1_w_gate, ffn1_w_up, ffn1_w_down)]
    w_in_all = w_in.astype(BF16)
    for l in range(n_layers):
        casts = [(w, l) for w in (w_branch, w_out, ffn2_w_gate, ffn2_w_up, ffn2_w_down)]
        x, (w_branch_b, w_out_b, *ffn2_w) = _ffn(x.reshape(n, d), _row(ffn1_norm[l]), *ffn1_w, casts)
        casts = [(w, l + 1) for w in (ffn1_w_gate, ffn1_w_up, ffn1_w_down)] if l + 1 < n_layers else []
        x, ffn1_w = _mixer_ffn(x.reshape(bsz, t, d), l, mix_norm[l], w_in_all, rg_conv_w[l], rg_conv_b[l], rg_w_r[l],
                               rg_b_r[l], rg_w_i[l], rg_b_i[l], rg_lambda[l], att_q_norm[l], att_k_norm[l],
                               dn_conv_w[l], dn_a_log[l], dn_dt_bias[l], dn_out_norm[l], w_branch_b, w_out_b,
                               ffn2_norm[l], ffn2_w, casts)
    return x
```

```python
import functools
import math

import jax
import jax.numpy as jnp
from jax import lax
from jax.experimental import pallas as pl
from jax.experimental.pallas import tpu as pltpu

F32 = jnp.float32
BF16 = jnp.bfloat16

EPS = 1e-6
NEG_INF = -1e30
RG_C = 8.0
RG_BLOCK_DIM = 64
CONV_WIDTH = 4
ATT_GROUPS = ((128, 1), (512, 4), (2048, 16))
ATT_HEADS_PER_GROUP = 4
ATT_HEAD_DIM = 64
ATT_SPAN = 128
DN_HEAD_DIM = 128
DN_CHUNK = 64

LANES = 128
SUBLANES = 8
MXU_DIM = 256
VMEM_LIMIT_BYTES = 60 * 1024 * 1024


def _rms(x, g):
    return x * lax.rsqrt(jnp.mean(x * x, axis=-1, keepdims=True) + EPS) * g


def _dot(a, b):
    return jnp.dot(a, b, preferred_element_type=F32)


def _dot_nt(a, b):
    return lax.dot_general(a, b, (((1,), (1,)), ((), ())), preferred_element_type=F32)


def _dot_tn(a, b):
    return lax.dot_general(a, b, (((0,), (0,)), ((), ())), preferred_element_type=F32)


def _softplus(z):
    return jnp.maximum(z, 0.0) + jnp.log(1.0 + jnp.exp(-jnp.abs(z)))


def _const_spec(shape, single_buffer=False):
    nd = len(shape)
    mode = pl.Buffered(1) if single_buffer else None
    return pl.BlockSpec(shape, lambda *_: (0,) * nd, pipeline_mode=mode)


def _col_block_spec(rows, cols, col_block, layer, single_buffer=False):
    mode = pl.Buffered(1) if single_buffer else None
    return pl.BlockSpec((None, rows, cols), lambda *_: (layer, 0, col_block), pipeline_mode=mode)


def _params(semantics):
    return pltpu.CompilerParams(dimension_semantics=semantics, vmem_limit_bytes=VMEM_LIMIT_BYTES)


BF16_SUBLANES = 16


def _cast_blocking(rows, n_steps):
    rpb = BF16_SUBLANES
    while rows % rpb or rows // rpb > n_steps:
        rpb += BF16_SUBLANES
        assert rpb <= rows
    return rpb, rows // rpb


def _cast_specs(casts, n_steps):
    cast_in, cast_out, cast_shapes = [], [], []
    for stack, layer in casts:
        _, rows, cols = stack.shape
        rpb, nb = _cast_blocking(rows, n_steps)
        cast_in.append(pl.BlockSpec((None, rpb, cols), lambda i, layer=layer, nb=nb: (layer, jnp.minimum(i, nb - 1), 0)))
        cast_out.append(pl.BlockSpec((rpb, cols), lambda i, nb=nb: (jnp.minimum(i, nb - 1), 0)))
        cast_shapes.append(jax.ShapeDtypeStruct((rows, cols), BF16))
    return cast_in, cast_out, cast_shapes


def _ffn_kernel(x_ref, g_ref, wg_ref, wu_ref, wd_ref, *rest, f_chunk):
    n_cast = (len(rest) - 1) // 2
    o_ref = rest[n_cast]
    for src, dst in zip(rest[:n_cast], rest[n_cast + 1:]):
        dst[...] = src[...].astype(BF16)
    x = x_ref[...]
    u = _rms(x, g_ref[...]).astype(BF16)
    acc = jnp.zeros_like(x)
    for c in range(wg_ref.shape[1] // f_chunk):
        sl = slice(c * f_chunk, (c + 1) * f_chunk)
        hg = _dot(u, wg_ref[:, sl])
        hu = _dot(u, wu_ref[:, sl])
        h = (hg * jax.nn.sigmoid(hg) * hu).astype(BF16)
        acc = acc + _dot(h, wd_ref[sl, :])
    o_ref[...] = x + 0.5 * acc


def _ffn(x2, g, wg, wu, wd, casts=(), *, tm=1024, f_chunk=256):
    n, d = x2.shape
    f = wg.shape[1]
    n_steps = n // tm
    cast_in, cast_out, cast_shapes = _cast_specs(casts, n_steps)
    outs = pl.pallas_call(
        functools.partial(_ffn_kernel, f_chunk=f_chunk),
        grid=(n_steps,),
        in_specs=[pl.BlockSpec((tm, d), lambda i: (i, 0)), _const_spec((1, d)),
                  _const_spec((d, f), True), _const_spec((d, f), True), _const_spec((f, d), True)] + cast_in,
        out_specs=[pl.BlockSpec((tm, d), lambda i: (i, 0))] + cast_out,
        out_shape=[jax.ShapeDtypeStruct((n, d), F32)] + cast_shapes,
        compiler_params=_params(("arbitrary",)),
        name="ffn",
    )(x2, g, wg, wu, wd, *[stack for stack, _ in casts])
    return outs[0], list(outs[1:])


def _shift_rows(cur, prev8, s):
    rows = lax.broadcasted_iota(jnp.int32, cur.shape, 0)
    return jnp.where(rows < s, pltpu.roll(prev8, s, 0), pltpu.roll(cur, s, 0))


def _causal_conv(p, tail8, w):
    kw = w.shape[0]
    acc = p * w[kw - 1:kw, :]
    top = p[0:SUBLANES] * w[kw - 1:kw, :]
    for s in range(1, kw):
        wk = w[kw - 1 - s:kw - s, :]
        acc = acc + pltpu.roll(p, s, 0) * wk
        top = top + _shift_rows(p[0:SUBLANES], tail8, s) * wk
    return jnp.concatenate([top, acc[SUBLANES:]], axis=0)


def _segment_perm(tt):
    r = jnp.arange(tt)
    src = (r % SUBLANES) * (tt // SUBLANES) + r // SUBLANES
    return (src[:, None] == jnp.arange(tt)[None, :]).astype(BF16)


def _rg_kernel(x_ref, g_ref, perm_ref, permt_ref, wx_ref, wgate_ref, cw_ref, cb_ref, wr_ref, br_ref, wi_ref,
               bi_ref, lamc_ref, o_ref, tail_ref, carry_ref):
    rows, width = o_ref.shape
    tt = perm_ref.shape[0]
    n_sub = rows // tt
    n_grp = tt // SUBLANES
    kw = cw_ref.shape[0]

    @pl.when(pl.program_id(1) == 0)
    def _():
        tail_ref[...] = jnp.zeros_like(tail_ref)
        carry_ref[...] = jnp.zeros_like(carry_ref)

    u = _rms(x_ref[...], g_ref[...]).astype(BF16)
    nc = width // MXU_DIM
    rows8 = lax.broadcasted_iota(jnp.int32, (SUBLANES, MXU_DIM), 0)
    up = {}
    tails = {c: tail_ref[:, c * MXU_DIM:(c + 1) * MXU_DIM] for c in range(nc)}
    carries = {c: carry_ref[:, c * MXU_DIM:(c + 1) * MXU_DIM] for c in range(nc)}

    def project(t, c):
        if t not in up:
            up[t] = _dot(perm_ref[...], u[t * tt:(t + 1) * tt]).astype(BF16)
        cs = slice(c * MXU_DIM, (c + 1) * MXU_DIM)
        return _dot(up[t], wx_ref[:, cs]), _dot(up[t], wgate_ref[:, cs])

    def conv(c, px):
        cs = slice(c * MXU_DIM, (c + 1) * MXU_DIM)
        tail = tails[c]
        head = [_shift_rows(px[(n_grp - k) * SUBLANES:(n_grp - k + 1) * SUBLANES],
                            tail[(kw - 1 - k) * SUBLANES:(kw - k) * SUBLANES], 1) for k in range(kw - 1, 0, -1)]
        ext = jnp.concatenate(head + [px], axis=0)
        tails[c] = px[tt - (kw - 1) * SUBLANES:tt]
        xa = cb_ref[:, cs] + ext[0:tt] * cw_ref[0:1, cs]
        for k in range(1, kw):
            xa = xa + ext[k * SUBLANES:k * SUBLANES + tt] * cw_ref[k:k + 1, cs]
        return xa

    def gates(c, xa):
        xab = xa.astype(BF16)
        return _dot(xab, wr_ref[c]), _dot(xab, wi_ref[c])

    def scan(c, xa, r_lin, i_lin, pg):
        cs = slice(c * MXU_DIM, (c + 1) * MXU_DIM)
        a = jnp.exp(lamc_ref[:, cs] * jax.nn.sigmoid(r_lin + br_ref[:, cs]))
        b = jnp.sqrt(1.0 - a * a) * (jax.nn.sigmoid(i_lin + bi_ref[:, cs]) * xa)
        hl = jnp.zeros((SUBLANES, MXU_DIM), F32)
        ap = jnp.ones((SUBLANES, MXU_DIM), F32)
        hs, aps = [], []
        for j in range(n_grp):
            rs = slice(j * SUBLANES, (j + 1) * SUBLANES)
            hl = a[rs] * hl + b[rs]
            ap = a[rs] * ap
            hs.append(hl)
            aps.append(ap)
        for s in (1, 2, 4):
            m = rows8 >= s
            hl = jnp.where(m, ap * pltpu.roll(hl, s, 0) + hl, hl)
            ap = jnp.where(m, ap * pltpu.roll(ap, s, 0), ap)
        cin = carries[c]
        seg_end = hl + ap * cin
        carries[c] = jnp.broadcast_to(seg_end[SUBLANES - 1:SUBLANES, :], seg_end.shape)
        seg_in = jnp.where(rows8 < 1, cin, pltpu.roll(seg_end, 1, 0))
        h = jnp.concatenate([h_ + a_ * seg_in for h_, a_ in zip(hs, aps)], axis=0)
        return (h * jax.nn.gelu(pg)).astype(BF16)

    def emit(t, c, y):
        o_ref[t * tt:(t + 1) * tt, c * MXU_DIM:(c + 1) * MXU_DIM] = _dot(permt_ref[...], y).astype(BF16)

    items = [(t, c) for t in range(n_sub) for c in range(nc)]
    proj, xas, gts, ys = {}, {}, {}, {}
    for step in range(len(items) + 3):
        if step < len(items):
            proj[step] = project(*items[step])
        k = step - 1
        if 0 <= k < len(items):
            xas[k] = conv(items[k][1], proj[k][0])
            gts[k] = gates(items[k][1], xas[k])
        k = step - 2
        if 0 <= k < len(items):
            ys[k] = scan(items[k][1], xas[k], gts[k][0], gts[k][1], proj[k][1])
        k = step - 3
        if 0 <= k < len(items):
            emit(*items[k], ys[k])
    for c in range(nc):
        tail_ref[:, c * MXU_DIM:(c + 1) * MXU_DIM] = tails[c]
        carry_ref[:, c * MXU_DIM:(c + 1) * MXU_DIM] = carries[c]


def _rg_branch(x, g, w_in, layer, cw, cb, wr_t, br, wi_t, bi, lamc, *, tt=256, n_sub=4):
    bsz, t, d = x.shape
    width = cw.shape[1]
    perm = _segment_perm(tt)
    rows = tt * n_sub
    return pl.pallas_call(
        _rg_kernel,
        grid=(bsz, t // rows),
        in_specs=[pl.BlockSpec((None, rows, d), lambda b, i: (b, i, 0)), _const_spec((1, d)),
                  _const_spec((tt, tt)), _const_spec((tt, tt)),
                  _col_block_spec(d, width, 0, layer), _col_block_spec(d, width, 1, layer),
                  _const_spec(cw.shape), _const_spec((1, width)),
                  _const_spec(wr_t.shape), _const_spec((1, width)),
                  _const_spec(wi_t.shape), _const_spec((1, width)), _const_spec((1, width))],
        out_specs=pl.BlockSpec((None, rows, width), lambda b, i: (b, i, 0)),
        out_shape=jax.ShapeDtypeStruct((bsz, t, width), BF16),
        scratch_shapes=[pltpu.VMEM(((cw.shape[0] - 1) * SUBLANES, width), F32), pltpu.VMEM((SUBLANES, width), F32)],
        compiler_params=_params(("arbitrary", "arbitrary")),
        name="rg_branch",
    )(x, g, perm, perm.T, w_in, w_in, cw, cb, wr_t, br, wi_t, bi, lamc)


ATT_BLOCK_BATCH = 4


def _att_kernel(x_ref, g_ref, *rest, rt, n_w):
    w_refs = rest[:n_w]
    gq_ref, gk_ref, e4_ref, mb_ref, o_ref, qkv_scr, o_scr, lse_scr = rest[n_w:]
    t_total = o_ref.shape[0]
    n_slab = o_scr.shape[0]
    i = pl.program_id(1)
    row0 = pl.multiple_of(i * rt, rt)
    span = ATT_SPAN

    u = _rms(x_ref[...], g_ref[...]).astype(BF16)
    qkv = jnp.concatenate([_dot(u, w_ref[...]) for w_ref in w_refs], axis=1)
    width = n_slab * LANES
    e4 = e4_ref[...]
    for s in range(n_slab):
        cs = slice(s * LANES, (s + 1) * LANES)
        qk = jnp.concatenate([qkv[:, cs], qkv[:, width + s * LANES: width + (s + 1) * LANES]], axis=1)
        gain = jnp.concatenate([gq_ref[:, cs] * (ATT_HEAD_DIM ** -0.5), gk_ref[:, cs]], axis=1)
        ms = _dot((qk * qk).astype(BF16), e4)
        qk = qk * lax.rsqrt(ms + EPS) * gain
        qkv_scr[s, pl.ds(row0, rt), :] = qk[:, :LANES]
        qkv_scr[n_slab + s, pl.ds(row0, rt), :] = qk[:, LANES:]
        qkv_scr[2 * n_slab + s, pl.ds(row0, rt), :] = qkv[:, 2 * width + s * LANES: 2 * width + (s + 1) * LANES]

    @pl.when(i == pl.num_programs(1) - 1)
    def _():
        lane = lax.broadcasted_iota(jnp.int32, (span, LANES), 1)
        first_head = lane < ATT_HEAD_DIM

        for g, (window, dil) in enumerate(ATT_GROUPS):
            assert window // dil == span
            nblk = t_total // dil // span
            blocks = [(r, n) for n in range(nblk) for r in range(dil)]

            def rows(start, dil=dil):
                return pl.ds(start, span) if dil == 1 else pl.ds(start, span, stride=dil)

            for b0 in range(0, len(blocks), ATT_BLOCK_BATCH):
                probs = [(r + n * span * dil, n > 0, 2 * g + sl)
                         for r, n in blocks[b0:b0 + ATT_BLOCK_BATCH] for sl in range(2)]
                qs, kk, vv = [], [], []
                for start, has_prev, slab in probs:
                    q2 = qkv_scr[slab, rows(start), :]
                    qs.append(jnp.concatenate([jnp.where(first_head, q2, 0.0), jnp.where(first_head, 0.0, q2)],
                                              axis=0).astype(BF16))
                    kc = qkv_scr[n_slab + slab, rows(start), :]
                    vc = qkv_scr[2 * n_slab + slab, rows(start), :]
                    if has_prev:
                        prev = start - span * dil
                        kc = jnp.concatenate([qkv_scr[n_slab + slab, rows(prev), :], kc], axis=0)
                        vc = jnp.concatenate([qkv_scr[2 * n_slab + slab, rows(prev), :], vc], axis=0)
                    kk.append(kc.astype(BF16))
                    vv.append(vc.astype(BF16))
                sc = [_dot_nt(q_, k_) for q_, k_ in zip(qs, kk)]
                sc = [s_ + (mb_ref[slab] if has_prev else mb_ref[slab, :, span:])
                      for s_, (_, has_prev, slab) in zip(sc, probs)]
                mx = [jnp.max(s_, axis=-1, keepdims=True) for s_ in sc]
                pr = [jnp.exp(s_ - m_) for s_, m_ in zip(sc, mx)]
                den = [jnp.sum(p_, axis=-1, keepdims=True) for p_ in pr]
                pv = [_dot(p_.astype(BF16), v_) for p_, v_ in zip(pr, vv)]
                for (start, _, slab), pv_, m_, d_ in zip(probs, pv, mx, den):
                    on = pv_ * (1.0 / d_)
                    lse = jnp.broadcast_to(m_ + jnp.log(d_), on.shape)
                    o_scr[slab, rows(start), :] = jnp.where(first_head, on[:span], on[span:])
                    lse_scr[slab, rows(start), :] = jnp.where(first_head, lse[:span], lse[span:])

        n_grp = len(ATT_GROUPS)
        ct = 256

        def combine(j, carry):
            rws = pl.ds(pl.multiple_of(j * ct, ct), ct)
            for sl in range(2):
                ls = [lse_scr[2 * g + sl, rws, :] for g in range(n_grp)]
                mx = functools.reduce(jnp.maximum, ls)
                es = [jnp.exp(l - mx) for l in ls]
                inv = 1.0 / functools.reduce(lambda a, b: a + b, es)
                for g in range(n_grp):
                    slab = 2 * g + sl
                    o_ref[rws, slab * LANES:(slab + 1) * LANES] = (o_scr[slab, rws, :] * (es[g] * inv)).astype(BF16)
            return carry

        lax.fori_loop(0, t_total // ct, combine, 0)


def _att_branch(x, g, w_in, layer, w_off, gq, gk, e4, maskbias, *, rt=512):
    bsz, t, d = x.shape
    width = gq.shape[1]
    n_slab = width // LANES
    assert w_off % MXU_DIM == 0 and (3 * width) % MXU_DIM == 0
    n_w = 3 * width // MXU_DIM
    return pl.pallas_call(
        functools.partial(_att_kernel, rt=rt, n_w=n_w),
        grid=(bsz, t // rt),
        in_specs=[pl.BlockSpec((None, rt, d), lambda b, i: (b, i, 0)), _const_spec((1, d))]
                 + [_col_block_spec(d, MXU_DIM, w_off // MXU_DIM + j, layer, True) for j in range(n_w)]
                 + [_const_spec((1, width)), _const_spec((1, width)),
                    _const_spec(e4.shape), _const_spec(maskbias.shape, True)],
        out_specs=pl.BlockSpec((None, t, width), lambda b, i: (b, 0, 0)),
        out_shape=jax.ShapeDtypeStruct((bsz, t, width), BF16),
        scratch_shapes=[pltpu.VMEM((3 * n_slab, t, LANES), F32), pltpu.VMEM((n_slab, t, LANES), F32),
                        pltpu.VMEM((n_slab, t, LANES), F32)],
        compiler_params=_params(("arbitrary", "arbitrary")),
        name="att_branch",
    )(x, g, *([w_in] * n_w), gq, gk, e4, maskbias)


def _att_maskbias(n_heads):
    slopes = jnp.exp2(-8.0 * jnp.arange(1, n_heads + 1, dtype=F32) / n_heads)
    qi = jnp.arange(ATT_SPAN)[:, None]
    kj = jnp.arange(2 * ATT_SPAN)[None, :]
    delta = qi + ATT_SPAN - kj
    valid = (delta >= 0) & (delta <= ATT_SPAN)
    dil = jnp.repeat(jnp.array([d for _, d in ATT_GROUPS], F32), ATT_HEADS_PER_GROUP)
    bias = -(slopes * dil)[:, None, None] * delta.astype(F32)[None]
    return jnp.where(valid[None], bias, NEG_INF).reshape(n_heads // 2, 2 * ATT_SPAN, 2 * ATT_SPAN)


def _head_mean_matrix(n_lanes, head_dim):
    lane_head = jnp.arange(n_lanes) // head_dim
    return ((lane_head[:, None] == lane_head[None, :]).astype(F32) / head_dim).astype(BF16)


def _cumsum_rows(x):
    n = x.shape[0]
    rows = lax.broadcasted_iota(jnp.int32, x.shape, 0)
    s = 1
    while s < n:
        x = x + jnp.where(rows >= s, pltpu.roll(x, s, 0), 0.0)
        s *= 2
    return x


def _dn_kernel(*refs, tiles_per_seq):
    tail_ref, s_scr, *stage_scratch = refs[8:]
    step = pl.program_id(0)

    @pl.when(step == 0)
    def _():
        for ref in stage_scratch:
            ref[...] = jnp.zeros_like(ref)

    @pl.when(step % tiles_per_seq == 0)
    def _():
        tail_ref[...] = jnp.zeros_like(tail_ref)

    @pl.when((step % tiles_per_seq == 1) | (step == 0))
    def _():
        s_scr[...] = jnp.zeros_like(s_scr)

    for parity in range(2):
        pl.when(step % 2 == parity)(functools.partial(_dn_body, *refs, rd=1 - parity, wr=parity))


def _dn_body(x_ref, g_ref, w_ref, cw_ref, aexp_ref, dtb_ref, on_ref, o_ref, tail_ref, s_scr,
             kbq_s, k_s, rhs_s, kd_s, qeg_s, gam_s, gamt_s, z_s, *, rd, wr):
    tt, width = o_ref.shape
    n_heads = width // DN_HEAD_DIM
    c = DN_CHUNK

    u = _rms(x_ref[...], g_ref[...]).astype(BF16)
    parts = {}

    def project(*idx):
        for j in idx:
            parts[j] = _dot(u, w_ref[:, j * MXU_DIM:(j + 1) * MXU_DIM])

    assert 2 * c == LANES and n_heads % 4 == 0
    n_ch = tt // c
    n_pair = n_heads // 2
    row2 = lax.broadcasted_iota(jnp.int32, (c, LANES), 0)
    lane2 = lax.broadcasted_iota(jnp.int32, (c, LANES), 1)
    col2 = lane2 & (c - 1)
    incl = col2 <= row2
    strict = col2 < row2
    lo2 = lane2 < c
    lane2x = lax.broadcasted_iota(jnp.int32, (LANES, LANES), 1)
    lo2x = lane2x < c
    row4 = lax.broadcasted_iota(jnp.int32, (c, 2 * LANES), 0)
    lane4 = lax.broadcasted_iota(jnp.int32, (c, 2 * LANES), 1)
    eye4 = ((lane4 & (c - 1)) == row4).astype(F32)
    blk4 = lane4 // c
    zero_k = jnp.zeros((c, DN_HEAD_DIM), BF16)

    def block_diag(m):
        return jnp.concatenate([jnp.where(blk4 == i, m, 0.0) for i in range(4)], axis=0).astype(BF16)

    def l2n(m):
        return m * lax.rsqrt(jnp.sum(m * m, axis=-1, keepdims=True) + EPS)

    probs = [(ch, h) for ch in range(n_ch) for h in range(n_heads)]
    pairs = [(ch, p) for ch in range(n_ch) for p in range(n_pair)]
    quads = [(ch, qd) for ch in range(n_ch) for qd in range(n_pair // 2)]
    pix = {ph: i for i, ph in enumerate(probs)}

    n_prob = len(probs)
    common = {}
    ba_piece = 4 * n_pair

    def prep_common():
        ba = parts[ba_piece][:, :LANES]
        common["beta"] = jax.nn.sigmoid(ba)
        glog_all = -aexp_ref[...] * _softplus(ba + dtb_ref[...])
        common["gam"] = [_cumsum_rows(glog_all[ch * c:(ch + 1) * c]) for ch in range(n_ch)]
        for ch, g_ in enumerate(common["gam"]):
            gam_s[wr, ch] = g_
            gamt_s[wr, ch] = jnp.concatenate([g_, g_], axis=0).T

    def prep_pair(p):
        def conv_silu(section):
            blk = parts[section * n_pair + p]
            cs = slice(section * width + p * MXU_DIM, section * width + (p + 1) * MXU_DIM)
            y = _causal_conv(blk, tail_ref[:, cs], cw_ref[:, cs])
            tail_ref[:, cs] = blk[tt - SUBLANES:tt]
            return y * jax.nn.sigmoid(y)

        qs, ks, vs = conv_silu(0), conv_silu(1), conv_silu(2)
        for ch in range(n_ch):
            rs = slice(ch * c, (ch + 1) * c)
            for sub in range(2):
                h = 2 * p + sub
                ls = slice(sub * DN_HEAD_DIM, (sub + 1) * DN_HEAD_DIM)
                q_ = l2n(qs[rs, ls]) * (DN_HEAD_DIM ** -0.5)
                k_ = l2n(ks[rs, ls])
                beta = common["beta"][rs, h:h + 1]
                gc = common["gam"][ch][:, n_heads + h:n_heads + h + 1]
                eg_ = jnp.exp(gc)
                kb_ = k_ * beta
                i = pix[(ch, h)]
                rhs_s[wr, i] = jnp.concatenate([vs[rs, ls] * beta, kb_ * eg_], axis=1).astype(BF16)
                kbq_s[wr, i] = jnp.concatenate([kb_, q_], axis=0).astype(BF16)
                kd_s[wr, i] = k_ * jnp.exp(gc[c - 1:c, :] - gc)
                qeg_s[wr, i] = (q_ * eg_).astype(BF16)
                k_s[wr, i] = k_.astype(BF16)

    fill = iter([
        lambda: project(ba_piece, 0, n_pair, 2 * n_pair),
        lambda: (prep_common(), project(1, n_pair + 1, 2 * n_pair + 1)),
        lambda: prep_pair(0),
        lambda: project(2, n_pair + 2, 2 * n_pair + 2),
        lambda: prep_pair(1),
        lambda: project(3, n_pair + 3, 2 * n_pair + 3),
        lambda: prep_pair(2),
        lambda: project(3 * n_pair, 3 * n_pair + 1),
        lambda: prep_pair(3),
        lambda: project(3 * n_pair + 2),
        lambda: project(3 * n_pair + 3),
    ])
    assert n_pair == 4

    def interleave():
        action = next(fill, None)
        if action is not None:
            action()

    kbq = [kbq_s[rd, i] for i in range(n_prob)]
    k = [k_s[rd, i] for i in range(n_prob)]
    rhs = [rhs_s[rd, i] for i in range(n_prob)]
    kd = [kd_s[rd, i] for i in range(n_prob)]
    qeg = [qeg_s[rd, i] for i in range(n_prob)]
    gam_all = [gam_s[rd, ch] for ch in range(n_ch)]
    gam_t = [gamt_s[rd, ch] for ch in range(n_ch)]
    z = z_s[rd]
    gcol = [gam_all[ch][:, n_heads + h:n_heads + h + 1] for ch, h in probs]
    glast = [gc[c - 1:c, :] for gc in gcol]

    def both(lst, ch, p):
        return lst[pix[(ch, 2 * p)]], lst[pix[(ch, 2 * p + 1)]]

    interleave()
    kq = []
    for ch, p in pairs:
        (x0, x1), (k0, k1) = both(kbq, ch, p), both(k, ch, p)
        kq.append(_dot_nt(x0, jnp.concatenate([k0.astype(BF16), zero_k], axis=0))
                  + _dot_nt(x1, jnp.concatenate([zero_k, k1.astype(BF16)], axis=0)))
    interleave()
    decay = []
    for ch, p in pairs:
        g0, g1 = both(gcol, ch, p)
        gr = jnp.where(lo2[0:1], gam_t[ch][n_heads + 2 * p:n_heads + 2 * p + 1, :],
                       gam_t[ch][n_heads + 2 * p + 1:n_heads + 2 * p + 2, :])
        diff = jnp.where(lo2, g0, g1) - gr
        decay.append(jnp.where(incl, jnp.exp(jnp.where(incl, diff, 0.0)), 0.0))
    a2 = [jnp.where(strict, kq_[:c] * d_, 0.0) for kq_, d_ in zip(kq, decay)]
    qk2 = [jnp.where(incl, kq_[c:] * d_, 0.0) for kq_, d_ in zip(kq, decay)]

    a4 = [jnp.concatenate([a2[ch * n_pair + 2 * qd], a2[ch * n_pair + 2 * qd + 1]], axis=1) for ch, qd in quads]
    tinv = [eye4 - a_ for a_ in a4]
    pw = [_dot(a_.astype(BF16), block_diag(a_)) for a_ in a4]
    interleave()
    for _ in range(int(math.log2(c)) - 2):
        r = [_dot(jnp.concatenate([t_, p_], axis=0).astype(BF16), block_diag(p_)) for t_, p_ in zip(tinv, pw)]
        tinv = [t_ + r_[:c] for t_, r_ in zip(tinv, r)]
        pw = [r_[c:] for r_ in r]
        interleave()
    tinv = [t_ + _dot(t_.astype(BF16), block_diag(p_)) for t_, p_ in zip(tinv, pw)]
    interleave()

    def split_rows(m):
        return jnp.concatenate([jnp.where(lo2, m, 0.0), jnp.where(lo2, 0.0, m)], axis=0).astype(BF16)

    sol, lhs_o = [], []
    for i, (ch, p) in enumerate(pairs):
        t2 = tinv[i // 2][:, (i % 2) * LANES:(i % 2 + 1) * LANES]
        r0, r1 = both(rhs, ch, p)
        sol.append(_dot(split_rows(t2), jnp.concatenate([r0, r1], axis=0)))
        kd0, kd1 = both(kd, ch, p)
        kdt = jnp.concatenate([kd0, kd1], axis=0).T
        lhs_o.append(jnp.concatenate([split_rows(qk2[i]), jnp.where(lo2x, kdt, 0.0).astype(BF16),
                                      jnp.where(lo2x, 0.0, kdt).astype(BF16)], axis=0))
    wq = []
    for ch, h in probs:
        s_ = sol[ch * n_pair + h // 2][(h % 2) * c:(h % 2 + 1) * c]
        i = pix[(ch, h)]
        wq.append(jnp.concatenate([s_[:, DN_HEAD_DIM:].astype(BF16), qeg[i]], axis=0))
    interleave()

    for ch in range(n_ch):
        rs = slice(ch * c, (ch + 1) * c)
        state = [s_scr[h] for h in range(n_heads)]
        ws = [_dot(wq[pix[(ch, h)]], state[h].astype(BF16)) for h in range(n_heads)]
        interleave()
        un = [sol[ch * n_pair + h // 2][(h % 2) * c:(h % 2 + 1) * c, :DN_HEAD_DIM] - ws[h][:c] for h in range(n_heads)]
        ou = [_dot(lhs_o[ch * n_pair + p], jnp.concatenate([un[2 * p], un[2 * p + 1]], axis=0).astype(BF16))
              for p in range(n_pair)]
        interleave()
        for h in range(n_heads):
            hs = slice(h * DN_HEAD_DIM, (h + 1) * DN_HEAD_DIM)
            o2 = ou[h // 2]
            sub = h % 2
            s_scr[h] = (state[h] * jnp.exp(glast[pix[(ch, h)]])
                        + o2[2 * c + sub * DN_HEAD_DIM:2 * c + (sub + 1) * DN_HEAD_DIM])
            on = _rms(ws[h][c:] + o2[sub * c:(sub + 1) * c], on_ref[:, hs])
            zz = z[rs, hs]
            o_ref[rs, hs] = (on * (zz * jax.nn.sigmoid(zz))).astype(BF16)

    for action in fill:
        action()
    z_s[wr] = jnp.concatenate([parts[3 * n_pair + j] for j in range(n_pair)], axis=1)


def _dn_branch(x, g, w_in, layer, w_block, cw, aexp, dtb, onorm, *, tt=256):
    bsz, t, d = x.shape
    width = onorm.shape[1]
    n_heads = width // DN_HEAD_DIM
    w_cols = 4 * width + 2 * LANES
    tiles_per_seq = t // tt
    n_tiles = bsz * tiles_per_seq
    n_prob = (tt // DN_CHUNK) * n_heads
    c = DN_CHUNK

    def x_map(s):
        j = jnp.minimum(s, n_tiles - 1)
        return j // tiles_per_seq, j % tiles_per_seq, 0

    def o_map(s):
        j = jnp.maximum(s - 1, 0)
        return j // tiles_per_seq, j % tiles_per_seq, 0

    assert tiles_per_seq > 1
    return pl.pallas_call(
        functools.partial(_dn_kernel, tiles_per_seq=tiles_per_seq),
        grid=(n_tiles + 1,),
        in_specs=[pl.BlockSpec((None, tt, d), x_map), _const_spec((1, d)),
                  _col_block_spec(d, w_cols, w_block, layer), _const_spec(cw.shape),
                  _const_spec((1, LANES)), _const_spec((1, LANES)), _const_spec((1, width))],
        out_specs=pl.BlockSpec((None, tt, width), o_map),
        out_shape=jax.ShapeDtypeStruct((bsz, t, width), BF16),
        scratch_shapes=[pltpu.VMEM((SUBLANES, 3 * width), F32),
                        pltpu.VMEM((n_heads, DN_HEAD_DIM, DN_HEAD_DIM), F32),
                        pltpu.VMEM((2, n_prob, 2 * c, DN_HEAD_DIM), BF16),
                        pltpu.VMEM((2, n_prob, c, DN_HEAD_DIM), BF16),
                        pltpu.VMEM((2, n_prob, c, 2 * DN_HEAD_DIM), BF16),
                        pltpu.VMEM((2, n_prob, c, DN_HEAD_DIM), F32),
                        pltpu.VMEM((2, n_prob, c, DN_HEAD_DIM), BF16),
                        pltpu.VMEM((2, tt // c, c, LANES), F32),
                        pltpu.VMEM((2, tt // c, LANES, LANES), F32),
                        pltpu.VMEM((2, tt, width), F32)],
        compiler_params=_params(("arbitrary",)),
        name="dn_branch",
    )(x, g, w_in, cw, aexp, dtb, onorm)


def _merge_ffn_kernel(x_ref, gm_ref, ya_ref, yb_ref, yc_ref, wm_ref, wa_ref, wb_ref, wc_ref, wo_ref,
                      gf_ref, wg_ref, wu_ref, wd_ref, *rest, f_chunk):
    n_cast = (len(rest) - 1) // 2
    o_ref = rest[n_cast]
    for src, dst in zip(rest[:n_cast], rest[n_cast + 1:]):
        dst[...] = src[...].astype(BF16)
    x = x_ref[...]
    d = x.shape[1]
    u = _rms(x, gm_ref[...]).astype(BF16)
    y = jnp.zeros_like(x)
    for j, (yr, wr) in enumerate(((ya_ref, wa_ref), (yb_ref, wb_ref), (yc_ref, wc_ref))):
        gate = jax.nn.sigmoid(_dot(u, wm_ref[:, j * d:(j + 1) * d]))
        y = y + gate * _dot(yr[...], wr[...])
    x = x + _dot(y.astype(BF16), wo_ref[...])
    u = _rms(x, gf_ref[...]).astype(BF16)
    acc = jnp.zeros_like(x)
    for c in range(wg_ref.shape[1] // f_chunk):
        sl = slice(c * f_chunk, (c + 1) * f_chunk)
        hg = _dot(u, wg_ref[:, sl])
        hu = _dot(u, wu_ref[:, sl])
        h = (hg * jax.nn.sigmoid(hg) * hu).astype(BF16)
        acc = acc + _dot(h, wd_ref[sl, :])
    o_ref[...] = x + 0.5 * acc


def _merge_ffn(x2, gm, ya, yb, yc, wm, wa, wb, wc, wo, gf, wg, wu, wd, casts=(), *, tm=512, f_chunk=256):
    n, d = x2.shape
    n_steps = n // tm
    row = lambda i: (i, 0)
    cast_in, cast_out, cast_shapes = _cast_specs(casts, n_steps)
    resident = [_const_spec(w.shape, True) for w in (wm, wa, wb, wc, wo)]
    outs = pl.pallas_call(
        functools.partial(_merge_ffn_kernel, f_chunk=f_chunk),
        grid=(n_steps,),
        in_specs=[pl.BlockSpec((tm, d), row), _const_spec((1, d)),
                  pl.BlockSpec((tm, ya.shape[1]), row), pl.BlockSpec((tm, yb.shape[1]), row),
                  pl.BlockSpec((tm, yc.shape[1]), row)] + resident
                 + [_const_spec((1, d))] + [_const_spec(w.shape, True) for w in (wg, wu, wd)] + cast_in,
        out_specs=[pl.BlockSpec((tm, d), row)] + cast_out,
        out_shape=[jax.ShapeDtypeStruct((n, d), F32)] + cast_shapes,
        compiler_params=_params(("arbitrary",)),
        name="merge_ffn",
    )(x2, gm, ya, yb, yc, wm, wa, wb, wc, wo, gf, wg, wu, wd, *[stack for stack, _ in casts])
    return outs[0], list(outs[1:])


def _block_diag_tiles(w):
    nb, k, _ = w.shape
    per = MXU_DIM // k
    w4 = w.reshape(nb // per, per, k, k)
    eye = jnp.eye(per, dtype=w.dtype)
    return jnp.einsum('tpij,pq->tpiqj', w4, eye).reshape(nb // per, MXU_DIM, MXU_DIM)


def _row(v):
    return v.reshape(1, -1).astype(F32)


def _mixer_ffn(x, layer, mix_norm, w_in, rg_conv_w, rg_conv_b, rg_w_r, rg_b_r, rg_w_i, rg_b_i, rg_lambda,
               att_q_norm, att_k_norm, dn_conv_w, dn_a_log, dn_dt_bias, dn_out_norm, w_branch, w_out,
               ffn_norm, ffn_w, casts):
    bsz, t, d = x.shape
    rg_w = rg_conv_w.shape[1]
    att_w = att_q_norm.size
    dn_w = dn_out_norm.size
    dn_h = dn_out_norm.shape[0]
    att_off = 2 * rg_w
    dn_off = att_off + 3 * att_w
    merge_off = dn_off + 4 * dn_w + 2 * dn_h
    dn_cols = 4 * dn_w + 2 * LANES
    assert dn_off % dn_cols == 0 and dn_off + dn_cols <= w_in.shape[2] and 2 * dn_h <= LANES
    w_merge = w_in[layer, :, merge_off:merge_off + 3 * d]
    g = _row(mix_norm)

    lamc = _row(-RG_C * jax.nn.softplus(-rg_lambda.astype(F32)))
    ya = _rg_branch(x, g, w_in, layer, rg_conv_w, _row(rg_conv_b),
                    _block_diag_tiles(rg_w_r).astype(BF16), _row(rg_b_r),
                    _block_diag_tiles(rg_w_i).astype(BF16), _row(rg_b_i), lamc)

    n_att_heads = att_q_norm.shape[0]
    yb = _att_branch(x, g, w_in, layer, att_off, _row(att_q_norm), _row(att_k_norm),
                     _head_mean_matrix(2 * LANES, ATT_HEAD_DIM), _att_maskbias(n_att_heads))

    pad = jnp.zeros((LANES - 2 * dn_h,), F32)
    aexp = _row(jnp.concatenate([jnp.zeros((dn_h,), F32), jnp.exp(dn_a_log.astype(F32)), pad]))
    dtb = _row(jnp.concatenate([jnp.zeros((dn_h,), F32), dn_dt_bias.astype(F32), pad]))
    yc = _dn_branch(x, g, w_in, layer, dn_off // dn_cols, dn_conv_w, aexp, dtb, _row(dn_out_norm))

    wa = w_branch[:rg_w]
    wb = w_branch[rg_w:rg_w + att_w]
    wc = w_branch[rg_w + att_w:]
    n = bsz * t
    out, cast = _merge_ffn(x.reshape(n, d), g, ya.reshape(n, rg_w), yb.reshape(n, att_w), yc.reshape(n, dn_w),
                           w_merge, wa, wb, wc, w_out, _row(ffn_norm), *ffn_w, casts)
    return out.reshape(bsz, t, d), cast


def kernel(x, ffn1_norm, ffn1_w_gate, ffn1_w_up, ffn1_w_down, mix_norm, w_in, rg_conv_w, rg_conv_b, rg_w_r, rg_b_r, rg_w_i, rg_b_i, rg_lambda, att_q_norm, att_k_norm, dn_conv_w, dn_a_log, dn_dt_bias, dn_out_norm, w_branch, w_out, ffn2_norm, ffn2_w_gate, ffn2_w_up, ffn2_w_down):
    bsz, t, d = x.shape
    n = bsz * t
    n_layers = ffn1_norm.shape[0]
    ffn1_w = [w[0].astype(BF16) for w in (ffn1_w_gate, ffn1_w_up, ffn1_w_down)]
    w_in_all = w_in.astype(BF16)
    for l in range(n_layers):
        casts = [(w, l) for w in (w_branch, w_out, ffn2_w_gate, ffn2_w_up, ffn2_w_down)]
        x, (w_branch_b, w_out_b, *ffn2_w) = _ffn(x.reshape(n, d), _row(ffn1_norm[l]), *ffn1_w, casts)
        casts = [(w, l + 1) for w in (ffn1_w_gate, ffn1_w_up, ffn1_w_down)] if l + 1 < n_layers else []
        x, ffn1_w = _mixer_ffn(x.reshape(bsz, t, d), l, mix_norm[l], w_in_all, rg_conv_w[l], rg_conv_b[l], rg_w_r[l],
                               rg_b_r[l], rg_w_i[l], rg_b_i[l], rg_lambda[l], att_q_norm[l], att_k_norm[l],
                               dn_conv_w[l], dn_a_log[l], dn_dt_bias[l], dn_out_norm[l], w_branch_b, w_out_b,
                               ffn2_norm[l], ffn2_w, casts)
    return x
```

```python
import functools
import math

import jax
import jax.numpy as jnp
from jax import lax
from jax.experimental import pallas as pl
from jax.experimental.pallas import tpu as pltpu

F32 = jnp.float32
BF16 = jnp.bfloat16

EPS = 1e-6
NEG_INF = -1e30
RG_C = 8.0
RG_BLOCK_DIM = 64
CONV_WIDTH = 4
ATT_GROUPS = ((128, 1), (512, 4), (2048, 16))
ATT_HEADS_PER_GROUP = 4
ATT_HEAD_DIM = 64
ATT_SPAN = 128
DN_HEAD_DIM = 128
DN_CHUNK = 64

LANES = 128
SUBLANES = 8
MXU_DIM = 256
VMEM_LIMIT_BYTES = 60 * 1024 * 1024


def _rms(x, g):
    return x * lax.rsqrt(jnp.mean(x * x, axis=-1, keepdims=True) + EPS) * g


def _dot(a, b):
    return jnp.dot(a, b, preferred_element_type=F32)


def _dot_nt(a, b):
    return lax.dot_general(a, b, (((1,), (1,)), ((), ())), preferred_element_type=F32)


def _dot_tn(a, b):
    return lax.dot_general(a, b, (((0,), (0,)), ((), ())), preferred_element_type=F32)


def _softplus(z):
    return jnp.maximum(z, 0.0) + jnp.log(1.0 + jnp.exp(-jnp.abs(z)))


def _const_spec(shape, single_buffer=False):
    nd = len(shape)
    mode = pl.Buffered(1) if single_buffer else None
    return pl.BlockSpec(shape, lambda *_: (0,) * nd, pipeline_mode=mode)


def _col_block_spec(rows, cols, col_block, layer, single_buffer=False):
    mode = pl.Buffered(1) if single_buffer else None
    return pl.BlockSpec((None, rows, cols), lambda *_: (layer, 0, col_block), pipeline_mode=mode)


def _params(semantics):
    return pltpu.CompilerParams(dimension_semantics=semantics, vmem_limit_bytes=VMEM_LIMIT_BYTES)


BF16_SUBLANES = 16


def _cast_blocking(rows, n_steps):
    rpb = BF16_SUBLANES
    while rows % rpb or rows // rpb > n_steps:
        rpb += BF16_SUBLANES
        assert rpb <= rows
    return rpb, rows // rpb


def _cast_specs(casts, n_steps):
    cast_in, cast_out, cast_shapes = [], [], []
    for stack, layer in casts:
        _, rows, cols = stack.shape
        rpb, nb = _cast_blocking(rows, n_steps)
        cast_in.append(pl.BlockSpec((None, rpb, cols), lambda i, layer=layer, nb=nb: (layer, jnp.minimum(i, nb - 1), 0)))
        cast_out.append(pl.BlockSpec((rpb, cols), lambda i, nb=nb: (jnp.minimum(i, nb - 1), 0)))
        cast_shapes.append(jax.ShapeDtypeStruct((rows, cols), BF16))
    return cast_in, cast_out, cast_shapes


def _ffn_kernel(x_ref, g_ref, wg_ref, wu_ref, wd_ref, *rest, f_chunk):
    n_cast = (len(rest) - 1) // 2
    o_ref = rest[n_cast]
    for src, dst in zip(rest[:n_cast], rest[n_cast + 1:]):
        dst[...] = src[...].astype(BF16)
    x = x_ref[...]
    u = _rms(x, g_ref[...]).astype(BF16)
    acc = jnp.zeros_like(x)
    for c in range(wg_ref.shape[1] // f_chunk):
        sl = slice(c * f_chunk, (c + 1) * f_chunk)
        hg = _dot(u, wg_ref[:, sl])
        hu = _dot(u, wu_ref[:, sl])
        h = (hg * jax.nn.sigmoid(hg) * hu).astype(BF16)
        acc = acc + _dot(h, wd_ref[sl, :])
    o_ref[...] = x + 0.5 * acc


def _ffn(x2, g, wg, wu, wd, casts=(), *, tm=1024, f_chunk=256):
    n, d = x2.shape
    f = wg.shape[1]
    n_steps = n // tm
    cast_in, cast_out, cast_shapes = _cast_specs(casts, n_steps)
    outs = pl.pallas_call(
        functools.partial(_ffn_kernel, f_chunk=f_chunk),
        grid=(n_steps,),
        in_specs=[pl.BlockSpec((tm, d), lambda i: (i, 0)), _const_spec((1, d)),
                  _const_spec((d, f), True), _const_spec((d, f), True), _const_spec((f, d), True)] + cast_in,
        out_specs=[pl.BlockSpec((tm, d), lambda i: (i, 0))] + cast_out,
        out_shape=[jax.ShapeDtypeStruct((n, d), F32)] + cast_shapes,
        compiler_params=_params(("arbitrary",)),
        name="ffn",
    )(x2, g, wg, wu, wd, *[stack for stack, _ in casts])
    return outs[0], list(outs[1:])


def _shift_rows(cur, prev8, s):
    rows = lax.broadcasted_iota(jnp.int32, cur.shape, 0)
    return jnp.where(rows < s, pltpu.roll(prev8, s, 0), pltpu.roll(cur, s, 0))


def _causal_conv(p, tail8, w):
    kw = w.shape[0]
    acc = p * w[kw - 1:kw, :]
    top = p[0:SUBLANES] * w[kw - 1:kw, :]
    for s in range(1, kw):
        wk = w[kw - 1 - s:kw - s, :]
        acc = acc + pltpu.roll(p, s, 0) * wk
        top = top + _shift_rows(p[0:SUBLANES], tail8, s) * wk
    return jnp.concatenate([top, acc[SUBLANES:]], axis=0)


def _segment_perm(tt):
    r = jnp.arange(tt)
    src = (r % SUBLANES) * (tt // SUBLANES) + r // SUBLANES
    return (src[:, None] == jnp.arange(tt)[None, :]).astype(BF16)


def _rg_kernel(x_ref, g_ref, perm_ref, permt_ref, wx_ref, wgate_ref, cw_ref, cb_ref, wr_ref, br_ref, wi_ref,
               bi_ref, lamc_ref, o_ref, tail_ref, carry_ref):
    rows, width = o_ref.shape
    tt = perm_ref.shape[0]
    n_sub = rows // tt
    n_grp = tt // SUBLANES
    kw = cw_ref.shape[0]

    @pl.when(pl.program_id(1) == 0)
    def _():
        tail_ref[...] = jnp.zeros_like(tail_ref)
        carry_ref[...] = jnp.zeros_like(carry_ref)

    u = _rms(x_ref[...], g_ref[...]).astype(BF16)
    nc = width // MXU_DIM
    rows8 = lax.broadcasted_iota(jnp.int32, (SUBLANES, MXU_DIM), 0)
    up = {}
    tails = {c: tail_ref[:, c * MXU_DIM:(c + 1) * MXU_DIM] for c in range(nc)}
    carries = {c: carry_ref[:, c * MXU_DIM:(c + 1) * MXU_DIM] for c in range(nc)}

    def project(t, c):
        if t not in up:
            up[t] = _dot(perm_ref[...], u[t * tt:(t + 1) * tt]).astype(BF16)
        cs = slice(c * MXU_DIM, (c + 1) * MXU_DIM)
        return _dot(up[t], wx_ref[:, cs]), _dot(up[t], wgate_ref[:, cs])

    def conv(c, px):
        cs = slice(c * MXU_DIM, (c + 1) * MXU_DIM)
        tail = tails[c]
        head = [_shift_rows(px[(n_grp - k) * SUBLANES:(n_grp - k + 1) * SUBLANES],
                            tail[(kw - 1 - k) * SUBLANES:(kw - k) * SUBLANES], 1) for k in range(kw - 1, 0, -1)]
        ext = jnp.concatenate(head + [px], axis=0)
        tails[c] = px[tt - (kw - 1) * SUBLANES:tt]
        xa = cb_ref[:, cs] + ext[0:tt] * cw_ref[0:1, cs]
        for k in range(1, kw):
            xa = xa + ext[k * SUBLANES:k * SUBLANES + tt] * cw_ref[k:k + 1, cs]
        return xa

    def gates(c, xa):
        xab = xa.astype(BF16)
        return _dot(xab, wr_ref[c]), _dot(xab, wi_ref[c])

    def scan(c, xa, r_lin, i_lin, pg):
        cs = slice(c * MXU_DIM, (c + 1) * MXU_DIM)
        a = jnp.exp(lamc_ref[:, cs] * jax.nn.sigmoid(r_lin + br_ref[:, cs]))
        b = jnp.sqrt(1.0 - a * a) * (jax.nn.sigmoid(i_lin + bi_ref[:, cs]) * xa)
        hl = jnp.zeros((SUBLANES, MXU_DIM), F32)
        ap = jnp.ones((SUBLANES, MXU_DIM), F32)
        hs, aps = [], []
        for j in range(n_grp):
            rs = slice(j * SUBLANES, (j + 1) * SUBLANES)
            hl = a[rs] * hl + b[rs]
            ap = a[rs] * ap
            hs.append(hl)
            aps.append(ap)
        for s in (1, 2, 4):
            m = rows8 >= s
            hl = jnp.where(m, ap * pltpu.roll(hl, s, 0) + hl, hl)
            ap = jnp.where(m, ap * pltpu.roll(ap, s, 0), ap)
        cin = carries[c]
        seg_end = hl + ap * cin
        carries[c] = jnp.broadcast_to(seg_end[SUBLANES - 1:SUBLANES, :], seg_end.shape)
        seg_in = jnp.where(rows8 < 1, cin, pltpu.roll(seg_end, 1, 0))
        h = jnp.concatenate([h_ + a_ * seg_in for h_, a_ in zip(hs, aps)], axis=0)
        k0 = 2.0 * math.sqrt(2.0 / math.pi)
        gate = pg * jax.nn.sigmoid(pg * (k0 + (k0 * 0.044715) * (pg * pg)))
        return (h * gate).astype(BF16)

    def emit(t, c, y):
        o_ref[t * tt:(t + 1) * tt, c * MXU_DIM:(c + 1) * MXU_DIM] = _dot(permt_ref[...], y).astype(BF16)

    items = [(t, c) for t in range(n_sub) for c in range(nc)]
    proj, xas, gts, ys = {}, {}, {}, {}
    for step in range(len(items) + 3):
        if step < len(items):
            proj[step] = project(*items[step])
        k = step - 1
        if 0 <= k < len(items):
            xas[k] = conv(items[k][1], proj[k][0])
            gts[k] = gates(items[k][1], xas[k])
        k = step - 2
        if 0 <= k < len(items):
            ys[k] = scan(items[k][1], xas[k], gts[k][0], gts[k][1], proj[k][1])
        k = step - 3
        if 0 <= k < len(items):
            emit(*items[k], ys[k])
    for c in range(nc):
        tail_ref[:, c * MXU_DIM:(c + 1) * MXU_DIM] = tails[c]
        carry_ref[:, c * MXU_DIM:(c + 1) * MXU_DIM] = carries[c]


def _rg_branch(x, g, w_in, layer, cw, cb, wr_t, br, wi_t, bi, lamc, *, tt=256, n_sub=4):
    bsz, t, d = x.shape
    width = cw.shape[1]
    perm = _segment_perm(tt)
    rows = tt * n_sub
    return pl.pallas_call(
        _rg_kernel,
        grid=(bsz, t // rows),
        in_specs=[pl.BlockSpec((None, rows, d), lambda b, i: (b, i, 0)), _const_spec((1, d)),
                  _const_spec((tt, tt)), _const_spec((tt, tt)),
                  _col_block_spec(d, width, 0, layer), _col_block_spec(d, width, 1, layer),
                  _const_spec(cw.shape), _const_spec((1, width)),
                  _const_spec(wr_t.shape), _const_spec((1, width)),
                  _const_spec(wi_t.shape), _const_spec((1, width)), _const_spec((1, width))],
        out_specs=pl.BlockSpec((None, rows, width), lambda b, i: (b, i, 0)),
        out_shape=jax.ShapeDtypeStruct((bsz, t, width), BF16),
        scratch_shapes=[pltpu.VMEM(((cw.shape[0] - 1) * SUBLANES, width), F32), pltpu.VMEM((SUBLANES, width), F32)],
        compiler_params=_params(("arbitrary", "arbitrary")),
        name="rg_branch",
    )(x, g, perm, perm.T, w_in, w_in, cw, cb, wr_t, br, wi_t, bi, lamc)


ATT_BLOCK_BATCH = 4


def _att_kernel(x_ref, g_ref, *rest, rt, n_w):
    w_refs = rest[:n_w]
    gq_ref, gk_ref, e4_ref, mb_ref, o_ref, qkv_scr, o_scr, lse_scr = rest[n_w:]
    t_total = o_ref.shape[0]
    n_slab = o_scr.shape[0]
    i = pl.program_id(1)
    row0 = pl.multiple_of(i * rt, rt)
    span = ATT_SPAN

    u = _rms(x_ref[...], g_ref[...]).astype(BF16)
    qkv = jnp.concatenate([_dot(u, w_ref[...]) for w_ref in w_refs], axis=1)
    width = n_slab * LANES
    e4 = e4_ref[...]
    for s in range(n_slab):
        cs = slice(s * LANES, (s + 1) * LANES)
        qk = jnp.concatenate([qkv[:, cs], qkv[:, width + s * LANES: width + (s + 1) * LANES]], axis=1)
        gain = jnp.concatenate([gq_ref[:, cs] * (ATT_HEAD_DIM ** -0.5), gk_ref[:, cs]], axis=1)
        ms = _dot((qk * qk).astype(BF16), e4)
        qk = qk * lax.rsqrt(ms + EPS) * gain
        qkv_scr[s, pl.ds(row0, rt), :] = qk[:, :LANES]
        qkv_scr[n_slab + s, pl.ds(row0, rt), :] = qk[:, LANES:]
        qkv_scr[2 * n_slab + s, pl.ds(row0, rt), :] = qkv[:, 2 * width + s * LANES: 2 * width + (s + 1) * LANES]

    @pl.when(i == pl.num_programs(1) - 1)
    def _():
        lane = lax.broadcasted_iota(jnp.int32, (span, LANES), 1)
        first_head = lane < ATT_HEAD_DIM

        for g, (window, dil) in enumerate(ATT_GROUPS):
            assert window // dil == span
            nblk = t_total // dil // span
            blocks = [(r, n) for n in range(nblk) for r in range(dil)]

            def rows(start, dil=dil):
                return pl.ds(start, span) if dil == 1 else pl.ds(start, span, stride=dil)

            for b0 in range(0, len(blocks), ATT_BLOCK_BATCH):
                probs = [(r + n * span * dil, n > 0, 2 * g + sl)
                         for r, n in blocks[b0:b0 + ATT_BLOCK_BATCH] for sl in range(2)]
                qs, kk, vv = [], [], []
                for start, has_prev, slab in probs:
                    q2 = qkv_scr[slab, rows(start), :]
                    qs.append(jnp.concatenate([jnp.where(first_head, q2, 0.0), jnp.where(first_head, 0.0, q2)],
                                              axis=0).astype(BF16))
                    kc = qkv_scr[n_slab + slab, rows(start), :]
                    vc = qkv_scr[2 * n_slab + slab, rows(start), :]
                    if has_prev:
                        prev = start - span * dil
                        kc = jnp.concatenate([qkv_scr[n_slab + slab, rows(prev), :], kc], axis=0)
                        vc = jnp.concatenate([qkv_scr[2 * n_slab + slab, rows(prev), :], vc], axis=0)
                    kk.append(kc.astype(BF16))
                    vv.append(vc.astype(BF16))
                sc = [_dot_nt(q_, k_) for q_, k_ in zip(qs, kk)]
                sc = [s_ + (mb_ref[slab] if has_prev else mb_ref[slab, :, span:])
                      for s_, (_, has_prev, slab) in zip(sc, probs)]
                mx = [jnp.max(s_, axis=-1, keepdims=True) for s_ in sc]
                pr = [jnp.exp(s_ - m_) for s_, m_ in zip(sc, mx)]
                den = [jnp.sum(p_, axis=-1, keepdims=True) for p_ in pr]
                pv = [_dot(p_.astype(BF16), v_) for p_, v_ in zip(pr, vv)]
                for (start, _, slab), pv_, m_, d_ in zip(probs, pv, mx, den):
                    on = pv_ * (1.0 / d_)
                    lse = jnp.broadcast_to(m_ + jnp.log(d_), on.shape)
                    o_scr[slab, rows(start), :] = jnp.where(first_head, on[:span], on[span:])
                    lse_scr[slab, rows(start), :] = jnp.where(first_head, lse[:span], lse[span:])

        n_grp = len(ATT_GROUPS)
        ct = 256

        def combine(j, carry):
            rws = pl.ds(pl.multiple_of(j * ct, ct), ct)
            for sl in range(2):
                ls = [lse_scr[2 * g + sl, rws, :] for g in range(n_grp)]
                mx = functools.reduce(jnp.maximum, ls)
                es = [jnp.exp(l - mx) for l in ls]
                inv = 1.0 / functools.reduce(lambda a, b: a + b, es)
                for g in range(n_grp):
                    slab = 2 * g + sl
                    o_ref[rws, slab * LANES:(slab + 1) * LANES] = (o_scr[slab, rws, :] * (es[g] * inv)).astype(BF16)
            return carry

        lax.fori_loop(0, t_total // ct, combine, 0)


def _att_branch(x, g, w_in, layer, w_off, gq, gk, e4, maskbias, *, rt=512):
    bsz, t, d = x.shape
    width = gq.shape[1]
    n_slab = width // LANES
    assert w_off % MXU_DIM == 0 and (3 * width) % MXU_DIM == 0
    n_w = 3 * width // MXU_DIM
    return pl.pallas_call(
        functools.partial(_att_kernel, rt=rt, n_w=n_w),
        grid=(bsz, t // rt),
        in_specs=[pl.BlockSpec((None, rt, d), lambda b, i: (b, i, 0)), _const_spec((1, d))]
                 + [_col_block_spec(d, MXU_DIM, w_off // MXU_DIM + j, layer, True) for j in range(n_w)]
                 + [_const_spec((1, width)), _const_spec((1, width)),
                    _const_spec(e4.shape), _const_spec(maskbias.shape, True)],
        out_specs=pl.BlockSpec((None, t, width), lambda b, i: (b, 0, 0)),
        out_shape=jax.ShapeDtypeStruct((bsz, t, width), BF16),
        scratch_shapes=[pltpu.VMEM((3 * n_slab, t, LANES), F32), pltpu.VMEM((n_slab, t, LANES), F32),
                        pltpu.VMEM((n_slab, t, LANES), F32)],
        compiler_params=_params(("arbitrary", "arbitrary")),
        name="att_branch",
    )(x, g, *([w_in] * n_w), gq, gk, e4, maskbias)


def _att_maskbias(n_heads):
    slopes = jnp.exp2(-8.0 * jnp.arange(1, n_heads + 1, dtype=F32) / n_heads)
    qi = jnp.arange(ATT_SPAN)[:, None]
    kj = jnp.arange(2 * ATT_SPAN)[None, :]
    delta = qi + ATT_SPAN - kj
    valid = (delta >= 0) & (delta <= ATT_SPAN)
    dil = jnp.repeat(jnp.array([d for _, d in ATT_GROUPS], F32), ATT_HEADS_PER_GROUP)
    bias = -(slopes * dil)[:, None, None] * delta.astype(F32)[None]
    return jnp.where(valid[None], bias, NEG_INF).reshape(n_heads // 2, 2 * ATT_SPAN, 2 * ATT_SPAN)


def _head_mean_matrix(n_lanes, head_dim):
    lane_head = jnp.arange(n_lanes) // head_dim
    return ((lane_head[:, None] == lane_head[None, :]).astype(F32) / head_dim).astype(BF16)


def _cumsum_rows(x):
    n = x.shape[0]
    rows = lax.broadcasted_iota(jnp.int32, x.shape, 0)
    s = 1
    while s < n:
        x = x + jnp.where(rows >= s, pltpu.roll(x, s, 0), 0.0)
        s *= 2
    return x


def _dn_kernel(*refs, tiles_per_seq):
    tail_ref, s_scr, *stage_scratch = refs[8:]
    step = pl.program_id(0)

    @pl.when(step == 0)
    def _():
        for ref in stage_scratch:
            ref[...] = jnp.zeros_like(ref)

    @pl.when(step % tiles_per_seq == 0)
    def _():
        tail_ref[...] = jnp.zeros_like(tail_ref)

    @pl.when((step % tiles_per_seq == 1) | (step == 0))
    def _():
        s_scr[...] = jnp.zeros_like(s_scr)

    for parity in range(2):
        pl.when(step % 2 == parity)(functools.partial(_dn_body, *refs, rd=1 - parity, wr=parity))


def _dn_body(x_ref, g_ref, w_ref, cw_ref, aexp_ref, dtb_ref, on_ref, o_ref, tail_ref, s_scr,
             kbq_s, k_s, rhs_s, kd_s, qeg_s, gam_s, gamt_s, z_s, *, rd, wr):
    tt, width = o_ref.shape
    n_heads = width // DN_HEAD_DIM
    c = DN_CHUNK

    u = _rms(x_ref[...], g_ref[...]).astype(BF16)
    parts = {}

    def project(*idx):
        for j in idx:
            parts[j] = _dot(u, w_ref[:, j * MXU_DIM:(j + 1) * MXU_DIM])

    assert 2 * c == LANES and n_heads % 4 == 0
    n_ch = tt // c
    n_pair = n_heads // 2
    row2 = lax.broadcasted_iota(jnp.int32, (c, LANES), 0)
    lane2 = lax.broadcasted_iota(jnp.int32, (c, LANES), 1)
    col2 = lane2 & (c - 1)
    incl = col2 <= row2
    strict = col2 < row2
    lo2 = lane2 < c
    lane2x = lax.broadcasted_iota(jnp.int32, (LANES, LANES), 1)
    lo2x = lane2x < c
    row4 = lax.broadcasted_iota(jnp.int32, (c, 2 * LANES), 0)
    lane4 = lax.broadcasted_iota(jnp.int32, (c, 2 * LANES), 1)
    eye4 = ((lane4 & (c - 1)) == row4).astype(F32)
    blk4 = lane4 // c
    zero_k = jnp.zeros((c, DN_HEAD_DIM), BF16)

    def block_diag(m):
        return jnp.concatenate([jnp.where(blk4 == i, m, 0.0) for i in range(4)], axis=0).astype(BF16)

    def l2n(m):
        return m * lax.rsqrt(jnp.sum(m * m, axis=-1, keepdims=True) + EPS)

    probs = [(ch, h) for ch in range(n_ch) for h in range(n_heads)]
    pairs = [(ch, p) for ch in range(n_ch) for p in range(n_pair)]
    quads = [(ch, qd) for ch in range(n_ch) for qd in range(n_pair // 2)]
    pix = {ph: i for i, ph in enumerate(probs)}

    n_prob = len(probs)
    common = {}
    ba_piece = 4 * n_pair

    def prep_common():
        ba = parts[ba_piece][:, :LANES]
        common["beta"] = jax.nn.sigmoid(ba)
        glog_all = -aexp_ref[...] * _softplus(ba + dtb_ref[...])
        common["gam"] = [_cumsum_rows(glog_all[ch * c:(ch + 1) * c]) for ch in range(n_ch)]
        for ch, g_ in enumerate(common["gam"]):
            gam_s[wr, ch] = g_
            gamt_s[wr, ch] = jnp.concatenate([g_, g_], axis=0).T

    def prep_pair(p):
        def conv_silu(section):
            blk = parts[section * n_pair + p]
            cs = slice(section * width + p * MXU_DIM, section * width + (p + 1) * MXU_DIM)
            y = _causal_conv(blk, tail_ref[:, cs], cw_ref[:, cs])
            tail_ref[:, cs] = blk[tt - SUBLANES:tt]
            return y * jax.nn.sigmoid(y)

        qs, ks, vs = conv_silu(0), conv_silu(1), conv_silu(2)
        for ch in range(n_ch):
            rs = slice(ch * c, (ch + 1) * c)
            for sub in range(2):
                h = 2 * p + sub
                ls = slice(sub * DN_HEAD_DIM, (sub + 1) * DN_HEAD_DIM)
                q_ = l2n(qs[rs, ls]) * (DN_HEAD_DIM ** -0.5)
                k_ = l2n(ks[rs, ls])
                beta = common["beta"][rs, h:h + 1]
                gc = common["gam"][ch][:, n_heads + h:n_heads + h + 1]
                eg_ = jnp.exp(gc)
                kb_ = k_ * beta
                i = pix[(ch, h)]
                rhs_s[wr, i] = jnp.concatenate([vs[rs, ls] * beta, kb_ * eg_], axis=1).astype(BF16)
                kbq_s[wr, i] = jnp.concatenate([kb_, q_], axis=0).astype(BF16)
                kd_s[wr, i] = k_ * jnp.exp(gc[c - 1:c, :] - gc)
                qeg_s[wr, i] = (q_ * eg_).astype(BF16)
                k_s[wr, i] = k_.astype(BF16)

    fill = iter([
        lambda: project(ba_piece, 0, n_pair, 2 * n_pair),
        lambda: (prep_common(), project(1, n_pair + 1, 2 * n_pair + 1)),
        lambda: prep_pair(0),
        lambda: project(2, n_pair + 2, 2 * n_pair + 2),
        lambda: prep_pair(1),
        lambda: project(3, n_pair + 3, 2 * n_pair + 3),
        lambda: prep_pair(2),
        lambda: project(3 * n_pair, 3 * n_pair + 1),
        lambda: prep_pair(3),
        lambda: project(3 * n_pair + 2),
        lambda: project(3 * n_pair + 3),
    ])
    assert n_pair == 4

    def interleave():
        action = next(fill, None)
        if action is not None:
            action()

    kbq = [kbq_s[rd, i] for i in range(n_prob)]
    k = [k_s[rd, i] for i in range(n_prob)]
    rhs = [rhs_s[rd, i] for i in range(n_prob)]
    kd = [kd_s[rd, i] for i in range(n_prob)]
    qeg = [qeg_s[rd, i] for i in range(n_prob)]
    gam_all = [gam_s[rd, ch] for ch in range(n_ch)]
    gam_t = [gamt_s[rd, ch] for ch in range(n_ch)]
    z = z_s[rd]
    gcol = [gam_all[ch][:, n_heads + h:n_heads + h + 1] for ch, h in probs]
    glast = [gc[c - 1:c, :] for gc in gcol]

    def both(lst, ch, p):
        return lst[pix[(ch, 2 * p)]], lst[pix[(ch, 2 * p + 1)]]

    interleave()
    kq = []
    for ch, p in pairs:
        (x0, x1), (k0, k1) = both(kbq, ch, p), both(k, ch, p)
        kq.append(_dot_nt(x0, jnp.concatenate([k0.astype(BF16), zero_k], axis=0))
                  + _dot_nt(x1, jnp.concatenate([zero_k, k1.astype(BF16)], axis=0)))
    interleave()
    decay = []
    for ch, p in pairs:
        g0, g1 = both(gcol, ch, p)
        gr = jnp.where(lo2[0:1], gam_t[ch][n_heads + 2 * p:n_heads + 2 * p + 1, :],
                       gam_t[ch][n_heads + 2 * p + 1:n_heads + 2 * p + 2, :])
        diff = jnp.where(lo2, g0, g1) - gr
        decay.append(jnp.where(incl, jnp.exp(jnp.where(incl, diff, 0.0)), 0.0))
    a2 = [jnp.where(strict, kq_[:c] * d_, 0.0) for kq_, d_ in zip(kq, decay)]
    qk2 = [jnp.where(incl, kq_[c:] * d_, 0.0) for kq_, d_ in zip(kq, decay)]

    a4 = [jnp.concatenate([a2[ch * n_pair + 2 * qd], a2[ch * n_pair + 2 * qd + 1]], axis=1) for ch, qd in quads]
    tinv = [eye4 - a_ for a_ in a4]
    pw = [_dot(a_.astype(BF16), block_diag(a_)) for a_ in a4]
    interleave()
    for _ in range(int(math.log2(c)) - 2):
        r = [_dot(jnp.concatenate([t_, p_], axis=0).astype(BF16), block_diag(p_)) for t_, p_ in zip(tinv, pw)]
        tinv = [t_ + r_[:c] for t_, r_ in zip(tinv, r)]
        pw = [r_[c:] for r_ in r]
        interleave()
    tinv = [t_ + _dot(t_.astype(BF16), block_diag(p_)) for t_, p_ in zip(tinv, pw)]
    interleave()

    def split_rows(m):
        return jnp.concatenate([jnp.where(lo2, m, 0.0), jnp.where(lo2, 0.0, m)], axis=0).astype(BF16)

    sol, lhs_o = [], []
    for i, (ch, p) in enumerate(pairs):
        t2 = tinv[i // 2][:, (i % 2) * LANES:(i % 2 + 1) * LANES]
        r0, r1 = both(rhs, ch, p)
        sol.append(_dot(split_rows(t2), jnp.concatenate([r0, r1], axis=0)))
        kd0, kd1 = both(kd, ch, p)
        kdt = jnp.concatenate([kd0, kd1], axis=0).T
        lhs_o.append(jnp.concatenate([split_rows(qk2[i]), jnp.where(lo2x, kdt, 0.0).astype(BF16),
                                      jnp.where(lo2x, 0.0, kdt).astype(BF16)], axis=0))
    wq = []
    for ch, h in probs:
        s_ = sol[ch * n_pair + h // 2][(h % 2) * c:(h % 2 + 1) * c]
        i = pix[(ch, h)]
        wq.append(jnp.concatenate([s_[:, DN_HEAD_DIM:].astype(BF16), qeg[i]], axis=0))
    interleave()

    for ch in range(n_ch):
        rs = slice(ch * c, (ch + 1) * c)
        state = [s_scr[h] for h in range(n_heads)]
        ws = [_dot(wq[pix[(ch, h)]], state[h].astype(BF16)) for h in range(n_heads)]
        interleave()
        un = [sol[ch * n_pair + h // 2][(h % 2) * c:(h % 2 + 1) * c, :DN_HEAD_DIM] - ws[h][:c] for h in range(n_heads)]
        ou = [_dot(lhs_o[ch * n_pair + p], jnp.concatenate([un[2 * p], un[2 * p + 1]], axis=0).astype(BF16))
              for p in range(n_pair)]
        interleave()
        for h in range(n_heads):
            hs = slice(h * DN_HEAD_DIM, (h + 1) * DN_HEAD_DIM)
            o2 = ou[h // 2]
            sub = h % 2
            s_scr[h] = (state[h] * jnp.exp(glast[pix[(ch, h)]])
                        + o2[2 * c + sub * DN_HEAD_DIM:2 * c + (sub + 1) * DN_HEAD_DIM])
            on = _rms(ws[h][c:] + o2[sub * c:(sub + 1) * c], on_ref[:, hs])
            zz = z[rs, hs]
            o_ref[rs, hs] = (on * (zz * jax.nn.sigmoid(zz))).astype(BF16)

    for action in fill:
        action()
    z_s[wr] = jnp.concatenate([parts[3 * n_pair + j] for j in range(n_pair)], axis=1)


def _dn_branch(x, g, w_in, layer, w_block, cw, aexp, dtb, onorm, *, tt=256):
    bsz, t, d = x.shape
    width = onorm.shape[1]
    n_heads = width // DN_HEAD_DIM
    w_cols = 4 * width + 2 * LANES
    tiles_per_seq = t // tt
    n_tiles = bsz * tiles_per_seq
    n_prob = (tt // DN_CHUNK) * n_heads
    c = DN_CHUNK

    def x_map(s):
        j = jnp.minimum(s, n_tiles - 1)
        return j // tiles_per_seq, j % tiles_per_seq, 0

    def o_map(s):
        j = jnp.maximum(s - 1, 0)
        return j // tiles_per_seq, j % tiles_per_seq, 0

    assert tiles_per_seq > 1
    return pl.pallas_call(
        functools.partial(_dn_kernel, tiles_per_seq=tiles_per_seq),
        grid=(n_tiles + 1,),
        in_specs=[pl.BlockSpec((None, tt, d), x_map), _const_spec((1, d)),
                  _col_block_spec(d, w_cols, w_block, layer), _const_spec(cw.shape),
                  _const_spec((1, LANES)), _const_spec((1, LANES)), _const_spec((1, width))],
        out_specs=pl.BlockSpec((None, tt, width), o_map),
        out_shape=jax.ShapeDtypeStruct((bsz, t, width), BF16),
        scratch_shapes=[pltpu.VMEM((SUBLANES, 3 * width), F32),
                        pltpu.VMEM((n_heads, DN_HEAD_DIM, DN_HEAD_DIM), F32),
                        pltpu.VMEM((2, n_prob, 2 * c, DN_HEAD_DIM), BF16),
                        pltpu.VMEM((2, n_prob, c, DN_HEAD_DIM), BF16),
                        pltpu.VMEM((2, n_prob, c, 2 * DN_HEAD_DIM), BF16),
                        pltpu.VMEM((2, n_prob, c, DN_HEAD_DIM), F32),
                        pltpu.VMEM((2, n_prob, c, DN_HEAD_DIM), BF16),
                        pltpu.VMEM((2, tt // c, c, LANES), F32),
                        pltpu.VMEM((2, tt // c, LANES, LANES), F32),
                        pltpu.VMEM((2, tt, width), F32)],
        compiler_params=_params(("arbitrary",)),
        name="dn_branch",
    )(x, g, w_in, cw, aexp, dtb, onorm)


def _merge_ffn_kernel(x_ref, gm_ref, ya_ref, yb_ref, yc_ref, wm_ref, wa_ref, wb_ref, wc_ref, wo_ref,
                      gf_ref, wg_ref, wu_ref, wd_ref, *rest, f_chunk):
    n_cast = (len(rest) - 1) // 2
    o_ref = rest[n_cast]
    for src, dst in zip(rest[:n_cast], rest[n_cast + 1:]):
        dst[...] = src[...].astype(BF16)
    x = x_ref[...]
    d = x.shape[1]
    u = _rms(x, gm_ref[...]).astype(BF16)
    y = jnp.zeros_like(x)
    for j, (yr, wr) in enumerate(((ya_ref, wa_ref), (yb_ref, wb_ref), (yc_ref, wc_ref))):
        gate = jax.nn.sigmoid(_dot(u, wm_ref[:, j * d:(j + 1) * d]))
        y = y + gate * _dot(yr[...], wr[...])
    x = x + _dot(y.astype(BF16), wo_ref[...])
    u = _rms(x, gf_ref[...]).astype(BF16)
    acc = jnp.zeros_like(x)
    for c in range(wg_ref.shape[1] // f_chunk):
        sl = slice(c * f_chunk, (c + 1) * f_chunk)
        hg = _dot(u, wg_ref[:, sl])
        hu = _dot(u, wu_ref[:, sl])
        h = (hg * jax.nn.sigmoid(hg) * hu).astype(BF16)
        acc = acc + _dot(h, wd_ref[sl, :])
    o_ref[...] = x + 0.5 * acc


def _merge_ffn(x2, gm, ya, yb, yc, wm, wa, wb, wc, wo, gf, wg, wu, wd, casts=(), *, tm=512, f_chunk=256):
    n, d = x2.shape
    n_steps = n // tm
    row = lambda i: (i, 0)
    cast_in, cast_out, cast_shapes = _cast_specs(casts, n_steps)
    resident = [_const_spec(w.shape, True) for w in (wm, wa, wb, wc, wo)]
    outs = pl.pallas_call(
        functools.partial(_merge_ffn_kernel, f_chunk=f_chunk),
        grid=(n_steps,),
        in_specs=[pl.BlockSpec((tm, d), row), _const_spec((1, d)),
                  pl.BlockSpec((tm, ya.shape[1]), row), pl.BlockSpec((tm, yb.shape[1]), row),
                  pl.BlockSpec((tm, yc.shape[1]), row)] + resident
                 + [_const_spec((1, d))] + [_const_spec(w.shape, True) for w in (wg, wu, wd)] + cast_in,
        out_specs=[pl.BlockSpec((tm, d), row)] + cast_out,
        out_shape=[jax.ShapeDtypeStruct((n, d), F32)] + cast_shapes,
        compiler_params=_params(("arbitrary",)),
        name="merge_ffn",
    )(x2, gm, ya, yb, yc, wm, wa, wb, wc, wo, gf, wg, wu, wd, *[stack for stack, _ in casts])
    return outs[0], list(outs[1:])


def _block_diag_tiles(w):
    nb, k, _ = w.shape
    per = MXU_DIM // k
    w4 = w.reshape(nb // per, per, k, k)
    eye = jnp.eye(per, dtype=w.dtype)
    return jnp.einsum('tpij,pq->tpiqj', w4, eye).reshape(nb // per, MXU_DIM, MXU_DIM)


def _row(v):
    return v.reshape(1, -1).astype(F32)


def _mixer_ffn(x, layer, mix_norm, w_in, rg_conv_w, rg_conv_b, rg_w_r, rg_b_r, rg_w_i, rg_b_i, rg_lambda,
               att_q_norm, att_k_norm, dn_conv_w, dn_a_log, dn_dt_bias, dn_out_norm, w_branch, w_out,
               ffn_norm, ffn_w, casts):
    bsz, t, d = x.shape
    rg_w = rg_conv_w.shape[1]
    att_w = att_q_norm.size
    dn_w = dn_out_norm.size
    dn_h = dn_out_norm.shape[0]
    att_off = 2 * rg_w
    dn_off = att_off + 3 * att_w
    merge_off = dn_off + 4 * dn_w + 2 * dn_h
    dn_cols = 4 * dn_w + 2 * LANES
    assert dn_off % dn_cols == 0 and dn_off + dn_cols <= w_in.shape[2] and 2 * dn_h <= LANES
    w_merge = w_in[layer, :, merge_off:merge_off + 3 * d]
    g = _row(mix_norm)

    lamc = _row(-RG_C * jax.nn.softplus(-rg_lambda.astype(F32)))
    ya = _rg_branch(x, g, w_in, layer, rg_conv_w, _row(rg_conv_b),
                    _block_diag_tiles(rg_w_r).astype(BF16), _row(rg_b_r),
                    _block_diag_tiles(rg_w_i).astype(BF16), _row(rg_b_i), lamc)

    n_att_heads = att_q_norm.shape[0]
    yb = _att_branch(x, g, w_in, layer, att_off, _row(att_q_norm), _row(att_k_norm),
                     _head_mean_matrix(2 * LANES, ATT_HEAD_DIM), _att_maskbias(n_att_heads))

    pad = jnp.zeros((LANES - 2 * dn_h,), F32)
    aexp = _row(jnp.concatenate([jnp.zeros((dn_h,), F32), jnp.exp(dn_a_log.astype(F32)), pad]))
    dtb = _row(jnp.concatenate([jnp.zeros((dn_h,), F32), dn_dt_bias.astype(F32), pad]))
    yc = _dn_branch(x, g, w_in, layer, dn_off // dn_cols, dn_conv_w, aexp, dtb, _row(dn_out_norm))

    wa = w_branch[:rg_w]
    wb = w_branch[rg_w:rg_w + att_w]
    wc = w_branch[rg_w + att_w:]
    n = bsz * t
    out, cast = _merge_ffn(x.reshape(n, d), g, ya.reshape(n, rg_w), yb.reshape(n, att_w), yc.reshape(n, dn_w),
                           w_merge, wa, wb, wc, w_out, _row(ffn_norm), *ffn_w, casts)
    return out.reshape(bsz, t, d), cast


def kernel(x, ffn1_norm, ffn1_w_gate, ffn1_w_up, ffn1_w_down, mix_norm, w_in, rg_conv_w, rg_conv_b, rg_w_r, rg_b_r, rg_w_i, rg_b_i, rg_lambda, att_q_norm, att_k_norm, dn_conv_w, dn_a_log, dn_dt_bias, dn_out_norm, w_branch, w_out, ffn2_norm, ffn2_w_gate, ffn2_w_up, ffn2_w_down):
    bsz, t, d = x.shape
    n = bsz * t
    n_layers = ffn1_norm.shape[0]
    ffn1_w = [w[0].astype(BF16) for w in (ffn1_w_gate, ffn1_w_up, ffn1_w_down)]
    w_in_all = w_in.astype(BF16)
    for l in range(n_layers):
        casts = [(w, l) for w in (w_branch, w_out, ffn2_w_gate, ffn2_w_up, ffn2_w_down)]
        x, (w_branch_b, w_out_b, *ffn2_w) = _ffn(x.reshape(n, d), _row(ffn1_norm[l]), *ffn1_w, casts)
        casts = [(w, l + 1) for w in (ffn1_w_gate, ffn1_w_up, ffn1_w_down)] if l + 1 < n_layers else []
        x, ffn1_w = _mixer_ffn(x.reshape(bsz, t, d), l, mix_norm[l], w_in_all, rg_conv_w[l], rg_conv_b[l], rg_w_r[l],
                               rg_b_r[l], rg_w_i[l], rg_b_i[l], rg_lambda[l], att_q_norm[l], att_k_norm[l],
                               dn_conv_w[l], dn_a_log[l], dn_dt_bias[l], dn_out_norm[l], w_branch_b, w_out_b,
                               ffn2_norm[l], ffn2_w, casts)
    return x
```
